```python
import math
import jax
import jax.numpy as jnp
from jax import lax
import numpy as np

D_MODEL = 2048
BATCH = 4
SEQ = 2048
DEPTH = 4
DEC_BATCH = 8
DEC_SEQ = 4
PAST_LEN = 16384
PAGE_SIZE = 128

HEAD_DIM = 128
H_PER_GROUP = 8
ATT_GROUPS = ((128, 1), (512, 4), (2048, 16))
N_GROUPS = 3
N_ATT_HEADS = N_GROUPS * H_PER_GROUP
QKV_WIDTH = N_ATT_HEADS * HEAD_DIM
ATT_WIDTH = H_PER_GROUP * HEAD_DIM
N_BUCKETS = 32
REL_MAX_DIST = 2048
BLK = 128
LRU_WIDTH = D_MODEL
LRU_BLOCKS = 16
LRU_BLOCK = LRU_WIDTH // LRU_BLOCKS
CONV_WIDTH = 4
LRU_C = 8.0
IN_COLS = 3 * QKV_WIDTH + ATT_WIDTH + 2 * LRU_WIDTH + 2 * D_MODEL
RMS_EPS = 1e-6
NEG = -1e30

kernel_name = "hybrid_dilated_attn_rglru_decoder_step"


def _rmsnorm(x, g):
    xf = x.astype(jnp.float32)
    y = xf * lax.rsqrt(jnp.mean(xf * xf, axis=-1, keepdims=True) + RMS_EPS)
    return (y * g.astype(jnp.float32)).astype(x.dtype)


def _t5_bucket(dist):
    dist = np.asarray(dist).astype(np.int32)
    max_exact = N_BUCKETS // 2
    safe = np.maximum(dist, 1).astype(np.float32)
    large = max_exact + (np.log(safe / max_exact) / np.float32(math.log(REL_MAX_DIST / max_exact))
                         * (N_BUCKETS - max_exact)).astype(np.int32)
    large = np.minimum(large, N_BUCKETS - 1)
    return np.where(dist < max_exact, dist, large).astype(np.int32)


def _band_attn_prompt(q, k, v, tbl, window, dil):
    B, T, H, Dh = q.shape
    J = window // dil
    span = BLK * dil
    Tp = -(-T // span) * span
    NB = Tp // span

    def blocks(a):
        a = jnp.pad(a, ((0, 0), (0, Tp - T), (0, 0), (0, 0)))
        a = a.reshape(B, NB, BLK, dil, H, Dh)
        return a.transpose(0, 3, 4, 1, 2, 5)

    def with_prev(a):
        prev = jnp.pad(a, ((0, 0), (0, 0), (0, 0), (1, 0), (0, 0), (0, 0)))[:, :, :, :-1]
        return jnp.concatenate([prev, a], axis=4)

    qb = blocks(q)
    kk = with_prev(blocks(k))
    vv = with_prev(blocks(v))
    qi = np.arange(BLK)[:, None]
    ki = np.arange(2 * BLK)[None, :]
    du = qi + BLK - ki
    band = (du >= 0) & (du <= J)
    nb = np.arange(NB)[:, None, None]
    valid = band[None] & (nb * BLK + ki[None] - BLK >= 0)
    bias = tbl[_t5_bucket(np.clip(du, 0, None) * dil)]
    bias = jnp.transpose(bias, (2, 0, 1)).astype(jnp.float32)
    logits = jnp.einsum('brhnqe,brhnke->brhnqk', qb, kk,
                        preferred_element_type=jnp.float32) * (HEAD_DIM ** -0.5)
    logits = jnp.where(valid[None, None, None], logits + bias[None, None, :, None], NEG)
    m = logits.max(axis=-1)
    p = jnp.exp(logits - m[..., None])
    s = p.sum(axis=-1)
    acc = jnp.einsum('brhnqk,brhnke->brhnqe', p, vv.astype(jnp.float32))
    acc = acc.transpose(0, 3, 4, 1, 2, 5).reshape(B, Tp, H, Dh)[:, :T]
    m = m.transpose(0, 3, 4, 1, 2).reshape(B, Tp, H)[:, :T]
    s = s.transpose(0, 3, 4, 1, 2).reshape(B, Tp, H)[:, :T]
    return acc, m, s


def _dilated_attn_step(q, k, v, kv_buf, tbl, window, dil):
    S = q.shape[1]
    Wb = kv_buf.shape[1]
    J = window // dil
    ext = jnp.concatenate([kv_buf, jnp.stack([k, v], axis=2).astype(kv_buf.dtype)], axis=1)
    j = np.arange(J + 1)
    idx = Wb + np.arange(S)[:, None] - j[None, :] * dil
    valid = idx >= 0
    g = ext[:, np.clip(idx, 0, None)]
    bias = tbl[_t5_bucket(j * dil)].astype(jnp.float32)
    logits = jnp.einsum('bshe,bsjhe->bshj', q, g[:, :, :, 0],
                        preferred_element_type=jnp.float32) * (HEAD_DIM ** -0.5)
    logits = jnp.where(valid[None, :, None, :], logits + bias.T[None, None], NEG)
    m = logits.max(axis=-1)
    p = jnp.exp(logits - m[..., None])
    s = p.sum(axis=-1)
    acc = jnp.einsum('bshj,bsjhe->bshe', p, g[:, :, :, 1].astype(jnp.float32))
    return acc, m, s, ext[:, S:]


def _merge_groups(accs, ms, ss):
    acc = jnp.stack(accs)
    m = jnp.stack(ms)
    s = jnp.stack(ss)
    w = jnp.exp(m - m.max(axis=0, keepdims=True))
    return (w[..., None] * acc).sum(axis=0) / (w * s).sum(axis=0)[..., None]


def _causal_conv(xb, buf, w, b):
    T = xb.shape[1]
    xp = jnp.concatenate([buf.astype(xb.dtype), xb], axis=1)
    y = xp[:, 0:T] * w[0]
    for tap in range(1, CONV_WIDTH):
        y = y + xp[:, tap:tap + T] * w[tap]
    return y + b, xp[:, T:]


def _rglru(x, h0, w_r, b_r, w_i, b_i, lam):
    B, T, C = x.shape
    xf = x.astype(jnp.float32)
    xblk = xf.reshape(B, T, LRU_BLOCKS, LRU_BLOCK)
    r = jax.nn.sigmoid(jnp.einsum('btnc,ncd->btnd', xblk, w_r.astype(jnp.float32)).reshape(B, T, C) + b_r)
    i = jax.nn.sigmoid(jnp.einsum('btnc,ncd->btnd', xblk, w_i.astype(jnp.float32)).reshape(B, T, C) + b_i)
    log_a = -LRU_C * r * jax.nn.softplus(-lam.astype(jnp.float32))
    a = jnp.exp(log_a)
    u = jnp.sqrt(-jnp.expm1(2.0 * log_a)) * (i * xf)
    u = u.at[:, 0].add(a[:, 0] * h0.astype(jnp.float32))

    def comb(l, rr):
        return (l[0] * rr[0], rr[0] * l[1] + rr[1])

    _, h = lax.associative_scan(comb, (a, u), axis=1)
    return h, h[:, -1]


def _layer(x, c, kv_bufs, conv_buf, h0, rel_bias, w_ada, b_ada, norm_g, w_in, conv_w, conv_b,
           w_r, b_r, w_i, b_i, lam, w_pa, w_pb, w_out):
    B, T, _ = x.shape
    mod = jnp.einsum('bd,de->be', jax.nn.silu(c), w_ada) + b_ada
    shift, scale, gate = jnp.split(mod, 3, axis=-1)
    h = _rmsnorm(x, norm_g) * (1 + scale[:, None]) + shift[:, None]
    proj = jnp.einsum('btd,de->bte', h, w_in)
    cuts = np.cumsum([QKV_WIDTH, QKV_WIDTH, QKV_WIDTH, ATT_WIDTH, LRU_WIDTH, LRU_WIDTH, D_MODEL]).tolist()
    q, k, v, z_att, x_lru, z_lru, g_att, g_lru = jnp.split(proj, cuts, axis=-1)
    heads = (B, T, N_GROUPS, H_PER_GROUP, HEAD_DIM)
    q, k, v = q.reshape(heads), k.reshape(heads), v.reshape(heads)
    accs, ms, ss, new_kv = [], [], [], []
    for gi, (win, dil) in enumerate(ATT_GROUPS):
        tbl = rel_bias[:, gi * H_PER_GROUP:(gi + 1) * H_PER_GROUP]
        qg, kg, vg = q[:, :, gi], k[:, :, gi], v[:, :, gi]
        if kv_bufs is None:
            acc, m, s = _band_attn_prompt(qg, kg, vg, tbl, win, dil)
            new = jnp.stack([kg, vg], axis=2)[:, T - min(win, T):]
        else:
            acc, m, s, new = _dilated_attn_step(qg, kg, vg, kv_bufs[gi], tbl, win, dil)
        accs.append(acc)
        ms.append(m)
        ss.append(s)
        new_kv.append(new)
    att = _merge_groups(accs, ms, ss).reshape(B, T, ATT_WIDTH).astype(x.dtype)
    y_att = jnp.einsum('bte,ed->btd', att * jax.nn.silu(z_att), w_pa)
    if conv_buf is None:
        conv_buf = jnp.zeros((B, CONV_WIDTH - 1, LRU_WIDTH), x.dtype)
        h0 = jnp.zeros((B, LRU_WIDTH), jnp.float32)
    xc, new_conv = _causal_conv(x_lru, conv_buf, conv_w, conv_b)
    hs, h_last = _rglru(xc, h0, w_r, b_r, w_i, b_i, lam)
    y_lru = jnp.einsum('bte,ed->btd', hs.astype(x.dtype) * jax.nn.silu(z_lru), w_pb)
    merged = jax.nn.sigmoid(g_att) * y_att + jax.nn.sigmoid(g_lru) * y_lru
    out = jnp.einsum('btd,de->bte', merged, w_out)
    return x + gate[:, None] * out, new_kv, new_conv, h_last


def setup_inputs(seed: int = 0) -> dict:
    key = jax.random.key(seed)
    ks = jax.random.split(key, 26)
    f32 = jnp.float32

    def nrm(k, shape, s):
        return jax.random.normal(k, shape, f32) * s

    (w0, _), (w1, _), (w2, _) = ATT_GROUPS
    u = jax.random.uniform(ks[20], (DEPTH, LRU_WIDTH), f32, 0.9, 0.999)
    sig = u ** (1.0 / LRU_C)
    lam = jnp.log(sig) - jnp.log1p(-sig)
    return {
        'x_prompt': nrm(ks[0], (BATCH, SEQ, D_MODEL), 1.0),
        'x_sample': nrm(ks[1], (DEC_BATCH, DEC_SEQ, D_MODEL), 1.0),
        'c_prompt': nrm(ks[2], (BATCH, D_MODEL), 1.0),
        'c_sample': nrm(ks[3], (DEC_BATCH, D_MODEL), 1.0),
        'cache_kv_g0': nrm(ks[4], (DEPTH, DEC_BATCH, min(w0, PAST_LEN), 2, H_PER_GROUP, HEAD_DIM), 1.0),
        'cache_kv_g1': nrm(ks[5], (DEPTH, DEC_BATCH, min(w1, PAST_LEN), 2, H_PER_GROUP, HEAD_DIM), 1.0),
        'cache_kv_g2': nrm(ks[6], (DEPTH, DEC_BATCH, min(w2, PAST_LEN), 2, H_PER_GROUP, HEAD_DIM), 1.0),
        'state_conv': nrm(ks[7], (DEPTH, DEC_BATCH, CONV_WIDTH - 1, LRU_WIDTH), 1.0),
        'state_h': nrm(ks[8], (DEPTH, DEC_BATCH, LRU_WIDTH), 0.5),
        'rel_bias': nrm(ks[9], (N_BUCKETS, N_ATT_HEADS), 0.3),
        'w_ada': nrm(ks[10], (DEPTH, D_MODEL, 3 * D_MODEL), 0.5 * D_MODEL ** -0.5),
        'b_ada': nrm(ks[11], (DEPTH, 3 * D_MODEL), 0.02),
        'norm_g': 1.0 + nrm(ks[12], (DEPTH, D_MODEL), 0.02),
        'w_in': nrm(ks[13], (DEPTH, D_MODEL, IN_COLS), D_MODEL ** -0.5),
        'conv_w': nrm(ks[14], (DEPTH, CONV_WIDTH, LRU_WIDTH), CONV_WIDTH ** -0.5),
        'conv_b': nrm(ks[15], (DEPTH, LRU_WIDTH), 0.02),
        'w_r': nrm(ks[16], (DEPTH, LRU_BLOCKS, LRU_BLOCK, LRU_BLOCK), LRU_BLOCK ** -0.5),
        'b_r': nrm(ks[17], (DEPTH, LRU_WIDTH), 0.1),
        'w_i': nrm(ks[18], (DEPTH, LRU_BLOCKS, LRU_BLOCK, LRU_BLOCK), LRU_BLOCK ** -0.5),
        'b_i': nrm(ks[19], (DEPTH, LRU_WIDTH), 0.1),
        'lam': lam,
        'w_pa': nrm(ks[21], (DEPTH, ATT_WIDTH, D_MODEL), ATT_WIDTH ** -0.5),
        'w_pb': nrm(ks[22], (DEPTH, LRU_WIDTH, D_MODEL), LRU_WIDTH ** -0.5),
        'w_out': nrm(ks[23], (DEPTH, D_MODEL, D_MODEL), D_MODEL ** -0.5),
        'final_g': 1.0 + nrm(ks[24], (D_MODEL,), 0.02),
    }


def reference(x_prompt, x_sample, c_prompt, c_sample, cache_kv_g0, cache_kv_g1, cache_kv_g2,
              state_conv, state_h, rel_bias, w_ada, b_ada, norm_g, w_in, conv_w, conv_b,
              w_r, b_r, w_i, b_i, lam, w_pa, w_pb, w_out, final_g):
    kv_p = ([], [], [])
    kv_s = ([], [], [])
    conv_p, conv_s, h_p, h_s = [], [], [], []
    xp, xs = x_prompt, x_sample
    for l in range(DEPTH):
        lw = (w_ada[l], b_ada[l], norm_g[l], w_in[l], conv_w[l], conv_b[l], w_r[l], b_r[l],
              w_i[l], b_i[l], lam[l], w_pa[l], w_pb[l], w_out[l])
        xp, kv, cv, hl = _layer(xp, c_prompt, None, None, None, rel_bias, *lw)
        for gi in range(N_GROUPS):
            kv_p[gi].append(kv[gi])
        conv_p.append(cv)
        h_p.append(hl)
        xs, kv, cv, hl = _layer(xs, c_sample, (cache_kv_g0[l], cache_kv_g1[l], cache_kv_g2[l]),
                                state_conv[l], state_h[l], rel_bias, *lw)
        for gi in range(N_GROUPS):
            kv_s[gi].append(kv[gi])
        conv_s.append(cv)
        h_s.append(hl)
    y_prompt = _rmsnorm(xp, final_g)
    y_sample = _rmsnorm(xs, final_g)
    kv_g0_prompt = jnp.stack(kv_p[0])
    kv_g1_prompt = jnp.stack(kv_p[1])
    kv_g2_prompt = jnp.stack(kv_p[2])
    conv_prompt = jnp.stack(conv_p)
    h_prompt = jnp.stack(h_p)
    kv_g0_sample = jnp.stack(kv_s[0])
    kv_g1_sample = jnp.stack(kv_s[1])
    kv_g2_sample = jnp.stack(kv_s[2])
    conv_sample = jnp.stack(conv_s)
    h_sample = jnp.stack(h_s)
    return (y_prompt, y_sample, kv_g0_prompt, kv_g1_prompt, kv_g2_prompt, conv_prompt, h_prompt,
            kv_g0_sample, kv_g1_sample, kv_g2_sample, conv_sample, h_sample)
```

```python
import functools
import math

import jax
import jax.numpy as jnp
import numpy as np
from jax import lax
from jax.experimental import pallas as pl
from jax.experimental.pallas import tpu as pltpu

D_MODEL = 2048
BATCH = 4
SEQ = 2048
DEPTH = 4
DEC_BATCH = 8
DEC_SEQ = 4
HEAD_DIM = 128
H_PER_GROUP = 8
ATT_GROUPS = ((128, 1), (512, 4), (2048, 16))
N_GROUPS = 3
N_ATT_HEADS = N_GROUPS * H_PER_GROUP
QKV_WIDTH = N_ATT_HEADS * HEAD_DIM
ATT_WIDTH = H_PER_GROUP * HEAD_DIM
N_BUCKETS = 32
REL_MAX_DIST = 2048
BLK = 128
LRU_WIDTH = D_MODEL
LRU_BLOCKS = 16
LRU_BLOCK = LRU_WIDTH // LRU_BLOCKS
CONV_WIDTH = 4
LRU_C = 8.0
IN_COLS = 3 * QKV_WIDTH + ATT_WIDTH + 2 * LRU_WIDTH + 2 * D_MODEL
RMS_EPS = 1e-6
NEG = -1e30
ATT_SCALE = HEAD_DIM ** -0.5

COL_Q = 0
COL_K = QKV_WIDTH
COL_V = 2 * QKV_WIDTH
COL_ZATT = 3 * QKV_WIDTH
COL_XLRU = COL_ZATT + ATT_WIDTH
COL_ZLRU = COL_XLRU + LRU_WIDTH
COL_GATT = COL_ZLRU + LRU_WIDTH
COL_GLRU = COL_GATT + D_MODEL

N_PROMPT_ROWS = BATCH * SEQ
N_SAMPLE_ROWS = DEC_BATCH * DEC_SEQ
MOD_ROWS = 16

VMEM_LIMIT = 52 * 1024 * 1024

F32 = jnp.float32
BF16 = jnp.bfloat16


def _cparams(n_grid_dims):
    return pltpu.CompilerParams(
        dimension_semantics=("arbitrary",) * n_grid_dims,
        vmem_limit_bytes=VMEM_LIMIT)


def _silu(x):
    return x * jax.nn.sigmoid(x)


def _t5_bucket(dist):
    dist = np.asarray(dist).astype(np.int32)
    max_exact = N_BUCKETS // 2
    safe = np.maximum(dist, 1).astype(np.float32)
    large = max_exact + (np.log(safe / max_exact) / np.float32(math.log(REL_MAX_DIST / max_exact))
                         * (N_BUCKETS - max_exact)).astype(np.int32)
    large = np.minimum(large, N_BUCKETS - 1)
    return np.where(dist < max_exact, dist, large).astype(np.int32)


def _mod_kernel(c_ref, w_ref, b_ref, o_ref):
    c = _silu(c_ref[...])
    o_ref[...] = jnp.dot(c.astype(BF16), w_ref[...].astype(BF16),
                         preferred_element_type=F32) + b_ref[...]


def _modulation(c_all, w_ada, b_ada):
    tn = 1024
    n_cols = 3 * D_MODEL
    return pl.pallas_call(
        _mod_kernel,
        grid=(DEPTH, n_cols // tn),
        in_specs=[
            pl.BlockSpec((MOD_ROWS, D_MODEL), lambda l, j: (0, 0)),
            pl.BlockSpec((None, D_MODEL, tn), lambda l, j: (l, 0, j)),
            pl.BlockSpec((None, 1, tn), lambda l, j: (l, 0, j)),
        ],
        out_specs=pl.BlockSpec((None, MOD_ROWS, tn), lambda l, j: (l, 0, j)),
        out_shape=jax.ShapeDtypeStruct((DEPTH, MOD_ROWS, n_cols), F32),
        compiler_params=_cparams(2),
        name="adaln_mod",
    )(c_all, w_ada, b_ada.reshape(DEPTH, 1, n_cols))


def _norm_kernel(x_ref, g_ref, *rest, modulate):
    if modulate:
        sc_ref, sh_ref, o_ref = rest
    else:
        (o_ref,) = rest
    x = x_ref[...]
    y = x * lax.rsqrt(jnp.mean(x * x, axis=-1, keepdims=True) + RMS_EPS)
    y = y * g_ref[...]
    if modulate:
        y = y * (1.0 + sc_ref[...]) + sh_ref[...]
    o_ref[...] = y.astype(o_ref.dtype)


def _norm_prompt(x, g, scale, shift, out_dtype):
    tm = 512
    modulate = scale is not None
    row_spec = pl.BlockSpec((None, tm, D_MODEL), lambda b, i: (b, i, 0))
    vec_spec = pl.BlockSpec((None, 1, D_MODEL), lambda b, i: (b, 0, 0))
    in_specs = [row_spec, pl.BlockSpec((1, D_MODEL), lambda b, i: (0, 0))]
    args = [x, g.reshape(1, D_MODEL)]
    if modulate:
        in_specs += [vec_spec, vec_spec]
        args += [scale, shift]
    return pl.pallas_call(
        functools.partial(_norm_kernel, modulate=modulate),
        grid=(BATCH, SEQ // tm),
        in_specs=in_specs,
        out_specs=row_spec,
        out_shape=jax.ShapeDtypeStruct(x.shape, out_dtype),
        compiler_params=_cparams(2),
        name="rmsnorm_prompt",
    )(*args)


def _norm_sample(x, g, scale_rows, shift_rows, out_dtype):
    modulate = scale_rows is not None
    args = [x, g.reshape(1, D_MODEL)]
    if modulate:
        args += [scale_rows, shift_rows]
    return pl.pallas_call(
        functools.partial(_norm_kernel, modulate=modulate),
        out_shape=jax.ShapeDtypeStruct(x.shape, out_dtype),
        name="rmsnorm_sample",
    )(*args)


def _in_proj_kernel(h_ref, hs_ref, w_ref, o_ref, os_ref, wbf_ref):
    @pl.when(pl.program_id(1) == 0)
    def _():
        wbf_ref[...] = w_ref[...].astype(BF16)
        os_ref[...] = jnp.dot(hs_ref[...], wbf_ref[...], preferred_element_type=F32)

    o_ref[...] = jnp.dot(h_ref[...], wbf_ref[...], preferred_element_type=F32)


def _in_proj(h_p, h_s, w_in, layer):
    tm, tn = 1024, 1024
    return pl.pallas_call(
        _in_proj_kernel,
        grid=(IN_COLS // tn, N_PROMPT_ROWS // tm),
        in_specs=[
            pl.BlockSpec((tm, D_MODEL), lambda j, i: (i, 0)),
            pl.BlockSpec((N_SAMPLE_ROWS, D_MODEL), lambda j, i: (0, 0)),
            pl.BlockSpec((None, D_MODEL, tn), lambda j, i: (layer, 0, j)),
        ],
        out_specs=[
            pl.BlockSpec((tm, tn), lambda j, i: (i, j)),
            pl.BlockSpec((N_SAMPLE_ROWS, tn), lambda j, i: (0, j)),
        ],
        out_shape=[
            jax.ShapeDtypeStruct((N_PROMPT_ROWS, IN_COLS), F32),
            jax.ShapeDtypeStruct((N_SAMPLE_ROWS, IN_COLS), F32),
        ],
        scratch_shapes=[pltpu.VMEM((D_MODEL, tn), BF16)],
        compiler_params=_cparams(2),
        name="in_proj",
    )(h_p, h_s, w_in)


def _attn_prompt_kernel(q0, k0, v0, q1, k1, v1, q2, k2, v2, z_ref, bias_ref, o_ref,
                        acc_s, m_s, s_s):
    nt = (((1,), (1,)), ((), ()))

    def load(ref, base, d):
        if d == 1:
            return ref[pl.ds(base, BLK), :].astype(BF16)
        return ref[pl.ds(base, BLK, stride=d), :].astype(BF16)

    def store(g, base, d, acc, m, s):
        idx = pl.ds(base, BLK) if d == 1 else pl.ds(base, BLK, stride=d)
        acc_s[g, idx, :] = acc
        m_s[g, idx, :] = jnp.broadcast_to(m, (BLK, HEAD_DIM))
        s_s[g, idx, :] = jnp.broadcast_to(s, (BLK, HEAD_DIM))

    def block(g, refs, d, base, pbase, has_prev):
        q_ref, k_ref, v_ref = refs
        q = load(q_ref, base, d)
        lc = lax.dot_general(q, load(k_ref, base, d), nt, preferred_element_type=F32)
        lc = lc * ATT_SCALE + bias_ref[g, :, BLK:]
        m = jnp.max(lc, axis=-1, keepdims=True)
        if has_prev is not None:
            lp = lax.dot_general(q, load(k_ref, pbase, d), nt, preferred_element_type=F32)
            lp = lp * ATT_SCALE + bias_ref[g, :, :BLK]
            lp = jnp.where(has_prev, lp, NEG)
            m = jnp.maximum(m, jnp.max(lp, axis=-1, keepdims=True))
        pc = jnp.exp(lc - m)
        s = jnp.sum(pc, axis=-1, keepdims=True)
        acc = jnp.dot(pc.astype(BF16), load(v_ref, base, d), preferred_element_type=F32)
        if has_prev is not None:
            pp = jnp.exp(lp - m)
            s = s + jnp.sum(pp, axis=-1, keepdims=True)
            acc = acc + jnp.dot(pp.astype(BF16), load(v_ref, pbase, d),
                                preferred_element_type=F32)
        store(g, base, d, acc, m, s)

    groups = ((q0, k0, v0), (q1, k1, v1), (q2, k2, v2))
    for g, (_, d) in enumerate(ATT_GROUPS):
        span = BLK * d
        n_blocks = SEQ // span
        if n_blocks == 1:
            def body_r(r, carry, g=g, d=d):
                block(g, groups[g], d, r, None, None)
                return carry
            lax.fori_loop(0, d, body_r, 0)
        else:
            for r in range(d):
                def body_n(n, carry, g=g, d=d, r=r, span=span):
                    base = n * span + r
                    pbase = jnp.maximum(n - 1, 0) * span + r
                    if d == 1:
                        base = pl.multiple_of(base, BLK)
                        pbase = pl.multiple_of(pbase, BLK)
                    block(g, groups[g], d, base, pbase, n > 0)
                    return carry
                lax.fori_loop(0, n_blocks, body_n, 0)

    rows = 256

    def merge(c, carry):
        r0 = pl.multiple_of(c * rows, rows)
        sl = pl.ds(r0, rows)
        m0, m1, m2 = m_s[0, sl, :], m_s[1, sl, :], m_s[2, sl, :]
        mm = jnp.maximum(jnp.maximum(m0, m1), m2)
        w0, w1, w2 = jnp.exp(m0 - mm), jnp.exp(m1 - mm), jnp.exp(m2 - mm)
        num = w0 * acc_s[0, sl, :] + w1 * acc_s[1, sl, :] + w2 * acc_s[2, sl, :]
        den = w0 * s_s[0, sl, :] + w1 * s_s[1, sl, :] + w2 * s_s[2, sl, :]
        att = num / den
        o_ref[sl, :] = (att * _silu(z_ref[sl, :])).astype(o_ref.dtype)
        return carry

    lax.fori_loop(0, SEQ // rows, merge, 0)


def _attn_prompt(proj3, bias_p):
    def head_spec(col0):
        return pl.BlockSpec((None, SEQ, HEAD_DIM),
                            lambda b, h, c=col0 // HEAD_DIM: (b, 0, c + h))

    in_specs = []
    for g in range(N_GROUPS):
        for col in (COL_Q, COL_K, COL_V):
            in_specs.append(head_spec(col + g * ATT_WIDTH))
    in_specs.append(head_spec(COL_ZATT))
    in_specs.append(pl.BlockSpec((N_GROUPS, None, BLK, 2 * BLK), lambda b, h: (0, h, 0, 0)))
    scratch = [pltpu.VMEM((N_GROUPS, SEQ, HEAD_DIM), F32) for _ in range(3)]
    return pl.pallas_call(
        _attn_prompt_kernel,
        grid=(BATCH, H_PER_GROUP),
        in_specs=in_specs,
        out_specs=pl.BlockSpec((None, SEQ, HEAD_DIM), lambda b, h: (b, 0, h)),
        out_shape=jax.ShapeDtypeStruct((BATCH, SEQ, ATT_WIDTH), BF16),
        scratch_shapes=scratch,
        compiler_params=_cparams(2),
        name="attn_prompt",
    )(*([proj3] * 10), bias_p)


def _prompt_bias(rel_bias):
    qi = np.arange(BLK)[:, None]
    ki = np.arange(2 * BLK)[None, :]
    du = qi + BLK - ki
    band = (du >= 0) & (du <= BLK)
    out = []
    for g, (win, dil) in enumerate(ATT_GROUPS):
        assert win // dil == BLK
        tbl = rel_bias[:, g * H_PER_GROUP:(g + 1) * H_PER_GROUP]
        b = tbl[_t5_bucket(np.clip(du, 0, None) * dil)]
        b = jnp.where(band[:, :, None], b.astype(F32), NEG)
        out.append(jnp.transpose(b, (2, 0, 1)))
    return jnp.stack(out)


def _softplus(x):
    return jnp.maximum(x, 0.0) + jnp.log1p(jnp.exp(-jnp.abs(x)))


def _lru_gates(xc, wr_ref, br, wi_ref, bi, nsp, n_blocks):
    a_parts, u_parts = [], []
    for n in range(n_blocks):
        sl = slice(n * LRU_BLOCK, (n + 1) * LRU_BLOCK)
        xn = xc[:, sl]
        xb = xn.astype(BF16)
        r = jax.nn.sigmoid(jnp.dot(xb, wr_ref[n].astype(BF16), preferred_element_type=F32) + br[:, sl])
        i = jax.nn.sigmoid(jnp.dot(xb, wi_ref[n].astype(BF16), preferred_element_type=F32) + bi[:, sl])
        log_a = (-LRU_C * r) * nsp[:, sl]
        a = jnp.exp(log_a)
        one_minus_a2 = -jnp.tanh(log_a) * (a * a + 1.0)
        a_parts.append(a)
        u_parts.append(jnp.sqrt(one_minus_a2) * (i * xn))
    return a_parts, u_parts


def _scan8(a, u, rows):
    for s in (1, 2, 4):
        keep = rows >= s
        a_sh = pltpu.roll(a, s, 0)
        u_sh = pltpu.roll(u, s, 0)
        u = jnp.where(keep, a * u_sh + u, u)
        a = jnp.where(keep, a * a_sh, a)
    return a, u


def _lru_prompt_kernel(x_ref, z_ref, cw_ref, cb_ref, wr_ref, br_ref, wi_ref, bi_ref, lam_ref,
                       o_ref, conv_ref, hl_ref, xpad, a_s, u_s):
    T, C = x_ref.shape
    n_blocks = C // LRU_BLOCK
    R = 256
    pad = 8

    xpad[0:pad, :] = jnp.zeros((pad, C), F32)

    def copy(c, carry):
        r0 = pl.multiple_of(c * R, R)
        xpad[pl.ds(r0 + pad, R), :] = x_ref[pl.ds(r0, R), :]
        return carry

    lax.fori_loop(0, T // R, copy, 0)
    conv_ref[...] = x_ref[T - (CONV_WIDTH - 1):T, :]

    nsp = _softplus(-lam_ref[...])
    br = br_ref[...]
    bi = bi_ref[...]
    cb = cb_ref[...]

    def gates(c, carry):
        r0 = pl.multiple_of(c * R, R)
        xw = xpad[pl.ds(r0, R + pad), :]
        off = pad - (CONV_WIDTH - 1)
        xc = xw[off:off + R] * cw_ref[0:1, :]
        for tap in range(1, CONV_WIDTH):
            xc = xc + xw[off + tap:off + tap + R] * cw_ref[tap:tap + 1, :]
        xc = xc + cb
        a_parts, u_parts = _lru_gates(xc, wr_ref, br, wi_ref, bi, nsp, n_blocks)
        for n in range(n_blocks):
            sl = slice(n * LRU_BLOCK, (n + 1) * LRU_BLOCK)
            a_s[pl.ds(r0, R), sl] = a_parts[n]
            u_s[pl.ds(r0, R), sl] = u_parts[n]
        return carry

    lax.fori_loop(0, T // R, gates, 0)

    rows = lax.broadcasted_iota(jnp.int32, (8, C), 0)

    def scan(c, h_prev):
        t0 = pl.multiple_of(c * 16, 16)
        hs = []
        for half in range(2):
            sl = pl.ds(t0 + 8 * half, 8)
            a, u = _scan8(a_s[sl, :], u_s[sl, :], rows)
            h = a * h_prev + u
            hs.append(h)
            h_prev = h[7:8, :]
        h16 = jnp.concatenate(hs, axis=0)
        z = z_ref[pl.ds(t0, 16), :]
        o_ref[pl.ds(t0, 16), :] = (h16 * _silu(z)).astype(o_ref.dtype)
        return h_prev

    h_last = lax.fori_loop(0, T // 16, scan, jnp.zeros((1, C), F32))
    hl_ref[...] = h_last


def _lru_prompt(proj3, conv_w, conv_b, w_r, b_r, w_i, b_i, lam, layer):
    tc = 512
    nb = tc // LRU_BLOCK
    n_ct = LRU_WIDTH // tc

    def col_spec(col0):
        return pl.BlockSpec((None, SEQ, tc), lambda b, c, c0=col0 // tc: (b, 0, c0 + c))

    vec_spec = pl.BlockSpec((None, 1, tc), lambda b, c: (layer, 0, c))
    w_spec = pl.BlockSpec((None, nb, LRU_BLOCK, LRU_BLOCK), lambda b, c: (layer, c, 0, 0))
    return pl.pallas_call(
        _lru_prompt_kernel,
        grid=(BATCH, n_ct),
        in_specs=[
            col_spec(COL_XLRU), col_spec(COL_ZLRU),
            pl.BlockSpec((None, CONV_WIDTH, tc), lambda b, c: (layer, 0, c)),
            vec_spec, w_spec, vec_spec, w_spec, vec_spec, vec_spec,
        ],
        out_specs=[
            pl.BlockSpec((None, SEQ, tc), lambda b, c: (b, 0, c)),
            pl.BlockSpec((None, CONV_WIDTH - 1, tc), lambda b, c: (b, 0, c)),
            pl.BlockSpec((None, 1, tc), lambda b, c: (b, 0, c)),
        ],
        out_shape=[
            jax.ShapeDtypeStruct((BATCH, SEQ, LRU_WIDTH), BF16),
            jax.ShapeDtypeStruct((BATCH, CONV_WIDTH - 1, LRU_WIDTH), F32),
            jax.ShapeDtypeStruct((BATCH, 1, LRU_WIDTH), F32),
        ],
        scratch_shapes=[
            pltpu.VMEM((SEQ + 8, tc), F32),
            pltpu.VMEM((SEQ, tc), F32),
            pltpu.VMEM((SEQ, tc), F32),
        ],
        compiler_params=_cparams(2),
        name="lru_prompt",
    )(proj3, proj3, conv_w, conv_b, w_r, b_r, w_i, b_i, lam)


def _mix_kernel(att_ref, lru_ref, ga_ref, gl_ref, wpa_ref, wpb_ref, o_ref):
    y_att = jnp.dot(att_ref[...], wpa_ref[...], preferred_element_type=F32)
    y_lru = jnp.dot(lru_ref[...], wpb_ref[...], preferred_element_type=F32)
    merged = jax.nn.sigmoid(ga_ref[...]) * y_att + jax.nn.sigmoid(gl_ref[...]) * y_lru
    o_ref[...] = merged.astype(o_ref.dtype)


def _mix(att, lru, proj, wpa, wpb, layer, tm):
    m_rows = att.shape[0]
    return pl.pallas_call(
        _mix_kernel,
        grid=(m_rows // tm,),
        in_specs=[
            pl.BlockSpec((tm, ATT_WIDTH), lambda i: (i, 0)),
            pl.BlockSpec((tm, LRU_WIDTH), lambda i: (i, 0)),
            pl.BlockSpec((tm, D_MODEL), lambda i: (i, COL_GATT // D_MODEL)),
            pl.BlockSpec((tm, D_MODEL), lambda i: (i, COL_GLRU // D_MODEL)),
            pl.BlockSpec((None, ATT_WIDTH, D_MODEL), lambda i: (layer, 0, 0)),
            pl.BlockSpec((None, LRU_WIDTH, D_MODEL), lambda i: (layer, 0, 0)),
        ],
        out_specs=pl.BlockSpec((tm, D_MODEL), lambda i: (i, 0)),
        out_shape=jax.ShapeDtypeStruct((m_rows, D_MODEL), BF16),
        compiler_params=_cparams(1),
        name="branch_mix",
    )(att, lru, proj, proj, wpa, wpb)


def _residual_kernel(m_ref, w_ref, x_ref, gate_ref, o_ref):
    out = jnp.dot(m_ref[...], w_ref[...], preferred_element_type=F32)
    o_ref[...] = x_ref[...] + gate_ref[...] * out


def _residual_prompt(merged, w_out, x, gate, layer):
    tm = 512
    row_spec = pl.BlockSpec((None, tm, D_MODEL), lambda b, i: (b, i, 0))
    return pl.pallas_call(
        _residual_kernel,
        grid=(BATCH, SEQ // tm),
        in_specs=[
            row_spec,
            pl.BlockSpec((None, D_MODEL, D_MODEL), lambda b, i: (layer, 0, 0)),
            row_spec,
            pl.BlockSpec((None, 1, D_MODEL), lambda b, i: (b, 0, 0)),
        ],
        out_specs=row_spec,
        out_shape=jax.ShapeDtypeStruct(x.shape, F32),
        compiler_params=_cparams(2),
        name="residual_prompt",
    )(merged, w_out, x, gate)


def _residual_sample(merged, w_out, x, gate_rows, layer):
    full = pl.BlockSpec((N_SAMPLE_ROWS, D_MODEL), lambda i: (0, 0))
    return pl.pallas_call(
        _residual_kernel,
        grid=(1,),
        in_specs=[full, pl.BlockSpec((None, D_MODEL, D_MODEL), lambda i: (layer, 0, 0)), full, full],
        out_specs=full,
        out_shape=jax.ShapeDtypeStruct(x.shape, F32),
        compiler_params=_cparams(1),
        name="residual_sample",
    )(merged, w_out, x, gate_rows)


def _cache_update_kernel(c0, c1, c2, n0, n1, n2, o0, o1, o2, sem):
    copies = []
    for gi, (c, n, o) in enumerate(((c0, n0, o0), (c1, n1, o1), (c2, n2, o2))):
        keep = c.shape[2] - DEC_SEQ
        for l in range(DEPTH):
            copies.append(pltpu.make_async_copy(
                c.at[l, :, pl.ds(DEC_SEQ, keep)], o.at[l, :, pl.ds(0, keep)], sem.at[0, gi, l]))
            copies.append(pltpu.make_async_copy(
                n.at[l], o.at[l, :, pl.ds(keep, DEC_SEQ)], sem.at[1, gi, l]))
    for cp in copies:
        cp.start()
    for cp in copies:
        cp.wait()


def _cache_update(caches, new_rows):
    any_spec = pl.BlockSpec(memory_space=pl.ANY)
    return pl.pallas_call(
        _cache_update_kernel,
        in_specs=[any_spec] * 6,
        out_specs=[any_spec] * 3,
        out_shape=[jax.ShapeDtypeStruct(c.shape, c.dtype) for c in caches],
        scratch_shapes=[pltpu.SemaphoreType.DMA((2, N_GROUPS, DEPTH))],
        name="cache_update",
    )(*caches, *new_rows)


def _attn_sample_kernel(q_ref, k_ref, v_ref, z_ref, c0_ref, c1_ref, c2_ref, bc_ref, bn_ref,
                        o_ref, kv0_ref, kv1_ref, kv2_ref):
    hp = H_PER_GROUP
    kv_refs = (kv0_ref, kv1_ref, kv2_ref)
    k_new, v_new = [], []
    for g in range(N_GROUPS):
        kn = k_ref[:, g * hp:(g + 1) * hp, :]
        vn = v_ref[:, g * hp:(g + 1) * hp, :]
        kv_refs[g][:, 0] = kn
        kv_refs[g][:, 1] = vn
        k_new.append(kn)
        v_new.append(vn)

    for s in range(DEC_SEQ):
        accs, ms, ss = [], [], []
        for g in range(N_GROUPS):
            qg = q_ref[s, g * hp:(g + 1) * hp, :]
            if g == 0:
                kc, vc = c0_ref[:, 0], c0_ref[:, 1]
            elif g == 1:
                kc, vc = c1_ref[:, s, 0], c1_ref[:, s, 1]
            else:
                kc, vc = c2_ref[:, s, 0], c2_ref[:, s, 1]
            lc = jnp.sum(kc * qg[None], axis=-1, keepdims=True) * ATT_SCALE + bc_ref[g, s]
            ln = jnp.sum(k_new[g] * qg[None], axis=-1, keepdims=True) * ATT_SCALE + bn_ref[g, s]
            m = jnp.maximum(jnp.max(lc, axis=0), jnp.max(ln, axis=0))
            pc = jnp.exp(lc - m[None])
            pn = jnp.exp(ln - m[None])
            ss.append(jnp.sum(pc, axis=0) + jnp.sum(pn, axis=0))
            accs.append(jnp.sum(pc * vc, axis=0) + jnp.sum(pn * v_new[g], axis=0))
            ms.append(m)
        mm = jnp.maximum(jnp.maximum(ms[0], ms[1]), ms[2])
        ws = [jnp.exp(m - mm) for m in ms]
        num = ws[0] * accs[0] + ws[1] * accs[1] + ws[2] * accs[2]
        den = ws[0] * ss[0] + ws[1] * ss[1] + ws[2] * ss[2]
        o_ref[s] = (num / den) * _silu(z_ref[s])


def _attn_sample(proj_s4, caches, bias_c, bias_n, layer):
    c0, c1, c2 = caches
    hp = H_PER_GROUP

    def head_spec(col0, n_heads):
        return pl.BlockSpec((None, DEC_SEQ, n_heads, HEAD_DIM),
                            lambda b, c=col0 // (HEAD_DIM * n_heads): (b, 0, c, 0))

    in_specs = [
        head_spec(COL_Q, N_ATT_HEADS), head_spec(COL_K, N_ATT_HEADS), head_spec(COL_V, N_ATT_HEADS),
        head_spec(COL_ZATT, hp),
        pl.BlockSpec((None, None, BLK, 2, hp, HEAD_DIM), lambda b: (layer, b, 0, 0, 0, 0)),
        pl.BlockSpec((None, None, BLK, DEC_SEQ, 2, hp, HEAD_DIM), lambda b: (layer, b, 0, 0, 0, 0, 0)),
        pl.BlockSpec((None, None, BLK, DEC_SEQ, 2, hp, HEAD_DIM), lambda b: (layer, b, 0, 0, 0, 0, 0)),
        pl.BlockSpec(bias_c.shape, lambda b: (0,) * 5),
        pl.BlockSpec(bias_n.shape, lambda b: (0,) * 5),
    ]
    out_specs = [pl.BlockSpec((None, DEC_SEQ, hp, HEAD_DIM), lambda b: (b, 0, 0, 0))]
    out_shape = [jax.ShapeDtypeStruct((DEC_BATCH, DEC_SEQ, hp, HEAD_DIM), F32)]
    for _ in range(N_GROUPS):
        out_specs.append(pl.BlockSpec((None, DEC_SEQ, 2, hp, HEAD_DIM), lambda b: (b, 0, 0, 0, 0)))
        out_shape.append(jax.ShapeDtypeStruct((DEC_BATCH, DEC_SEQ, 2, hp, HEAD_DIM), F32))
    return pl.pallas_call(
        _attn_sample_kernel,
        grid=(DEC_BATCH,),
        in_specs=in_specs,
        out_specs=out_specs,
        out_shape=out_shape,
        compiler_params=_cparams(1),
        name="attn_sample",
    )(proj_s4, proj_s4, proj_s4, proj_s4, c0, c1, c2, bias_c, bias_n)


def _sample_bias(rel_bias):
    s_idx = np.arange(DEC_SEQ)
    bc, bn = [], []
    for g, (win, dil) in enumerate(ATT_GROUPS):
        tbl = rel_bias[:, g * H_PER_GROUP:(g + 1) * H_PER_GROUP].astype(F32)
        key = np.arange(BLK)
        if dil == 1:
            j = BLK + s_idx[:, None] - key[None, :]
            valid = j <= BLK
        else:
            j = np.broadcast_to(BLK - key[None, :], (DEC_SEQ, BLK))
            valid = np.ones_like(j, dtype=bool)
        b = tbl[_t5_bucket(np.clip(j, 0, None) * dil)]
        bc.append(jnp.where(valid[:, :, None], b, NEG))
        jn = s_idx[:, None] - s_idx[None, :]
        valid_n = (jn >= 0) & (jn * dil <= win) & ((jn == 0) | (dil == 1))
        b = tbl[_t5_bucket(np.clip(jn, 0, None) * dil)]
        bn.append(jnp.where(valid_n[:, :, None], b, NEG))
    bc = jnp.stack(bc)
    bn = jnp.stack(bn)
    bc = jnp.broadcast_to(bc[..., None], bc.shape + (HEAD_DIM,))
    bn = jnp.broadcast_to(bn[..., None], bn.shape + (HEAD_DIM,))
    return bc, bn


def _lru_sample_kernel(x_ref, z_ref, cs_ref, h0_ref, cw_ref, cb_ref, wr_ref, br_ref, wi_ref, bi_ref,
                       lam_ref, o_ref, conv_ref, hl_ref):
    S = DEC_SEQ
    xp = [cs_ref[t] for t in range(CONV_WIDTH - 1)] + [x_ref[t] for t in range(S)]
    xc = []
    for t in range(S):
        y = xp[t] * cw_ref[0:1, :]
        for tap in range(1, CONV_WIDTH):
            y = y + xp[t + tap] * cw_ref[tap:tap + 1, :]
        xc.append(y + cb_ref[...])
    xcat = jnp.concatenate(xc, axis=0)
    nsp = _softplus(-lam_ref[...])
    a_parts, u_parts = _lru_gates(xcat, wr_ref, br_ref[...], wi_ref, bi_ref[...], nsp, LRU_BLOCKS)
    a = jnp.concatenate(a_parts, axis=1)
    u = jnp.concatenate(u_parts, axis=1)
    nb = DEC_BATCH
    h = h0_ref[...]
    for t in range(S):
        h = a[t * nb:(t + 1) * nb] * h + u[t * nb:(t + 1) * nb]
        o_ref[t] = h * _silu(z_ref[t])
    for t in range(CONV_WIDTH - 1):
        conv_ref[t] = xp[S + t]
    hl_ref[...] = h


def _lru_sample(x_t, z_t, conv_t, h0, conv_w, conv_b, w_r, b_r, w_i, b_i, lam, layer):
    C = LRU_WIDTH

    def full(shape):
        return pl.BlockSpec(shape, lambda i: (0,) * len(shape))

    vec_spec = pl.BlockSpec((None, 1, C), lambda i: (layer, 0, 0))
    w_spec = pl.BlockSpec((None, LRU_BLOCKS, LRU_BLOCK, LRU_BLOCK), lambda i: (layer, 0, 0, 0))
    return pl.pallas_call(
        _lru_sample_kernel,
        grid=(1,),
        in_specs=[
            full((DEC_SEQ, DEC_BATCH, C)), full((DEC_SEQ, DEC_BATCH, C)),
            full((CONV_WIDTH - 1, DEC_BATCH, C)), full((DEC_BATCH, C)),
            pl.BlockSpec((None, CONV_WIDTH, C), lambda i: (layer, 0, 0)),
            vec_spec, w_spec, vec_spec, w_spec, vec_spec, vec_spec,
        ],
        out_specs=[full((DEC_SEQ, DEC_BATCH, C)), full((CONV_WIDTH - 1, DEC_BATCH, C)),
                   full((DEC_BATCH, C))],
        out_shape=[
            jax.ShapeDtypeStruct((DEC_SEQ, DEC_BATCH, C), F32),
            jax.ShapeDtypeStruct((CONV_WIDTH - 1, DEC_BATCH, C), F32),
            jax.ShapeDtypeStruct((DEC_BATCH, C), F32),
        ],
        compiler_params=_cparams(1),
        name="lru_sample",
    )(x_t, z_t, conv_t, h0, conv_w, conv_b, w_r, b_r, w_i, b_i, lam)


def kernel(x_prompt, x_sample, c_prompt, c_sample, cache_kv_g0, cache_kv_g1, cache_kv_g2, state_conv, state_h, rel_bias, w_ada, b_ada, norm_g, w_in, conv_w, conv_b, w_r, b_r, w_i, b_i, lam, w_pa, w_pb, w_out, final_g):
    L, B, T, D = DEPTH, BATCH, SEQ, D_MODEL
    Bd, S = DEC_BATCH, DEC_SEQ

    c_all = jnp.concatenate(
        [c_prompt, c_sample, jnp.zeros((MOD_ROWS - B - Bd, D), F32)], axis=0)
    mod = _modulation(c_all, w_ada, b_ada).reshape(L, MOD_ROWS, 3, D)
    mod_p = mod[:, :B]
    mod_s = jnp.repeat(mod[:, B:B + Bd], S, axis=1)

    wpa_bf = w_pa.astype(BF16)
    wpb_bf = w_pb.astype(BF16)
    wout_bf = w_out.astype(BF16)
    conv_b3 = conv_b.reshape(L, 1, LRU_WIDTH)
    b_r3 = b_r.reshape(L, 1, LRU_WIDTH)
    b_i3 = b_i.reshape(L, 1, LRU_WIDTH)
    lam3 = lam.reshape(L, 1, LRU_WIDTH)

    bias_p = _prompt_bias(rel_bias)
    bias_c, bias_n = _sample_bias(rel_bias)

    cache_views = (
        cache_kv_g0,
        cache_kv_g1.reshape(L, Bd, BLK, 4, 2, H_PER_GROUP, HEAD_DIM),
        cache_kv_g2.reshape(L, Bd, BLK, 16, 2, H_PER_GROUP, HEAD_DIM),
    )

    xp = x_prompt
    xs = x_sample.reshape(Bd * S, D)
    kv_p = ([], [], [])
    kv_new = ([], [], [])
    conv_p, h_p, conv_s, h_s = [], [], [], []
    for l in range(L):
        shift_p, scale_p, gate_p = (mod_p[l, :, i][:, None, :] for i in range(3))
        shift_s, scale_s, gate_s = (mod_s[l, :, i] for i in range(3))

        hp_ = _norm_prompt(xp, norm_g[l], scale_p, shift_p, BF16).reshape(B * T, D)
        hs_ = _norm_sample(xs, norm_g[l], scale_s, shift_s, BF16)
        proj_p, proj_s = _in_proj(hp_, hs_, w_in, l)
        proj3 = proj_p.reshape(B, T, IN_COLS)

        att_p = _attn_prompt(proj3, bias_p)
        lru_p, cv, hl = _lru_prompt(proj3, conv_w, conv_b3, w_r, b_r3, w_i, b_i3, lam3, l)
        merged_p = _mix(att_p.reshape(B * T, ATT_WIDTH), lru_p.reshape(B * T, LRU_WIDTH),
                        proj_p, wpa_bf, wpb_bf, l, 256)
        xp = _residual_prompt(merged_p.reshape(B, T, D), wout_bf, xp, gate_p, l)
        conv_p.append(cv)
        h_p.append(hl.reshape(B, LRU_WIDTH))
        for g, (win, _) in enumerate(ATT_GROUPS):
            w = min(win, T)
            kg = proj3[:, T - w:, COL_K + g * ATT_WIDTH:COL_K + (g + 1) * ATT_WIDTH]
            vg = proj3[:, T - w:, COL_V + g * ATT_WIDTH:COL_V + (g + 1) * ATT_WIDTH]
            kv_p[g].append(jnp.stack([kg, vg], axis=2).reshape(B, w, 2, H_PER_GROUP, HEAD_DIM))

        proj_s4 = proj_s.reshape(Bd, S, IN_COLS // HEAD_DIM, HEAD_DIM)
        outs = _attn_sample(proj_s4, cache_views, bias_c, bias_n, l)
        att_s = outs[0]
        for g in range(N_GROUPS):
            kv_new[g].append(outs[1 + g])
        ps3 = proj_s.reshape(Bd, S, IN_COLS)
        x_t = jnp.transpose(ps3[:, :, COL_XLRU:COL_XLRU + LRU_WIDTH], (1, 0, 2))
        z_t = jnp.transpose(ps3[:, :, COL_ZLRU:COL_ZLRU + LRU_WIDTH], (1, 0, 2))
        conv_t = jnp.transpose(state_conv[l], (1, 0, 2))
        lru_t, cv_t, hl_s = _lru_sample(x_t, z_t, conv_t, state_h[l], conv_w, conv_b3,
                                        w_r, b_r3, w_i, b_i3, lam3, l)
        lru_s = jnp.transpose(lru_t, (1, 0, 2)).reshape(Bd * S, LRU_WIDTH).astype(BF16)
        merged_s = _mix(att_s.reshape(Bd * S, ATT_WIDTH).astype(BF16), lru_s, proj_s,
                        wpa_bf, wpb_bf, l, Bd * S)
        xs = _residual_sample(merged_s, wout_bf, xs, gate_s, l)
        conv_s.append(jnp.transpose(cv_t, (1, 0, 2)))
        h_s.append(hl_s)

    y_prompt = _norm_prompt(xp, final_g, None, None, F32)
    y_sample = _norm_sample(xs, final_g, None, None, F32).reshape(Bd, S, D)
    kv_s = _cache_update((cache_kv_g0, cache_kv_g1, cache_kv_g2),
                         tuple(jnp.stack(rows) for rows in kv_new))
    return (y_prompt, y_sample,
            jnp.stack(kv_p[0]), jnp.stack(kv_p[1]), jnp.stack(kv_p[2]),
            jnp.stack(conv_p), jnp.stack(h_p),
            kv_s[0], kv_s[1], kv_s[2],
            jnp.stack(conv_s), jnp.stack(h_s))
```

```python
import functools
import math

import jax
import jax.numpy as jnp
import numpy as np
from jax import lax
from jax.experimental import pallas as pl
from jax.experimental.pallas import tpu as pltpu

D_MODEL = 2048
BATCH = 4
SEQ = 2048
DEPTH = 4
DEC_BATCH = 8
DEC_SEQ = 4
HEAD_DIM = 128
H_PER_GROUP = 8
ATT_GROUPS = ((128, 1), (512, 4), (2048, 16))
N_GROUPS = 3
N_ATT_HEADS = N_GROUPS * H_PER_GROUP
QKV_WIDTH = N_ATT_HEADS * HEAD_DIM
ATT_WIDTH = H_PER_GROUP * HEAD_DIM
N_BUCKETS = 32
REL_MAX_DIST = 2048
BLK = 128
LRU_WIDTH = D_MODEL
LRU_BLOCKS = 16
LRU_BLOCK = LRU_WIDTH // LRU_BLOCKS
CONV_WIDTH = 4
LRU_C = 8.0
IN_COLS = 3 * QKV_WIDTH + ATT_WIDTH + 2 * LRU_WIDTH + 2 * D_MODEL
RMS_EPS = 1e-6
NEG = -1e30
ATT_SCALE = HEAD_DIM ** -0.5

COL_Q = 0
COL_K = QKV_WIDTH
COL_V = 2 * QKV_WIDTH
COL_ZATT = 3 * QKV_WIDTH
COL_XLRU = COL_ZATT + ATT_WIDTH
COL_ZLRU = COL_XLRU + LRU_WIDTH
COL_GATT = COL_ZLRU + LRU_WIDTH
COL_GLRU = COL_GATT + D_MODEL

N_PROMPT_ROWS = BATCH * SEQ
N_SAMPLE_ROWS = DEC_BATCH * DEC_SEQ
MOD_ROWS = 16

VMEM_LIMIT = 52 * 1024 * 1024
ATT_ILP = 4

F32 = jnp.float32
BF16 = jnp.bfloat16


def _cparams(n_grid_dims, vmem_limit=VMEM_LIMIT):
    return pltpu.CompilerParams(
        dimension_semantics=("arbitrary",) * n_grid_dims,
        vmem_limit_bytes=vmem_limit)


def _sigmoid(x):
    return 0.5 * jnp.tanh(0.5 * x) + 0.5


def _silu(x):
    return x * _sigmoid(x)


def _t5_bucket(dist):
    dist = np.asarray(dist).astype(np.int32)
    max_exact = N_BUCKETS // 2
    safe = np.maximum(dist, 1).astype(np.float32)
    large = max_exact + (np.log(safe / max_exact) / np.float32(math.log(REL_MAX_DIST / max_exact))
                         * (N_BUCKETS - max_exact)).astype(np.int32)
    large = np.minimum(large, N_BUCKETS - 1)
    return np.where(dist < max_exact, dist, large).astype(np.int32)


def _mod_kernel(c_ref, w_ref, b_ref, o_ref):
    c = _silu(c_ref[...])
    o_ref[...] = jnp.dot(c.astype(BF16), w_ref[...].astype(BF16),
                         preferred_element_type=F32) + b_ref[...]


def _modulation(c_all, w_ada, b_ada):
    tn = 1024
    n_cols = 3 * D_MODEL
    return pl.pallas_call(
        _mod_kernel,
        grid=(DEPTH, n_cols // tn),
        in_specs=[
            pl.BlockSpec((MOD_ROWS, D_MODEL), lambda l, j: (0, 0)),
            pl.BlockSpec((None, D_MODEL, tn), lambda l, j: (l, 0, j)),
            pl.BlockSpec((None, 1, tn), lambda l, j: (l, 0, j)),
        ],
        out_specs=pl.BlockSpec((None, MOD_ROWS, tn), lambda l, j: (l, 0, j)),
        out_shape=jax.ShapeDtypeStruct((DEPTH, MOD_ROWS, n_cols), F32),
        compiler_params=_cparams(2),
        name="adaln_mod",
    )(c_all, w_ada, b_ada.reshape(DEPTH, 1, n_cols))


def _norm_kernel(x_ref, g_ref, *rest, modulate):
    if modulate:
        sc_ref, sh_ref, o_ref = rest
    else:
        (o_ref,) = rest
    x = x_ref[...]
    y = x * lax.rsqrt(jnp.mean(x * x, axis=-1, keepdims=True) + RMS_EPS)
    y = y * g_ref[...]
    if modulate:
        y = y * (1.0 + sc_ref[...]) + sh_ref[...]
    o_ref[...] = y.astype(o_ref.dtype)


def _norm_prompt(x, g, scale, shift, out_dtype):
    tm = 512
    modulate = scale is not None
    row_spec = pl.BlockSpec((None, tm, D_MODEL), lambda b, i: (b, i, 0))
    vec_spec = pl.BlockSpec((None, 1, D_MODEL), lambda b, i: (b, 0, 0))
    in_specs = [row_spec, pl.BlockSpec((1, D_MODEL), lambda b, i: (0, 0))]
    args = [x, g.reshape(1, D_MODEL)]
    if modulate:
        in_specs += [vec_spec, vec_spec]
        args += [scale, shift]
    return pl.pallas_call(
        functools.partial(_norm_kernel, modulate=modulate),
        grid=(BATCH, SEQ // tm),
        in_specs=in_specs,
        out_specs=row_spec,
        out_shape=jax.ShapeDtypeStruct(x.shape, out_dtype),
        compiler_params=_cparams(2),
        name="rmsnorm_prompt",
    )(*args)


def _norm_sample(x, g, scale_rows, shift_rows, out_dtype):
    modulate = scale_rows is not None
    args = [x, g.reshape(1, D_MODEL)]
    if modulate:
        args += [scale_rows, shift_rows]
    return pl.pallas_call(
        functools.partial(_norm_kernel, modulate=modulate),
        out_shape=jax.ShapeDtypeStruct(x.shape, out_dtype),
        name="rmsnorm_sample",
    )(*args)


IN_PROJ_VMEM_LIMIT = 58 * 1024 * 1024
IN_TM = 1024
IN_TN = ATT_WIDTH
IN_TILES_PER_SEQ = SEQ // IN_TM
K_TILE0 = COL_K // IN_TN
V_TILE0 = COL_V // IN_TN


def _kv_rows(g):
    w = min(ATT_GROUPS[g][0], SEQ)
    assert w % IN_TM == 0 or IN_TM % w == 0
    return min(w, IN_TM)


def _in_proj_kernel(h_ref, hs_ref, w_ref, o_ref, os_ref, kv0_ref, kv1_ref, kv2_ref, wbf_ref):
    j = pl.program_id(0)
    i = pl.program_id(1)

    @pl.when(i == 0)
    def _():
        wbf_ref[...] = w_ref[...].astype(BF16)
        os_ref[...] = jnp.dot(hs_ref[...], wbf_ref[...], preferred_element_type=F32)

    o_ref[...] = jnp.dot(h_ref[...], wbf_ref[...], preferred_element_type=F32)

    for g, kv_ref in enumerate((kv0_ref, kv1_ref, kv2_ref)):
        rows = _kv_rows(g)
        is_kv = (j == K_TILE0 + g) | (j == V_TILE0 + g)
        if min(ATT_GROUPS[g][0], SEQ) < SEQ:
            is_kv = is_kv & (i % IN_TILES_PER_SEQ == IN_TILES_PER_SEQ - 1)

        @pl.when(is_kv)
        def _(kv_ref=kv_ref, rows=rows):
            for h in range(H_PER_GROUP):
                kv_ref[:, h, :] = o_ref[IN_TM - rows:, h * HEAD_DIM:(h + 1) * HEAD_DIM]


def _kv_index_map(g):
    w = min(ATT_GROUPS[g][0], SEQ)
    n_i = N_PROMPT_ROWS // IN_TM
    full = w >= SEQ

    def index_map(j, i):
        k_col = K_TILE0 + g
        v_col = V_TILE0 + g
        c = jnp.where(j >= v_col, 1, 0)
        in_col = (j == k_col) | (j == v_col)
        if full:
            ii = jnp.where(in_col, i, jnp.where(j < k_col, 0, n_i - 1))
            return (ii // IN_TILES_PER_SEQ, ii % IN_TILES_PER_SEQ, c, 0, 0)
        b_col = jnp.maximum(i - (IN_TILES_PER_SEQ - 1), 0) // IN_TILES_PER_SEQ
        b = jnp.where(in_col, b_col, jnp.where(j < k_col, 0, BATCH - 1))
        return (b, 0, c, 0, 0)

    return index_map


def _in_proj(h_p, h_s, w_in, layer):
    tm, tn = IN_TM, IN_TN
    out_specs = [
        pl.BlockSpec((tm, tn), lambda j, i: (i, j)),
        pl.BlockSpec((N_SAMPLE_ROWS, tn), lambda j, i: (0, j)),
    ]
    out_shape = [
        jax.ShapeDtypeStruct((N_PROMPT_ROWS, IN_COLS), F32),
        jax.ShapeDtypeStruct((N_SAMPLE_ROWS, IN_COLS), F32),
    ]
    for g, (win, _) in enumerate(ATT_GROUPS):
        out_specs.append(pl.BlockSpec((None, _kv_rows(g), None, H_PER_GROUP, HEAD_DIM), _kv_index_map(g)))
        out_shape.append(jax.ShapeDtypeStruct((BATCH, min(win, SEQ), 2, H_PER_GROUP, HEAD_DIM), F32))
    return pl.pallas_call(
        _in_proj_kernel,
        grid=(IN_COLS // tn, N_PROMPT_ROWS // tm),
        in_specs=[
            pl.BlockSpec((tm, D_MODEL), lambda j, i: (i, 0)),
            pl.BlockSpec((N_SAMPLE_ROWS, D_MODEL), lambda j, i: (0, 0)),
            pl.BlockSpec((None, D_MODEL, tn), lambda j, i: (layer, 0, j)),
        ],
        out_specs=out_specs,
        out_shape=out_shape,
        scratch_shapes=[pltpu.VMEM((D_MODEL, tn), BF16)],
        compiler_params=_cparams(2, IN_PROJ_VMEM_LIMIT),
        name="in_proj",
    )(h_p, h_s, w_in)


def _attn_prompt_kernel(q0, k0, v0, q1, k1, v1, q2, k2, v2, z_ref, bias_ref, o_ref,
                        acc_s, m_s, s_s):
    nt = (((1,), (1,)), ((), ()))

    def load(ref, base, d):
        if d == 1:
            return ref[pl.ds(base, BLK), :].astype(BF16)
        return ref[pl.ds(base, BLK, stride=d), :].astype(BF16)

    def store(g, base, d, acc, m, s):
        idx = pl.ds(base, BLK) if d == 1 else pl.ds(base, BLK, stride=d)
        acc_s[g, idx, :] = acc
        m_s[g, idx, :] = jnp.broadcast_to(m, (BLK, HEAD_DIM))
        s_s[g, idx, :] = jnp.broadcast_to(s, (BLK, HEAD_DIM))

    def block(g, refs, d, base, pbase, has_prev):
        q_ref, k_ref, v_ref = refs
        q = load(q_ref, base, d)
        lc = lax.dot_general(q, load(k_ref, base, d), nt, preferred_element_type=F32)
        lc = lc * ATT_SCALE + bias_ref[g, :, BLK:]
        if has_prev is None:
            return lc, None
        lp = lax.dot_general(q, load(k_ref, pbase, d), nt, preferred_element_type=F32)
        lp = lp * ATT_SCALE + bias_ref[g, :, :BLK]
        return lc, jnp.where(has_prev, lp, NEG)

    def softmax(lc, lp):
        if lp is None:
            m = jnp.max(lc, axis=-1, keepdims=True)
            pc = jnp.exp(lc - m)
            return pc, None, m, jnp.sum(pc, axis=-1, keepdims=True)
        m = jnp.max(jnp.maximum(lc, lp), axis=-1, keepdims=True)
        pc = jnp.exp(lc - m)
        pp = jnp.exp(lp - m)
        return pc, pp, m, jnp.sum(pc + pp, axis=-1, keepdims=True)

    def weighted(refs, d, base, pbase, pc, pp):
        v_ref = refs[2]
        acc = jnp.dot(pc.astype(BF16), load(v_ref, base, d), preferred_element_type=F32)
        if pp is not None:
            acc = acc + jnp.dot(pp.astype(BF16), load(v_ref, pbase, d), preferred_element_type=F32)
        return acc

    groups = ((q0, k0, v0), (q1, k1, v1), (q2, k2, v2))

    def run_blocks(g, d, specs):
        refs = groups[g]
        logits = [block(g, refs, d, base, pbase, has_prev) for base, pbase, has_prev in specs]
        probs = [softmax(lc, lp) for lc, lp in logits]
        accs = [weighted(refs, d, base, pbase, pc, pp)
                for (base, pbase, _), (pc, pp, _, _) in zip(specs, probs)]
        for (base, _, _), acc, (_, _, m, s) in zip(specs, accs, probs):
            store(g, base, d, acc, m, s)

    for g, (_, d) in enumerate(ATT_GROUPS):
        span = BLK * d
        n_blocks = SEQ // span
        if n_blocks == 1:
            def body_r(rr, carry, g=g, d=d):
                run_blocks(g, d, [(rr * ATT_ILP + u, None, None) for u in range(ATT_ILP)])
                return carry
            lax.fori_loop(0, d // ATT_ILP, body_r, 0)
        elif d == 1:
            def body_n(nn, carry, g=g, span=span):
                specs = []
                for u in range(ATT_ILP):
                    n = nn * ATT_ILP + u
                    specs.append((pl.multiple_of(n * span, BLK),
                                  pl.multiple_of(jnp.maximum(n - 1, 0) * span, BLK), n > 0))
                run_blocks(g, 1, specs)
                return carry
            lax.fori_loop(0, n_blocks // ATT_ILP, body_n, 0)
        else:
            assert d == ATT_ILP
            def body_n(n, carry, g=g, d=d, span=span):
                run_blocks(g, d, [(n * span + r, jnp.maximum(n - 1, 0) * span + r, n > 0)
                                  for r in range(d)])
                return carry
            lax.fori_loop(0, n_blocks, body_n, 0)

    rows = 256

    def merge(c, carry):
        r0 = pl.multiple_of(c * rows, rows)
        sl = pl.ds(r0, rows)
        m0, m1, m2 = m_s[0, sl, :], m_s[1, sl, :], m_s[2, sl, :]
        mm = jnp.maximum(jnp.maximum(m0, m1), m2)
        w0, w1, w2 = jnp.exp(m0 - mm), jnp.exp(m1 - mm), jnp.exp(m2 - mm)
        num = w0 * acc_s[0, sl, :] + w1 * acc_s[1, sl, :] + w2 * acc_s[2, sl, :]
        den = w0 * s_s[0, sl, :] + w1 * s_s[1, sl, :] + w2 * s_s[2, sl, :]
        att = num / den
        o_ref[sl, :] = (att * _silu(z_ref[sl, :])).astype(o_ref.dtype)
        return carry

    lax.fori_loop(0, SEQ // rows, merge, 0)


def _attn_prompt(proj3, bias_p):
    def head_spec(col0):
        return pl.BlockSpec((None, SEQ, HEAD_DIM),
                            lambda b, h, c=col0 // HEAD_DIM: (b, 0, c + h))

    in_specs = []
    for g in range(N_GROUPS):
        for col in (COL_Q, COL_K, COL_V):
            in_specs.append(head_spec(col + g * ATT_WIDTH))
    in_specs.append(head_spec(COL_ZATT))
    in_specs.append(pl.BlockSpec((N_GROUPS, None, BLK, 2 * BLK), lambda b, h: (0, h, 0, 0)))
    scratch = [pltpu.VMEM((N_GROUPS, SEQ, HEAD_DIM), F32) for _ in range(3)]
    return pl.pallas_call(
        _attn_prompt_kernel,
        grid=(BATCH, H_PER_GROUP),
        in_specs=in_specs,
        out_specs=pl.BlockSpec((None, SEQ, HEAD_DIM), lambda b, h: (b, 0, h)),
        out_shape=jax.ShapeDtypeStruct((BATCH, SEQ, ATT_WIDTH), BF16),
        scratch_shapes=scratch,
        compiler_params=_cparams(2),
        name="attn_prompt",
    )(*([proj3] * 10), bias_p)


def _lookup(tbl, idx):
    idx = np.asarray(idx, np.int32)
    onehot = (jnp.asarray(idx.reshape(-1, 1)) == jnp.arange(tbl.shape[0], dtype=jnp.int32)[None, :])
    out = jnp.dot(onehot.astype(F32), tbl.astype(F32), precision=lax.Precision.HIGHEST)
    return out.reshape(idx.shape + tbl.shape[1:])


def _prompt_bias(rel_bias):
    qi = np.arange(BLK)[:, None]
    ki = np.arange(2 * BLK)[None, :]
    du = qi + BLK - ki
    band = (du >= 0) & (du <= BLK)
    out = []
    for g, (win, dil) in enumerate(ATT_GROUPS):
        assert win // dil == BLK
        tbl = rel_bias[:, g * H_PER_GROUP:(g + 1) * H_PER_GROUP]
        b = _lookup(tbl, _t5_bucket(np.clip(du, 0, None) * dil))
        b = jnp.where(band[:, :, None], b, NEG)
        out.append(jnp.transpose(b, (2, 0, 1)))
    return jnp.stack(out)


def _softplus(x):
    return jnp.maximum(x, 0.0) + jnp.log1p(jnp.exp(-jnp.abs(x)))


def _lru_gates(xc, wr_ref, br, wi_ref, bi, nsp, n_blocks):
    a_parts, u_parts = [], []
    for n in range(n_blocks):
        sl = slice(n * LRU_BLOCK, (n + 1) * LRU_BLOCK)
        xn = xc[:, sl]
        xb = xn.astype(BF16)
        r = _sigmoid(jnp.dot(xb, wr_ref[n].astype(BF16), preferred_element_type=F32) + br[:, sl])
        i = _sigmoid(jnp.dot(xb, wi_ref[n].astype(BF16), preferred_element_type=F32) + bi[:, sl])
        log_a = (-LRU_C * r) * nsp[:, sl]
        a = jnp.exp(log_a)
        one_minus_a2 = -jnp.tanh(log_a) * (a * a + 1.0)
        a_parts.append(a)
        u_parts.append(jnp.sqrt(one_minus_a2) * (i * xn))
    return a_parts, u_parts


def _scan8(a, u, rows):
    for s in (1, 2, 4):
        keep = rows >= s
        a_sh = pltpu.roll(a, s, 0)
        u_sh = pltpu.roll(u, s, 0)
        u = jnp.where(keep, a * u_sh + u, u)
        a = jnp.where(keep, a * a_sh, a)
    return a, u


def _lru_prompt_kernel(x_ref, z_ref, cw_ref, cb_ref, wr_ref, br_ref, wi_ref, bi_ref, lam_ref,
                       o_ref, conv_ref, hl_ref, xpad, a_s, u_s):
    T, C = x_ref.shape
    n_blocks = C // LRU_BLOCK
    R = 256
    pad = 8

    xpad[0:pad, :] = jnp.zeros((pad, C), F32)

    def copy(c, carry):
        r0 = pl.multiple_of(c * R, R)
        xpad[pl.ds(r0 + pad, R), :] = x_ref[pl.ds(r0, R), :]
        return carry

    lax.fori_loop(0, T // R, copy, 0)
    conv_ref[...] = x_ref[T - (CONV_WIDTH - 1):T, :]

    nsp = _softplus(-lam_ref[...])
    br = br_ref[...]
    bi = bi_ref[...]
    cb = cb_ref[...]

    def gates(c, carry):
        r0 = pl.multiple_of(c * R, R)
        xw = xpad[pl.ds(r0, R + pad), :]
        xc = xw[pad:] * cw_ref[CONV_WIDTH - 1:CONV_WIDTH, :]
        for tap in range(CONV_WIDTH - 1):
            shifted = pltpu.roll(xw, CONV_WIDTH - 1 - tap, 0)[pad:]
            xc = xc + shifted * cw_ref[tap:tap + 1, :]
        xc = xc + cb
        a_parts, u_parts = _lru_gates(xc, wr_ref, br, wi_ref, bi, nsp, n_blocks)
        for n in range(n_blocks):
            sl = slice(n * LRU_BLOCK, (n + 1) * LRU_BLOCK)
            a_s[pl.ds(r0, R), sl] = a_parts[n]
            u_s[pl.ds(r0, R), sl] = u_parts[n]
        return carry

    lax.fori_loop(0, T // R, gates, 0)

    rows = lax.broadcasted_iota(jnp.int32, (8, C), 0)

    def scan(c, h_prev):
        t0 = pl.multiple_of(c * 16, 16)
        hs = []
        for half in range(2):
            sl = pl.ds(t0 + 8 * half, 8)
            a, u = _scan8(a_s[sl, :], u_s[sl, :], rows)
            h = a * h_prev + u
            hs.append(h)
            h_prev = h[7:8, :]
        h16 = jnp.concatenate(hs, axis=0)
        z = z_ref[pl.ds(t0, 16), :]
        o_ref[pl.ds(t0, 16), :] = (h16 * _silu(z)).astype(o_ref.dtype)
        return h_prev

    h_last = lax.fori_loop(0, T // 16, scan, jnp.zeros((1, C), F32))
    hl_ref[...] = h_last


def _lru_prompt(proj3, conv_w, conv_b, w_r, b_r, w_i, b_i, lam, layer):
    tc = 512
    nb = tc // LRU_BLOCK
    n_ct = LRU_WIDTH // tc

    def col_spec(col0):
        return pl.BlockSpec((None, SEQ, tc), lambda b, c, c0=col0 // tc: (b, 0, c0 + c))

    vec_spec = pl.BlockSpec((None, 1, tc), lambda b, c: (layer, 0, c))
    w_spec = pl.BlockSpec((None, nb, LRU_BLOCK, LRU_BLOCK), lambda b, c: (layer, c, 0, 0))
    return pl.pallas_call(
        _lru_prompt_kernel,
        grid=(BATCH, n_ct),
        in_specs=[
            col_spec(COL_XLRU), col_spec(COL_ZLRU),
            pl.BlockSpec((None, CONV_WIDTH, tc), lambda b, c: (layer, 0, c)),
            vec_spec, w_spec, vec_spec, w_spec, vec_spec, vec_spec,
        ],
        out_specs=[
            pl.BlockSpec((None, SEQ, tc), lambda b, c: (b, 0, c)),
            pl.BlockSpec((None, CONV_WIDTH - 1, tc), lambda b, c: (b, 0, c)),
            pl.BlockSpec((None, 1, tc), lambda b, c: (b, 0, c)),
        ],
        out_shape=[
            jax.ShapeDtypeStruct((BATCH, SEQ, LRU_WIDTH), BF16),
            jax.ShapeDtypeStruct((BATCH, CONV_WIDTH - 1, LRU_WIDTH), F32),
            jax.ShapeDtypeStruct((BATCH, 1, LRU_WIDTH), F32),
        ],
        scratch_shapes=[
            pltpu.VMEM((SEQ + 8, tc), F32),
            pltpu.VMEM((SEQ, tc), F32),
            pltpu.VMEM((SEQ, tc), F32),
        ],
        compiler_params=_cparams(2),
        name="lru_prompt",
    )(proj3, proj3, conv_w, conv_b, w_r, b_r, w_i, b_i, lam)


def _mix_kernel(att_ref, lru_ref, ga_ref, gl_ref, wpa_ref, wpb_ref, o_ref):
    y_att = jnp.dot(att_ref[...], wpa_ref[...], preferred_element_type=F32)
    y_lru = jnp.dot(lru_ref[...], wpb_ref[...], preferred_element_type=F32)
    merged = _sigmoid(ga_ref[...]) * y_att + _sigmoid(gl_ref[...]) * y_lru
    o_ref[...] = merged.astype(o_ref.dtype)


def _mix(att, lru, proj, wpa, wpb, layer, tm):
    m_rows = att.shape[0]
    return pl.pallas_call(
        _mix_kernel,
        grid=(m_rows // tm,),
        in_specs=[
            pl.BlockSpec((tm, ATT_WIDTH), lambda i: (i, 0)),
            pl.BlockSpec((tm, LRU_WIDTH), lambda i: (i, 0)),
            pl.BlockSpec((tm, D_MODEL), lambda i: (i, COL_GATT // D_MODEL)),
            pl.BlockSpec((tm, D_MODEL), lambda i: (i, COL_GLRU // D_MODEL)),
            pl.BlockSpec((None, ATT_WIDTH, D_MODEL), lambda i: (layer, 0, 0)),
            pl.BlockSpec((None, LRU_WIDTH, D_MODEL), lambda i: (layer, 0, 0)),
        ],
        out_specs=pl.BlockSpec((tm, D_MODEL), lambda i: (i, 0)),
        out_shape=jax.ShapeDtypeStruct((m_rows, D_MODEL), BF16),
        compiler_params=_cparams(1),
        name="branch_mix",
    )(att, lru, proj, proj, wpa, wpb)


def _residual_kernel(m_ref, w_ref, x_ref, gate_ref, o_ref):
    out = jnp.dot(m_ref[...], w_ref[...], preferred_element_type=F32)
    o_ref[...] = x_ref[...] + gate_ref[...] * out


def _residual_prompt(merged, w_out, x, gate, layer):
    tm = 512
    row_spec = pl.BlockSpec((None, tm, D_MODEL), lambda b, i: (b, i, 0))
    return pl.pallas_call(
        _residual_kernel,
        grid=(BATCH, SEQ // tm),
        in_specs=[
            row_spec,
            pl.BlockSpec((None, D_MODEL, D_MODEL), lambda b, i: (layer, 0, 0)),
            row_spec,
            pl.BlockSpec((None, 1, D_MODEL), lambda b, i: (b, 0, 0)),
        ],
        out_specs=row_spec,
        out_shape=jax.ShapeDtypeStruct(x.shape, F32),
        compiler_params=_cparams(2),
        name="residual_prompt",
    )(merged, w_out, x, gate)


def _residual_sample(merged, w_out, x, gate_rows, layer):
    full = pl.BlockSpec((N_SAMPLE_ROWS, D_MODEL), lambda i: (0, 0))
    return pl.pallas_call(
        _residual_kernel,
        grid=(1,),
        in_specs=[full, pl.BlockSpec((None, D_MODEL, D_MODEL), lambda i: (layer, 0, 0)), full, full],
        out_specs=full,
        out_shape=jax.ShapeDtypeStruct(x.shape, F32),
        compiler_params=_cparams(1),
        name="residual_sample",
    )(merged, w_out, x, gate_rows)


CACHE_CHUNK_ROWS = 512
CACHE_SLOTS = 6
CACHE_LOOKAHEAD = 3


def _cache_update_kernel(c0, c1, c2, n0, n1, n2, o0, o1, o2, buf, in_sem, out_sem, new_sem):
    chunks = []
    new_copies = []
    for gi, (c, n, o) in enumerate(((c0, n0, o0), (c1, n1, o1), (c2, n2, o2))):
        keep = c.shape[2] - DEC_SEQ
        for l in range(DEPTH):
            new_copies.append(pltpu.make_async_copy(
                n.at[l], o.at[l, :, pl.ds(keep, DEC_SEQ)], new_sem.at[gi, l]))
            for b in range(DEC_BATCH):
                for r in range(0, keep, CACHE_CHUNK_ROWS):
                    rows = min(CACHE_CHUNK_ROWS, keep - r)
                    chunks.append((c.at[l, b, pl.ds(DEC_SEQ + r, rows)],
                                   o.at[l, b, pl.ds(r, rows)], rows))

    def read(i):
        src, _, rows = chunks[i]
        slot = i % CACHE_SLOTS
        return pltpu.make_async_copy(src, buf.at[slot, pl.ds(0, rows)], in_sem.at[slot])

    def write(i):
        _, dst, rows = chunks[i]
        slot = i % CACHE_SLOTS
        return pltpu.make_async_copy(buf.at[slot, pl.ds(0, rows)], dst, out_sem.at[slot])

    for cp in new_copies:
        cp.start()
    n_chunks = len(chunks)
    for i in range(n_chunks + CACHE_LOOKAHEAD):
        if i < n_chunks:
            if i >= CACHE_SLOTS:
                write(i - CACHE_SLOTS).wait()
            read(i).start()
        j = i - CACHE_LOOKAHEAD
        if j >= 0:
            read(j).wait()
            write(j).start()
    for j in range(max(n_chunks - CACHE_SLOTS, 0), n_chunks):
        write(j).wait()
    for cp in new_copies:
        cp.wait()


def _cache_update(caches, new_rows):
    any_spec = pl.BlockSpec(memory_space=pl.ANY)
    vmem_spec = pl.BlockSpec(memory_space=pltpu.VMEM)
    return pl.pallas_call(
        _cache_update_kernel,
        in_specs=[any_spec] * 3 + [vmem_spec] * 3,
        out_specs=[any_spec] * 3,
        out_shape=[jax.ShapeDtypeStruct(c.shape, c.dtype) for c in caches],
        scratch_shapes=[
            pltpu.VMEM((CACHE_SLOTS, CACHE_CHUNK_ROWS, 2, H_PER_GROUP, HEAD_DIM), F32),
            pltpu.SemaphoreType.DMA((CACHE_SLOTS,)),
            pltpu.SemaphoreType.DMA((CACHE_SLOTS,)),
            pltpu.SemaphoreType.DMA((N_GROUPS, DEPTH)),
        ],
        compiler_params=pltpu.CompilerParams(vmem_limit_bytes=VMEM_LIMIT),
        name="cache_update",
    )(*caches, *new_rows)


def _attn_sample_kernel(q_ref, k_ref, v_ref, z_ref, c0_ref, c1_ref, c2_ref, bc_ref, bn_ref,
                        o_ref, kv0_ref, kv1_ref, kv2_ref):
    hp = H_PER_GROUP
    kv_refs = (kv0_ref, kv1_ref, kv2_ref)
    k_new, v_new = [], []
    for g in range(N_GROUPS):
        kn = k_ref[:, g * hp:(g + 1) * hp, :]
        vn = v_ref[:, g * hp:(g + 1) * hp, :]
        kv_refs[g][:, 0] = kn
        kv_refs[g][:, 1] = vn
        k_new.append(kn)
        v_new.append(vn)

    for s in range(DEC_SEQ):
        accs, ms, ss = [], [], []
        for g in range(N_GROUPS):
            qg = q_ref[s, g * hp:(g + 1) * hp, :]
            if g == 0:
                kc, vc = c0_ref[:, 0], c0_ref[:, 1]
            elif g == 1:
                kc, vc = c1_ref[:, s, 0], c1_ref[:, s, 1]
            else:
                kc, vc = c2_ref[:, s, 0], c2_ref[:, s, 1]
            lc = jnp.sum(kc * qg[None], axis=-1, keepdims=True) * ATT_SCALE + bc_ref[g, s]
            ln = jnp.sum(k_new[g] * qg[None], axis=-1, keepdims=True) * ATT_SCALE + bn_ref[g, s]
            m = jnp.maximum(jnp.max(lc, axis=0), jnp.max(ln, axis=0))
            pc = jnp.exp(lc - m[None])
            pn = jnp.exp(ln - m[None])
            ss.append(jnp.sum(pc, axis=0) + jnp.sum(pn, axis=0))
            accs.append(jnp.sum(pc * vc, axis=0) + jnp.sum(pn * v_new[g], axis=0))
            ms.append(m)
        mm = jnp.maximum(jnp.maximum(ms[0], ms[1]), ms[2])
        ws = [jnp.exp(m - mm) for m in ms]
        num = ws[0] * accs[0] + ws[1] * accs[1] + ws[2] * accs[2]
        den = ws[0] * ss[0] + ws[1] * ss[1] + ws[2] * ss[2]
        o_ref[s] = (num / den) * _silu(z_ref[s])


def _attn_sample(proj_s4, caches, bias_c, bias_n, layer):
    c0, c1, c2 = caches
    hp = H_PER_GROUP

    def head_spec(col0, n_heads):
        return pl.BlockSpec((None, DEC_SEQ, n_heads, HEAD_DIM),
                            lambda b, c=col0 // (HEAD_DIM * n_heads): (b, 0, c, 0))

    in_specs = [
        head_spec(COL_Q, N_ATT_HEADS), head_spec(COL_K, N_ATT_HEADS), head_spec(COL_V, N_ATT_HEADS),
        head_spec(COL_ZATT, hp),
        pl.BlockSpec((None, None, BLK, 2, hp, HEAD_DIM), lambda b: (layer, b, 0, 0, 0, 0)),
        pl.BlockSpec((None, None, BLK, DEC_SEQ, 2, hp, HEAD_DIM), lambda b: (layer, b, 0, 0, 0, 0, 0)),
        pl.BlockSpec((None, None, BLK, DEC_SEQ, 2, hp, HEAD_DIM), lambda b: (layer, b, 0, 0, 0, 0, 0)),
        pl.BlockSpec(bias_c.shape, lambda b: (0,) * 5),
        pl.BlockSpec(bias_n.shape, lambda b: (0,) * 5),
    ]
    out_specs = [pl.BlockSpec((None, DEC_SEQ, hp, HEAD_DIM), lambda b: (b, 0, 0, 0))]
    out_shape = [jax.ShapeDtypeStruct((DEC_BATCH, DEC_SEQ, hp, HEAD_DIM), F32)]
    for _ in range(N_GROUPS):
        out_specs.append(pl.BlockSpec((None, DEC_SEQ, 2, hp, HEAD_DIM), lambda b: (b, 0, 0, 0, 0)))
        out_shape.append(jax.ShapeDtypeStruct((DEC_BATCH, DEC_SEQ, 2, hp, HEAD_DIM), F32))
    return pl.pallas_call(
        _attn_sample_kernel,
        grid=(DEC_BATCH,),
        in_specs=in_specs,
        out_specs=out_specs,
        out_shape=out_shape,
        compiler_params=_cparams(1),
        name="attn_sample",
    )(proj_s4, proj_s4, proj_s4, proj_s4, c0, c1, c2, bias_c, bias_n)


def _sample_bias(rel_bias):
    s_idx = np.arange(DEC_SEQ)
    bc, bn = [], []
    for g, (win, dil) in enumerate(ATT_GROUPS):
        tbl = rel_bias[:, g * H_PER_GROUP:(g + 1) * H_PER_GROUP].astype(F32)
        key = np.arange(BLK)
        if dil == 1:
            j = BLK + s_idx[:, None] - key[None, :]
            valid = j <= BLK
        else:
            j = np.broadcast_to(BLK - key[None, :], (DEC_SEQ, BLK))
            valid = np.ones_like(j, dtype=bool)
        b = _lookup(tbl, _t5_bucket(np.clip(j, 0, None) * dil))
        bc.append(jnp.where(valid[:, :, None], b, NEG))
        jn = s_idx[:, None] - s_idx[None, :]
        valid_n = (jn >= 0) & (jn * dil <= win) & ((jn == 0) | (dil == 1))
        b = _lookup(tbl, _t5_bucket(np.clip(jn, 0, None) * dil))
        bn.append(jnp.where(valid_n[:, :, None], b, NEG))
    bc = jnp.stack(bc)
    bn = jnp.stack(bn)
    bc = jnp.broadcast_to(bc[..., None], bc.shape + (HEAD_DIM,))
    bn = jnp.broadcast_to(bn[..., None], bn.shape + (HEAD_DIM,))
    return bc, bn


def _lru_sample_kernel(x_ref, z_ref, cs_ref, h0_ref, cw_ref, cb_ref, wr_ref, br_ref, wi_ref, bi_ref,
                       lam_ref, o_ref, conv_ref, hl_ref):
    S = DEC_SEQ
    xp = [cs_ref[t] for t in range(CONV_WIDTH - 1)] + [x_ref[t] for t in range(S)]
    xc = []
    for t in range(S):
        y = xp[t] * cw_ref[0:1, :]
        for tap in range(1, CONV_WIDTH):
            y = y + xp[t + tap] * cw_ref[tap:tap + 1, :]
        xc.append(y + cb_ref[...])
    xcat = jnp.concatenate(xc, axis=0)
    nsp = _softplus(-lam_ref[...])
    a_parts, u_parts = _lru_gates(xcat, wr_ref, br_ref[...], wi_ref, bi_ref[...], nsp, LRU_BLOCKS)
    a = jnp.concatenate(a_parts, axis=1)
    u = jnp.concatenate(u_parts, axis=1)
    nb = DEC_BATCH
    h = h0_ref[...]
    for t in range(S):
        h = a[t * nb:(t + 1) * nb] * h + u[t * nb:(t + 1) * nb]
        o_ref[t] = h * _silu(z_ref[t])
    for t in range(CONV_WIDTH - 1):
        conv_ref[t] = xp[S + t]
    hl_ref[...] = h


def _lru_sample(x_t, z_t, conv_t, h0, conv_w, conv_b, w_r, b_r, w_i, b_i, lam, layer):
    C = LRU_WIDTH

    def full(shape):
        return pl.BlockSpec(shape, lambda i: (0,) * len(shape))

    vec_spec = pl.BlockSpec((None, 1, C), lambda i: (layer, 0, 0))
    w_spec = pl.BlockSpec((None, LRU_BLOCKS, LRU_BLOCK, LRU_BLOCK), lambda i: (layer, 0, 0, 0))
    return pl.pallas_call(
        _lru_sample_kernel,
        grid=(1,),
        in_specs=[
            full((DEC_SEQ, DEC_BATCH, C)), full((DEC_SEQ, DEC_BATCH, C)),
            full((CONV_WIDTH - 1, DEC_BATCH, C)), full((DEC_BATCH, C)),
            pl.BlockSpec((None, CONV_WIDTH, C), lambda i: (layer, 0, 0)),
            vec_spec, w_spec, vec_spec, w_spec, vec_spec, vec_spec,
        ],
        out_specs=[full((DEC_SEQ, DEC_BATCH, C)), full((CONV_WIDTH - 1, DEC_BATCH, C)),
                   full((DEC_BATCH, C))],
        out_shape=[
            jax.ShapeDtypeStruct((DEC_SEQ, DEC_BATCH, C), F32),
            jax.ShapeDtypeStruct((CONV_WIDTH - 1, DEC_BATCH, C), F32),
            jax.ShapeDtypeStruct((DEC_BATCH, C), F32),
        ],
        compiler_params=_cparams(1),
        name="lru_sample",
    )(x_t, z_t, conv_t, h0, conv_w, conv_b, w_r, b_r, w_i, b_i, lam)


def kernel(x_prompt, x_sample, c_prompt, c_sample, cache_kv_g0, cache_kv_g1, cache_kv_g2, state_conv, state_h, rel_bias, w_ada, b_ada, norm_g, w_in, conv_w, conv_b, w_r, b_r, w_i, b_i, lam, w_pa, w_pb, w_out, final_g):
    L, B, T, D = DEPTH, BATCH, SEQ, D_MODEL
    Bd, S = DEC_BATCH, DEC_SEQ

    c_all = jnp.concatenate(
        [c_prompt, c_sample, jnp.zeros((MOD_ROWS - B - Bd, D), F32)], axis=0)
    mod = _modulation(c_all, w_ada, b_ada).reshape(L, MOD_ROWS, 3, D)
    mod_p = mod[:, :B]
    mod_s = jnp.repeat(mod[:, B:B + Bd], S, axis=1)

    wpa_bf = w_pa.astype(BF16)
    wpb_bf = w_pb.astype(BF16)
    wout_bf = w_out.astype(BF16)
    conv_b3 = conv_b.reshape(L, 1, LRU_WIDTH)
    b_r3 = b_r.reshape(L, 1, LRU_WIDTH)
    b_i3 = b_i.reshape(L, 1, LRU_WIDTH)
    lam3 = lam.reshape(L, 1, LRU_WIDTH)

    bias_p = _prompt_bias(rel_bias)
    bias_c, bias_n = _sample_bias(rel_bias)

    cache_views = (
        cache_kv_g0,
        cache_kv_g1.reshape(L, Bd, BLK, 4, 2, H_PER_GROUP, HEAD_DIM),
        cache_kv_g2.reshape(L, Bd, BLK, 16, 2, H_PER_GROUP, HEAD_DIM),
    )

    xp = x_prompt
    xs = x_sample.reshape(Bd * S, D)
    kv_p = ([], [], [])
    kv_new = ([], [], [])
    conv_p, h_p, conv_s, h_s = [], [], [], []
    for l in range(L):
        shift_p, scale_p, gate_p = (mod_p[l, :, i][:, None, :] for i in range(3))
        shift_s, scale_s, gate_s = (mod_s[l, :, i] for i in range(3))

        hp_ = _norm_prompt(xp, norm_g[l], scale_p, shift_p, BF16).reshape(B * T, D)
        hs_ = _norm_sample(xs, norm_g[l], scale_s, shift_s, BF16)
        proj_p, proj_s, *kv_l = _in_proj(hp_, hs_, w_in, l)
        proj3 = proj_p.reshape(B, T, IN_COLS)
        for g in range(N_GROUPS):
            kv_p[g].append(kv_l[g])

        att_p = _attn_prompt(proj3, bias_p)
        lru_p, cv, hl = _lru_prompt(proj3, conv_w, conv_b3, w_r, b_r3, w_i, b_i3, lam3, l)
        merged_p = _mix(att_p.reshape(B * T, ATT_WIDTH), lru_p.reshape(B * T, LRU_WIDTH),
                        proj_p, wpa_bf, wpb_bf, l, 256)
        xp = _residual_prompt(merged_p.reshape(B, T, D), wout_bf, xp, gate_p, l)
        conv_p.append(cv)
        h_p.append(hl.reshape(B, LRU_WIDTH))

        proj_s4 = proj_s.reshape(Bd, S, IN_COLS // HEAD_DIM, HEAD_DIM)
        outs = _attn_sample(proj_s4, cache_views, bias_c, bias_n, l)
        att_s = outs[0]
        for g in range(N_GROUPS):
            kv_new[g].append(outs[1 + g])
        ps3 = proj_s.reshape(Bd, S, IN_COLS)
        x_t = jnp.transpose(ps3[:, :, COL_XLRU:COL_XLRU + LRU_WIDTH], (1, 0, 2))
        z_t = jnp.transpose(ps3[:, :, COL_ZLRU:COL_ZLRU + LRU_WIDTH], (1, 0, 2))
        conv_t = jnp.transpose(state_conv[l], (1, 0, 2))
        lru_t, cv_t, hl_s = _lru_sample(x_t, z_t, conv_t, state_h[l], conv_w, conv_b3,
                                        w_r, b_r3, w_i, b_i3, lam3, l)
        lru_s = jnp.transpose(lru_t, (1, 0, 2)).reshape(Bd * S, LRU_WIDTH).astype(BF16)
        merged_s = _mix(att_s.reshape(Bd * S, ATT_WIDTH).astype(BF16), lru_s, proj_s,
                        wpa_bf, wpb_bf, l, Bd * S)
        xs = _residual_sample(merged_s, wout_bf, xs, gate_s, l)
        conv_s.append(jnp.transpose(cv_t, (1, 0, 2)))
        h_s.append(hl_s)

    y_prompt = _norm_prompt(xp, final_g, None, None, F32)
    y_sample = _norm_sample(xs, final_g, None, None, F32).reshape(Bd, S, D)
    kv_s = _cache_update((cache_kv_g0, cache_kv_g1, cache_kv_g2),
                         tuple(jnp.stack(rows) for rows in kv_new))
    return (y_prompt, y_sample,
            jnp.stack(kv_p[0]), jnp.stack(kv_p[1]), jnp.stack(kv_p[2]),
            jnp.stack(conv_p), jnp.stack(h_p),
            kv_s[0], kv_s[1], kv_s[2],
            jnp.stack(conv_s), jnp.stack(h_s))
```

```python
import functools
import math

import jax
import jax.numpy as jnp
import numpy as np
from jax import lax
from jax.experimental import pallas as pl
from jax.experimental.pallas import tpu as pltpu

D_MODEL = 2048
BATCH = 4
SEQ = 2048
DEPTH = 4
DEC_BATCH = 8
DEC_SEQ = 4
HEAD_DIM = 128
H_PER_GROUP = 8
ATT_GROUPS = ((128, 1), (512, 4), (2048, 16))
N_GROUPS = 3
N_ATT_HEADS = N_GROUPS * H_PER_GROUP
QKV_WIDTH = N_ATT_HEADS * HEAD_DIM
ATT_WIDTH = H_PER_GROUP * HEAD_DIM
N_BUCKETS = 32
REL_MAX_DIST = 2048
BLK = 128
LRU_WIDTH = D_MODEL
LRU_BLOCKS = 16
LRU_BLOCK = LRU_WIDTH // LRU_BLOCKS
CONV_WIDTH = 4
LRU_C = 8.0
IN_COLS = 3 * QKV_WIDTH + ATT_WIDTH + 2 * LRU_WIDTH + 2 * D_MODEL
RMS_EPS = 1e-6
NEG = -1e30
ATT_SCALE = HEAD_DIM ** -0.5

COL_Q = 0
COL_K = QKV_WIDTH
COL_V = 2 * QKV_WIDTH
COL_ZATT = 3 * QKV_WIDTH
COL_XLRU = COL_ZATT + ATT_WIDTH
COL_ZLRU = COL_XLRU + LRU_WIDTH
COL_GATT = COL_ZLRU + LRU_WIDTH
COL_GLRU = COL_GATT + D_MODEL

N_PROMPT_ROWS = BATCH * SEQ
N_SAMPLE_ROWS = DEC_BATCH * DEC_SEQ
MOD_ROWS = 16

VMEM_LIMIT = 52 * 1024 * 1024
ATT_ILP = 8

F32 = jnp.float32
BF16 = jnp.bfloat16


def _cparams(n_grid_dims, vmem_limit=VMEM_LIMIT):
    return pltpu.CompilerParams(
        dimension_semantics=("arbitrary",) * n_grid_dims,
        vmem_limit_bytes=vmem_limit)


def _sigmoid(x):
    return 0.5 * jnp.tanh(0.5 * x) + 0.5


def _silu(x):
    return x * _sigmoid(x)


def _t5_bucket(dist):
    dist = np.asarray(dist).astype(np.int32)
    max_exact = N_BUCKETS // 2
    safe = np.maximum(dist, 1).astype(np.float32)
    large = max_exact + (np.log(safe / max_exact) / np.float32(math.log(REL_MAX_DIST / max_exact))
                         * (N_BUCKETS - max_exact)).astype(np.int32)
    large = np.minimum(large, N_BUCKETS - 1)
    return np.where(dist < max_exact, dist, large).astype(np.int32)


def _mod_kernel(c_ref, w_ref, b_ref, o_ref):
    c = _silu(c_ref[...])
    o_ref[...] = jnp.dot(c.astype(BF16), w_ref[...].astype(BF16),
                         preferred_element_type=F32) + b_ref[...]


def _modulation(c_all, w_ada, b_ada):
    tn = 1024
    n_cols = 3 * D_MODEL
    return pl.pallas_call(
        _mod_kernel,
        grid=(DEPTH, n_cols // tn),
        in_specs=[
            pl.BlockSpec((MOD_ROWS, D_MODEL), lambda l, j: (0, 0)),
            pl.BlockSpec((None, D_MODEL, tn), lambda l, j: (l, 0, j)),
            pl.BlockSpec((None, 1, tn), lambda l, j: (l, 0, j)),
        ],
        out_specs=pl.BlockSpec((None, MOD_ROWS, tn), lambda l, j: (l, 0, j)),
        out_shape=jax.ShapeDtypeStruct((DEPTH, MOD_ROWS, n_cols), F32),
        compiler_params=_cparams(2),
        name="adaln_mod",
    )(c_all, w_ada, b_ada.reshape(DEPTH, 1, n_cols))


def _norm_kernel(x_ref, g_ref, *rest, modulate):
    if modulate:
        sc_ref, sh_ref, o_ref = rest
    else:
        (o_ref,) = rest
    x = x_ref[...]
    y = x * lax.rsqrt(jnp.mean(x * x, axis=-1, keepdims=True) + RMS_EPS)
    y = y * g_ref[...]
    if modulate:
        y = y * (1.0 + sc_ref[...]) + sh_ref[...]
    o_ref[...] = y.astype(o_ref.dtype)


def _norm_prompt(x, g, scale, shift, out_dtype):
    tm = 512
    modulate = scale is not None
    row_spec = pl.BlockSpec((None, tm, D_MODEL), lambda b, i: (b, i, 0))
    vec_spec = pl.BlockSpec((None, 1, D_MODEL), lambda b, i: (b, 0, 0))
    in_specs = [row_spec, pl.BlockSpec((1, D_MODEL), lambda b, i: (0, 0))]
    args = [x, g.reshape(1, D_MODEL)]
    if modulate:
        in_specs += [vec_spec, vec_spec]
        args += [scale, shift]
    return pl.pallas_call(
        functools.partial(_norm_kernel, modulate=modulate),
        grid=(BATCH, SEQ // tm),
        in_specs=in_specs,
        out_specs=row_spec,
        out_shape=jax.ShapeDtypeStruct(x.shape, out_dtype),
        compiler_params=_cparams(2),
        name="rmsnorm_prompt",
    )(*args)


def _norm_sample(x, g, scale_rows, shift_rows, out_dtype):
    modulate = scale_rows is not None
    args = [x, g.reshape(1, D_MODEL)]
    if modulate:
        args += [scale_rows, shift_rows]
    return pl.pallas_call(
        functools.partial(_norm_kernel, modulate=modulate),
        out_shape=jax.ShapeDtypeStruct(x.shape, out_dtype),
        name="rmsnorm_sample",
    )(*args)


IN_PROJ_VMEM_LIMIT = 58 * 1024 * 1024
IN_TM = 1024
IN_TN = ATT_WIDTH
IN_TILES_PER_SEQ = SEQ // IN_TM
K_TILE0 = COL_K // IN_TN
V_TILE0 = COL_V // IN_TN


def _kv_rows(g):
    w = min(ATT_GROUPS[g][0], SEQ)
    assert w % IN_TM == 0 or IN_TM % w == 0
    return min(w, IN_TM)


def _in_proj_kernel(h_ref, hs_ref, w_ref, o_ref, os_ref, kv0_ref, kv1_ref, kv2_ref, wbf_ref):
    j = pl.program_id(0)
    i = pl.program_id(1)

    @pl.when(i == 0)
    def _():
        wbf_ref[...] = w_ref[...].astype(BF16)
        os_ref[...] = jnp.dot(hs_ref[...], wbf_ref[...], preferred_element_type=F32)

    o_ref[...] = jnp.dot(h_ref[...], wbf_ref[...], preferred_element_type=F32)

    for g, kv_ref in enumerate((kv0_ref, kv1_ref, kv2_ref)):
        rows = _kv_rows(g)
        is_kv = (j == K_TILE0 + g) | (j == V_TILE0 + g)
        if min(ATT_GROUPS[g][0], SEQ) < SEQ:
            is_kv = is_kv & (i % IN_TILES_PER_SEQ == IN_TILES_PER_SEQ - 1)

        @pl.when(is_kv)
        def _(kv_ref=kv_ref, rows=rows):
            flat = kv_ref.reshape(rows * H_PER_GROUP, HEAD_DIM)
            for h in range(H_PER_GROUP):
                flat[pl.ds(h, rows, stride=H_PER_GROUP), :] = (
                    o_ref[IN_TM - rows:, h * HEAD_DIM:(h + 1) * HEAD_DIM])


def _kv_index_map(g):
    w = min(ATT_GROUPS[g][0], SEQ)
    n_i = N_PROMPT_ROWS // IN_TM
    full = w >= SEQ

    def index_map(j, i):
        k_col = K_TILE0 + g
        v_col = V_TILE0 + g
        c = jnp.where(j >= v_col, 1, 0)
        in_col = (j == k_col) | (j == v_col)
        if full:
            ii = jnp.where(in_col, i, jnp.where(j < k_col, 0, n_i - 1))
            return (ii // IN_TILES_PER_SEQ, ii % IN_TILES_PER_SEQ, c, 0, 0)
        b_col = jnp.maximum(i - (IN_TILES_PER_SEQ - 1), 0) // IN_TILES_PER_SEQ
        b = jnp.where(in_col, b_col, jnp.where(j < k_col, 0, BATCH - 1))
        return (b, 0, c, 0, 0)

    return index_map


def _in_proj(h_p, h_s, w_in, layer):
    tm, tn = IN_TM, IN_TN
    out_specs = [
        pl.BlockSpec((tm, tn), lambda j, i: (i, j)),
        pl.BlockSpec((N_SAMPLE_ROWS, tn), lambda j, i: (0, j)),
    ]
    out_shape = [
        jax.ShapeDtypeStruct((N_PROMPT_ROWS, IN_COLS), F32),
        jax.ShapeDtypeStruct((N_SAMPLE_ROWS, IN_COLS), F32),
    ]
    for g, (win, _) in enumerate(ATT_GROUPS):
        out_specs.append(pl.BlockSpec((None, _kv_rows(g), None, H_PER_GROUP, HEAD_DIM), _kv_index_map(g)))
        out_shape.append(jax.ShapeDtypeStruct((BATCH, min(win, SEQ), 2, H_PER_GROUP, HEAD_DIM), F32))
    return pl.pallas_call(
        _in_proj_kernel,
        grid=(IN_COLS // tn, N_PROMPT_ROWS // tm),
        in_specs=[
            pl.BlockSpec((tm, D_MODEL), lambda j, i: (i, 0)),
            pl.BlockSpec((N_SAMPLE_ROWS, D_MODEL), lambda j, i: (0, 0)),
            pl.BlockSpec((None, D_MODEL, tn), lambda j, i: (layer, 0, j)),
        ],
        out_specs=out_specs,
        out_shape=out_shape,
        scratch_shapes=[pltpu.VMEM((D_MODEL, tn), BF16)],
        compiler_params=_cparams(2, IN_PROJ_VMEM_LIMIT),
        name="in_proj",
    )(h_p, h_s, w_in)


def _attn_prompt_kernel(q0, k0, v0, q1, k1, v1, q2, k2, v2, z_ref, bias_ref, o_ref,
                        acc_s, m_s, s_s):
    nt = (((1,), (1,)), ((), ()))

    def load(ref, base, d):
        if d == 1:
            return ref[pl.ds(base, BLK), :].astype(BF16)
        return ref[pl.ds(base, BLK, stride=d), :].astype(BF16)

    def store(g, base, d, acc, m, s):
        idx = pl.ds(base, BLK) if d == 1 else pl.ds(base, BLK, stride=d)
        acc_s[g, idx, :] = acc
        m_s[g, idx, :] = jnp.broadcast_to(m, (BLK, HEAD_DIM))
        s_s[g, idx, :] = jnp.broadcast_to(s, (BLK, HEAD_DIM))

    def block(g, refs, d, base, pbase, has_prev):
        q_ref, k_ref, v_ref = refs
        q = load(q_ref, base, d)
        lc = lax.dot_general(q, load(k_ref, base, d), nt, preferred_element_type=F32)
        lc = lc * ATT_SCALE + bias_ref[g, :, BLK:]
        if has_prev is None:
            return lc, None
        lp = lax.dot_general(q, load(k_ref, pbase, d), nt, preferred_element_type=F32)
        lp = lp * ATT_SCALE + bias_ref[g, :, :BLK]
        return lc, jnp.where(has_prev, lp, NEG)

    def softmax(lc, lp):
        if lp is None:
            m = jnp.max(lc, axis=-1, keepdims=True)
            pc = jnp.exp(lc - m)
            return pc, None, m, jnp.sum(pc, axis=-1, keepdims=True)
        m = jnp.max(jnp.maximum(lc, lp), axis=-1, keepdims=True)
        pc = jnp.exp(lc - m)
        pp = jnp.exp(lp - m)
        return pc, pp, m, jnp.sum(pc + pp, axis=-1, keepdims=True)

    def weighted(refs, d, base, pbase, pc, pp):
        v_ref = refs[2]
        acc = jnp.dot(pc.astype(BF16), load(v_ref, base, d), preferred_element_type=F32)
        if pp is not None:
            acc = acc + jnp.dot(pp.astype(BF16), load(v_ref, pbase, d), preferred_element_type=F32)
        return acc

    groups = ((q0, k0, v0), (q1, k1, v1), (q2, k2, v2))

    def run_blocks(g, d, specs):
        refs = groups[g]
        logits = [block(g, refs, d, base, pbase, has_prev) for base, pbase, has_prev in specs]
        probs = [softmax(lc, lp) for lc, lp in logits]
        accs = [weighted(refs, d, base, pbase, pc, pp)
                for (base, pbase, _), (pc, pp, _, _) in zip(specs, probs)]
        for (base, _, _), acc, (_, _, m, s) in zip(specs, accs, probs):
            store(g, base, d, acc, m, s)

    for g, (_, d) in enumerate(ATT_GROUPS):
        span = BLK * d
        n_blocks = SEQ // span
        if n_blocks == 1:
            def body_r(rr, carry, g=g, d=d):
                run_blocks(g, d, [(rr * ATT_ILP + u, None, None) for u in range(ATT_ILP)])
                return carry
            lax.fori_loop(0, d // ATT_ILP, body_r, 0)
        elif d == 1:
            def body_n(nn, carry, g=g, span=span):
                specs = []
                for u in range(ATT_ILP):
                    n = nn * ATT_ILP + u
                    specs.append((pl.multiple_of(n * span, BLK),
                                  pl.multiple_of(jnp.maximum(n - 1, 0) * span, BLK), n > 0))
                run_blocks(g, 1, specs)
                return carry
            lax.fori_loop(0, n_blocks // ATT_ILP, body_n, 0)
        else:
            n_per_iter = ATT_ILP // d
            assert n_per_iter * d == ATT_ILP and n_blocks % n_per_iter == 0
            def body_n(nn, carry, g=g, d=d, span=span, n_per_iter=n_per_iter):
                specs = []
                for u in range(n_per_iter):
                    n = nn * n_per_iter + u
                    specs += [(n * span + r, jnp.maximum(n - 1, 0) * span + r, n > 0)
                              for r in range(d)]
                run_blocks(g, d, specs)
                return carry
            lax.fori_loop(0, n_blocks // n_per_iter, body_n, 0)

    rows = 256

    def merge(c, carry):
        r0 = pl.multiple_of(c * rows, rows)
        sl = pl.ds(r0, rows)
        m0, m1, m2 = m_s[0, sl, :], m_s[1, sl, :], m_s[2, sl, :]
        mm = jnp.maximum(jnp.maximum(m0, m1), m2)
        w0, w1, w2 = jnp.exp(m0 - mm), jnp.exp(m1 - mm), jnp.exp(m2 - mm)
        num = w0 * acc_s[0, sl, :] + w1 * acc_s[1, sl, :] + w2 * acc_s[2, sl, :]
        den = w0 * s_s[0, sl, :] + w1 * s_s[1, sl, :] + w2 * s_s[2, sl, :]
        att = num / den
        o_ref[sl, :] = (att * _silu(z_ref[sl, :])).astype(o_ref.dtype)
        return carry

    lax.fori_loop(0, SEQ // rows, merge, 0)


def _attn_prompt(proj3, bias_p):
    def head_spec(col0):
        return pl.BlockSpec((None, SEQ, HEAD_DIM),
                            lambda b, h, c=col0 // HEAD_DIM: (b, 0, c + h))

    in_specs = []
    for g in range(N_GROUPS):
        for col in (COL_Q, COL_K, COL_V):
            in_specs.append(head_spec(col + g * ATT_WIDTH))
    in_specs.append(head_spec(COL_ZATT))
    in_specs.append(pl.BlockSpec((N_GROUPS, None, BLK, 2 * BLK), lambda b, h: (0, h, 0, 0)))
    scratch = [pltpu.VMEM((N_GROUPS, SEQ, HEAD_DIM), F32) for _ in range(3)]
    return pl.pallas_call(
        _attn_prompt_kernel,
        grid=(BATCH, H_PER_GROUP),
        in_specs=in_specs,
        out_specs=pl.BlockSpec((None, SEQ, HEAD_DIM), lambda b, h: (b, 0, h)),
        out_shape=jax.ShapeDtypeStruct((BATCH, SEQ, ATT_WIDTH), BF16),
        scratch_shapes=scratch,
        compiler_params=_cparams(2),
        name="attn_prompt",
    )(*([proj3] * 10), bias_p)


def _lookup(tbl, idx):
    idx = np.asarray(idx, np.int32)
    onehot = (jnp.asarray(idx.reshape(-1, 1)) == jnp.arange(tbl.shape[0], dtype=jnp.int32)[None, :])
    out = jnp.dot(onehot.astype(F32), tbl.astype(F32), precision=lax.Precision.HIGHEST)
    return out.reshape(idx.shape + tbl.shape[1:])


def _prompt_bias(rel_bias):
    qi = np.arange(BLK)[:, None]
    ki = np.arange(2 * BLK)[None, :]
    du = qi + BLK - ki
    band = (du >= 0) & (du <= BLK)
    out = []
    for g, (win, dil) in enumerate(ATT_GROUPS):
        assert win // dil == BLK
        tbl = rel_bias[:, g * H_PER_GROUP:(g + 1) * H_PER_GROUP]
        b = _lookup(tbl, _t5_bucket(np.clip(du, 0, None) * dil))
        b = jnp.where(band[:, :, None], b, NEG)
        out.append(jnp.transpose(b, (2, 0, 1)))
    return jnp.stack(out)


def _softplus(x):
    return jnp.maximum(x, 0.0) + jnp.log1p(jnp.exp(-jnp.abs(x)))


def _lru_gates(xc, wr_ref, br, wi_ref, bi, nsp, n_blocks):
    a_parts, u_parts = [], []
    for n in range(n_blocks):
        sl = slice(n * LRU_BLOCK, (n + 1) * LRU_BLOCK)
        xn = xc[:, sl]
        xb = xn.astype(BF16)
        r = _sigmoid(jnp.dot(xb, wr_ref[n].astype(BF16), preferred_element_type=F32) + br[:, sl])
        i = _sigmoid(jnp.dot(xb, wi_ref[n].astype(BF16), preferred_element_type=F32) + bi[:, sl])
        log_a = (-LRU_C * r) * nsp[:, sl]
        a = jnp.exp(log_a)
        one_minus_a2 = -jnp.tanh(log_a) * (a * a + 1.0)
        a_parts.append(a)
        u_parts.append(jnp.sqrt(one_minus_a2) * (i * xn))
    return a_parts, u_parts


def _scan8(a, u, rows):
    for s in (1, 2, 4):
        keep = rows >= s
        a_sh = pltpu.roll(a, s, 0)
        u_sh = pltpu.roll(u, s, 0)
        u = jnp.where(keep, a * u_sh + u, u)
        a = jnp.where(keep, a * a_sh, a)
    return a, u


def _lru_prompt_kernel(x_ref, z_ref, cw_ref, cb_ref, wr_ref, br_ref, wi_ref, bi_ref, lam_ref,
                       o_ref, conv_ref, hl_ref, xpad, a_s, u_s):
    T, C = x_ref.shape
    n_blocks = C // LRU_BLOCK
    R = 256
    pad = 8

    xpad[0:pad, :] = jnp.zeros((pad, C), F32)

    def copy(c, carry):
        r0 = pl.multiple_of(c * R, R)
        xpad[pl.ds(r0 + pad, R), :] = x_ref[pl.ds(r0, R), :]
        return carry

    lax.fori_loop(0, T // R, copy, 0)
    conv_ref[...] = x_ref[T - (CONV_WIDTH - 1):T, :]

    nsp = _softplus(-lam_ref[...])
    br = br_ref[...]
    bi = bi_ref[...]
    cb = cb_ref[...]

    def gates(c, carry):
        r0 = pl.multiple_of(c * R, R)
        xw = xpad[pl.ds(r0, R + pad), :]
        xc = xw[pad:] * cw_ref[CONV_WIDTH - 1:CONV_WIDTH, :]
        for tap in range(CONV_WIDTH - 1):
            shifted = pltpu.roll(xw, CONV_WIDTH - 1 - tap, 0)[pad:]
            xc = xc + shifted * cw_ref[tap:tap + 1, :]
        xc = xc + cb
        a_parts, u_parts = _lru_gates(xc, wr_ref, br, wi_ref, bi, nsp, n_blocks)
        for n in range(n_blocks):
            sl = slice(n * LRU_BLOCK, (n + 1) * LRU_BLOCK)
            a_s[pl.ds(r0, R), sl] = a_parts[n]
            u_s[pl.ds(r0, R), sl] = u_parts[n]
        return carry

    lax.fori_loop(0, T // R, gates, 0)

    rows = lax.broadcasted_iota(jnp.int32, (8, C), 0)

    def scan(c, h_prev):
        t0 = pl.multiple_of(c * 16, 16)
        hs = []
        for half in range(2):
            sl = pl.ds(t0 + 8 * half, 8)
            a, u = _scan8(a_s[sl, :], u_s[sl, :], rows)
            h = a * h_prev + u
            hs.append(h)
            h_prev = h[7:8, :]
        h16 = jnp.concatenate(hs, axis=0)
        z = z_ref[pl.ds(t0, 16), :]
        o_ref[pl.ds(t0, 16), :] = (h16 * _silu(z)).astype(o_ref.dtype)
        return h_prev

    h_last = lax.fori_loop(0, T // 16, scan, jnp.zeros((1, C), F32))
    hl_ref[...] = h_last


def _lru_prompt(proj3, conv_w, conv_b, w_r, b_r, w_i, b_i, lam, layer):
    tc = 512
    nb = tc // LRU_BLOCK
    n_ct = LRU_WIDTH // tc

    def col_spec(col0):
        return pl.BlockSpec((None, SEQ, tc), lambda b, c, c0=col0 // tc: (b, 0, c0 + c))

    vec_spec = pl.BlockSpec((None, 1, tc), lambda b, c: (layer, 0, c))
    w_spec = pl.BlockSpec((None, nb, LRU_BLOCK, LRU_BLOCK), lambda b, c: (layer, c, 0, 0))
    return pl.pallas_call(
        _lru_prompt_kernel,
        grid=(BATCH, n_ct),
        in_specs=[
            col_spec(COL_XLRU), col_spec(COL_ZLRU),
            pl.BlockSpec((None, CONV_WIDTH, tc), lambda b, c: (layer, 0, c)),
            vec_spec, w_spec, vec_spec, w_spec, vec_spec, vec_spec,
        ],
        out_specs=[
            pl.BlockSpec((None, SEQ, tc), lambda b, c: (b, 0, c)),
            pl.BlockSpec((None, CONV_WIDTH - 1, tc), lambda b, c: (b, 0, c)),
            pl.BlockSpec((None, 1, tc), lambda b, c: (b, 0, c)),
        ],
        out_shape=[
            jax.ShapeDtypeStruct((BATCH, SEQ, LRU_WIDTH), BF16),
            jax.ShapeDtypeStruct((BATCH, CONV_WIDTH - 1, LRU_WIDTH), F32),
            jax.ShapeDtypeStruct((BATCH, 1, LRU_WIDTH), F32),
        ],
        scratch_shapes=[
            pltpu.VMEM((SEQ + 8, tc), F32),
            pltpu.VMEM((SEQ, tc), F32),
            pltpu.VMEM((SEQ, tc), F32),
        ],
        compiler_params=_cparams(2),
        name="lru_prompt",
    )(proj3, proj3, conv_w, conv_b, w_r, b_r, w_i, b_i, lam)


def _mix_kernel(att_ref, lru_ref, ga_ref, gl_ref, wpa_ref, wpb_ref, o_ref):
    y_att = jnp.dot(att_ref[...], wpa_ref[...], preferred_element_type=F32)
    y_lru = jnp.dot(lru_ref[...], wpb_ref[...], preferred_element_type=F32)
    merged = _sigmoid(ga_ref[...]) * y_att + _sigmoid(gl_ref[...]) * y_lru
    o_ref[...] = merged.astype(o_ref.dtype)


def _mix(att, lru, proj, wpa, wpb, layer, tm):
    m_rows = att.shape[0]
    return pl.pallas_call(
        _mix_kernel,
        grid=(m_rows // tm,),
        in_specs=[
            pl.BlockSpec((tm, ATT_WIDTH), lambda i: (i, 0)),
            pl.BlockSpec((tm, LRU_WIDTH), lambda i: (i, 0)),
            pl.BlockSpec((tm, D_MODEL), lambda i: (i, COL_GATT // D_MODEL)),
            pl.BlockSpec((tm, D_MODEL), lambda i: (i, COL_GLRU // D_MODEL)),
            pl.BlockSpec((None, ATT_WIDTH, D_MODEL), lambda i: (layer, 0, 0),
                         pipeline_mode=pl.Buffered(1)),
            pl.BlockSpec((None, LRU_WIDTH, D_MODEL), lambda i: (layer, 0, 0),
                         pipeline_mode=pl.Buffered(1)),
        ],
        out_specs=pl.BlockSpec((tm, D_MODEL), lambda i: (i, 0)),
        out_shape=jax.ShapeDtypeStruct((m_rows, D_MODEL), BF16),
        compiler_params=_cparams(1),
        name="branch_mix",
    )(att, lru, proj, proj, wpa, wpb)


def _residual_kernel(m_ref, w_ref, x_ref, gate_ref, g_ref, *rest, last):
    out = jnp.dot(m_ref[...], w_ref[...], preferred_element_type=F32)
    x_new = x_ref[...] + gate_ref[...] * out
    y = x_new * lax.rsqrt(jnp.mean(x_new * x_new, axis=-1, keepdims=True) + RMS_EPS)
    y = y * g_ref[...]
    if last:
        (y_ref,) = rest
        y_ref[...] = y
    else:
        sc_ref, sh_ref, xo_ref, ho_ref = rest
        xo_ref[...] = x_new
        ho_ref[...] = (y * (1.0 + sc_ref[...]) + sh_ref[...]).astype(ho_ref.dtype)


def _residual_prompt(merged, w_out, x, gate, g, scale, shift, layer):
    tm = 512
    last = scale is None
    row_spec = pl.BlockSpec((None, tm, D_MODEL), lambda b, i: (b, i, 0))
    vec_spec = pl.BlockSpec((None, 1, D_MODEL), lambda b, i: (b, 0, 0))
    in_specs = [
        row_spec,
        pl.BlockSpec((None, D_MODEL, D_MODEL), lambda b, i: (layer, 0, 0),
                     pipeline_mode=pl.Buffered(1)),
        row_spec,
        vec_spec,
        pl.BlockSpec((1, D_MODEL), lambda b, i: (0, 0)),
    ]
    args = [merged, w_out, x, gate, g.reshape(1, D_MODEL)]
    if last:
        out_specs = row_spec
        out_shape = jax.ShapeDtypeStruct(x.shape, F32)
    else:
        in_specs += [vec_spec, vec_spec]
        args += [scale, shift]
        out_specs = [row_spec, row_spec]
        out_shape = [jax.ShapeDtypeStruct(x.shape, F32), jax.ShapeDtypeStruct(x.shape, BF16)]
    return pl.pallas_call(
        functools.partial(_residual_kernel, last=last),
        grid=(BATCH, SEQ // tm),
        in_specs=in_specs,
        out_specs=out_specs,
        out_shape=out_shape,
        compiler_params=_cparams(2),
        name="residual_prompt",
    )(*args)


def _residual_sample(merged, w_out, x, gate_rows, g, scale_rows, shift_rows, layer):
    last = scale_rows is None
    full = pl.BlockSpec((N_SAMPLE_ROWS, D_MODEL), lambda i: (0, 0))
    in_specs = [full, pl.BlockSpec((None, D_MODEL, D_MODEL), lambda i: (layer, 0, 0)), full, full,
                pl.BlockSpec((1, D_MODEL), lambda i: (0, 0))]
    args = [merged, w_out, x, gate_rows, g.reshape(1, D_MODEL)]
    if last:
        out_specs = full
        out_shape = jax.ShapeDtypeStruct(x.shape, F32)
    else:
        in_specs += [full, full]
        args += [scale_rows, shift_rows]
        out_specs = [full, full]
        out_shape = [jax.ShapeDtypeStruct(x.shape, F32), jax.ShapeDtypeStruct(x.shape, BF16)]
    return pl.pallas_call(
        functools.partial(_residual_kernel, last=last),
        grid=(1,),
        in_specs=in_specs,
        out_specs=out_specs,
        out_shape=out_shape,
        compiler_params=_cparams(1),
        name="residual_sample",
    )(*args)


CACHE_CHUNK_ROWS = 512
CACHE_SLOTS = 6
CACHE_LOOKAHEAD = 3


def _cache_update_kernel(c0, c1, c2, n0, n1, n2, o0, o1, o2, buf, in_sem, out_sem, new_sem):
    chunks = []
    new_copies = []
    for gi, (c, n, o) in enumerate(((c0, n0, o0), (c1, n1, o1), (c2, n2, o2))):
        keep = c.shape[2] - DEC_SEQ
        for l in range(DEPTH):
            new_copies.append(pltpu.make_async_copy(
                n.at[l], o.at[l, :, pl.ds(keep, DEC_SEQ)], new_sem.at[gi, l]))
            for b in range(DEC_BATCH):
                for r in range(0, keep, CACHE_CHUNK_ROWS):
                    rows = min(CACHE_CHUNK_ROWS, keep - r)
                    chunks.append((c.at[l, b, pl.ds(DEC_SEQ + r, rows)],
                                   o.at[l, b, pl.ds(r, rows)], rows))

    def read(i):
        src, _, rows = chunks[i]
        slot = i % CACHE_SLOTS
        return pltpu.make_async_copy(src, buf.at[slot, pl.ds(0, rows)], in_sem.at[slot])

    def write(i):
        _, dst, rows = chunks[i]
        slot = i % CACHE_SLOTS
        return pltpu.make_async_copy(buf.at[slot, pl.ds(0, rows)], dst, out_sem.at[slot])

    for cp in new_copies:
        cp.start()
    n_chunks = len(chunks)
    for i in range(n_chunks + CACHE_LOOKAHEAD):
        if i < n_chunks:
            if i >= CACHE_SLOTS:
                write(i - CACHE_SLOTS).wait()
            read(i).start()
        j = i - CACHE_LOOKAHEAD
        if j >= 0:
            read(j).wait()
            write(j).start()
    for j in range(max(n_chunks - CACHE_SLOTS, 0), n_chunks):
        write(j).wait()
    for cp in new_copies:
        cp.wait()


def _cache_update(caches, new_rows):
    any_spec = pl.BlockSpec(memory_space=pl.ANY)
    vmem_spec = pl.BlockSpec(memory_space=pltpu.VMEM)
    return pl.pallas_call(
        _cache_update_kernel,
        in_specs=[any_spec] * 3 + [vmem_spec] * 3,
        out_specs=[any_spec] * 3,
        out_shape=[jax.ShapeDtypeStruct(c.shape, c.dtype) for c in caches],
        scratch_shapes=[
            pltpu.VMEM((CACHE_SLOTS, CACHE_CHUNK_ROWS, 2, H_PER_GROUP, HEAD_DIM), F32),
            pltpu.SemaphoreType.DMA((CACHE_SLOTS,)),
            pltpu.SemaphoreType.DMA((CACHE_SLOTS,)),
            pltpu.SemaphoreType.DMA((N_GROUPS, DEPTH)),
        ],
        compiler_params=pltpu.CompilerParams(vmem_limit_bytes=VMEM_LIMIT),
        name="cache_update",
    )(*caches, *new_rows)


def _attn_sample_kernel(q_ref, k_ref, v_ref, z_ref, c0_ref, c1_ref, c2_ref, bc_ref, bn_ref,
                        o_ref, kv0_ref, kv1_ref, kv2_ref):
    hp = H_PER_GROUP
    kv_refs = (kv0_ref, kv1_ref, kv2_ref)
    k_new, v_new = [], []
    for g in range(N_GROUPS):
        kn = k_ref[:, g * hp:(g + 1) * hp, :]
        vn = v_ref[:, g * hp:(g + 1) * hp, :]
        kv_refs[g][:, 0] = kn
        kv_refs[g][:, 1] = vn
        k_new.append(kn)
        v_new.append(vn)

    for s in range(DEC_SEQ):
        accs, ms, ss = [], [], []
        for g in range(N_GROUPS):
            qg = q_ref[s, g * hp:(g + 1) * hp, :]
            if g == 0:
                kc, vc = c0_ref[:, 0], c0_ref[:, 1]
            elif g == 1:
                kc, vc = c1_ref[:, s, 0], c1_ref[:, s, 1]
            else:
                kc, vc = c2_ref[:, s, 0], c2_ref[:, s, 1]
            lc = jnp.sum(kc * qg[None], axis=-1, keepdims=True) * ATT_SCALE + bc_ref[g, s]
            ln = jnp.sum(k_new[g] * qg[None], axis=-1, keepdims=True) * ATT_SCALE + bn_ref[g, s]
            m = jnp.maximum(jnp.max(lc, axis=0), jnp.max(ln, axis=0))
            pc = jnp.exp(lc - m[None])
            pn = jnp.exp(ln - m[None])
            ss.append(jnp.sum(pc, axis=0) + jnp.sum(pn, axis=0))
            accs.append(jnp.sum(pc * vc, axis=0) + jnp.sum(pn * v_new[g], axis=0))
            ms.append(m)
        mm = jnp.maximum(jnp.maximum(ms[0], ms[1]), ms[2])
        ws = [jnp.exp(m - mm) for m in ms]
        num = ws[0] * accs[0] + ws[1] * accs[1] + ws[2] * accs[2]
        den = ws[0] * ss[0] + ws[1] * ss[1] + ws[2] * ss[2]
        o_ref[s] = (num / den) * _silu(z_ref[s])


def _attn_sample(proj_s4, caches, bias_c, bias_n, layer):
    c0, c1, c2 = caches
    hp = H_PER_GROUP

    def head_spec(col0, n_heads):
        return pl.BlockSpec((None, DEC_SEQ, n_heads, HEAD_DIM),
                            lambda b, c=col0 // (HEAD_DIM * n_heads): (b, 0, c, 0))

    in_specs = [
        head_spec(COL_Q, N_ATT_HEADS), head_spec(COL_K, N_ATT_HEADS), head_spec(COL_V, N_ATT_HEADS),
        head_spec(COL_ZATT, hp),
        pl.BlockSpec((None, None, BLK, 2, hp, HEAD_DIM), lambda b: (layer, b, 0, 0, 0, 0)),
        pl.BlockSpec((None, None, BLK, DEC_SEQ, 2, hp, HEAD_DIM), lambda b: (layer, b, 0, 0, 0, 0, 0)),
        pl.BlockSpec((None, None, BLK, DEC_SEQ, 2, hp, HEAD_DIM), lambda b: (layer, b, 0, 0, 0, 0, 0)),
        pl.BlockSpec(bias_c.shape, lambda b: (0,) * 5),
        pl.BlockSpec(bias_n.shape, lambda b: (0,) * 5),
    ]
    out_specs = [pl.BlockSpec((None, DEC_SEQ, hp, HEAD_DIM), lambda b: (b, 0, 0, 0))]
    out_shape = [jax.ShapeDtypeStruct((DEC_BATCH, DEC_SEQ, hp, HEAD_DIM), F32)]
    for _ in range(N_GROUPS):
        out_specs.append(pl.BlockSpec((None, DEC_SEQ, 2, hp, HEAD_DIM), lambda b: (b, 0, 0, 0, 0)))
        out_shape.append(jax.ShapeDtypeStruct((DEC_BATCH, DEC_SEQ, 2, hp, HEAD_DIM), F32))
    return pl.pallas_call(
        _attn_sample_kernel,
        grid=(DEC_BATCH,),
        in_specs=in_specs,
        out_specs=out_specs,
        out_shape=out_shape,
        compiler_params=_cparams(1),
        name="attn_sample",
    )(proj_s4, proj_s4, proj_s4, proj_s4, c0, c1, c2, bias_c, bias_n)


def _sample_bias(rel_bias):
    s_idx = np.arange(DEC_SEQ)
    bc, bn = [], []
    for g, (win, dil) in enumerate(ATT_GROUPS):
        tbl = rel_bias[:, g * H_PER_GROUP:(g + 1) * H_PER_GROUP].astype(F32)
        key = np.arange(BLK)
        if dil == 1:
            j = BLK + s_idx[:, None] - key[None, :]
            valid = j <= BLK
        else:
            j = np.broadcast_to(BLK - key[None, :], (DEC_SEQ, BLK))
            valid = np.ones_like(j, dtype=bool)
        b = _lookup(tbl, _t5_bucket(np.clip(j, 0, None) * dil))
        bc.append(jnp.where(valid[:, :, None], b, NEG))
        jn = s_idx[:, None] - s_idx[None, :]
        valid_n = (jn >= 0) & (jn * dil <= win) & ((jn == 0) | (dil == 1))
        b = _lookup(tbl, _t5_bucket(np.clip(jn, 0, None) * dil))
        bn.append(jnp.where(valid_n[:, :, None], b, NEG))
    bc = jnp.stack(bc)
    bn = jnp.stack(bn)
    bc = jnp.broadcast_to(bc[..., None], bc.shape + (HEAD_DIM,))
    bn = jnp.broadcast_to(bn[..., None], bn.shape + (HEAD_DIM,))
    return bc, bn


def _lru_sample_kernel(x_ref, z_ref, cs_ref, h0_ref, cw_ref, cb_ref, wr_ref, br_ref, wi_ref, bi_ref,
                       lam_ref, o_ref, conv_ref, hl_ref):
    S = DEC_SEQ
    xp = [cs_ref[t] for t in range(CONV_WIDTH - 1)] + [x_ref[t] for t in range(S)]
    xc = []
    for t in range(S):
        y = xp[t] * cw_ref[0:1, :]
        for tap in range(1, CONV_WIDTH):
            y = y + xp[t + tap] * cw_ref[tap:tap + 1, :]
        xc.append(y + cb_ref[...])
    xcat = jnp.concatenate(xc, axis=0)
    nsp = _softplus(-lam_ref[...])
    a_parts, u_parts = _lru_gates(xcat, wr_ref, br_ref[...], wi_ref, bi_ref[...], nsp, LRU_BLOCKS)
    a = jnp.concatenate(a_parts, axis=1)
    u = jnp.concatenate(u_parts, axis=1)
    nb = DEC_BATCH
    h = h0_ref[...]
    for t in range(S):
        h = a[t * nb:(t + 1) * nb] * h + u[t * nb:(t + 1) * nb]
        o_ref[t] = h * _silu(z_ref[t])
    for t in range(CONV_WIDTH - 1):
        conv_ref[t] = xp[S + t]
    hl_ref[...] = h


def _lru_sample(x_t, z_t, conv_t, h0, conv_w, conv_b, w_r, b_r, w_i, b_i, lam, layer):
    C = LRU_WIDTH

    def full(shape):
        return pl.BlockSpec(shape, lambda i: (0,) * len(shape))

    vec_spec = pl.BlockSpec((None, 1, C), lambda i: (layer, 0, 0))
    w_spec = pl.BlockSpec((None, LRU_BLOCKS, LRU_BLOCK, LRU_BLOCK), lambda i: (layer, 0, 0, 0))
    return pl.pallas_call(
        _lru_sample_kernel,
        grid=(1,),
        in_specs=[
            full((DEC_SEQ, DEC_BATCH, C)), full((DEC_SEQ, DEC_BATCH, C)),
            full((CONV_WIDTH - 1, DEC_BATCH, C)), full((DEC_BATCH, C)),
            pl.BlockSpec((None, CONV_WIDTH, C), lambda i: (layer, 0, 0)),
            vec_spec, w_spec, vec_spec, w_spec, vec_spec, vec_spec,
        ],
        out_specs=[full((DEC_SEQ, DEC_BATCH, C)), full((CONV_WIDTH - 1, DEC_BATCH, C)),
                   full((DEC_BATCH, C))],
        out_shape=[
            jax.ShapeDtypeStruct((DEC_SEQ, DEC_BATCH, C), F32),
            jax.ShapeDtypeStruct((CONV_WIDTH - 1, DEC_BATCH, C), F32),
            jax.ShapeDtypeStruct((DEC_BATCH, C), F32),
        ],
        compiler_params=_cparams(1),
        name="lru_sample",
    )(x_t, z_t, conv_t, h0, conv_w, conv_b, w_r, b_r, w_i, b_i, lam)


def kernel(x_prompt, x_sample, c_prompt, c_sample, cache_kv_g0, cache_kv_g1, cache_kv_g2, state_conv, state_h, rel_bias, w_ada, b_ada, norm_g, w_in, conv_w, conv_b, w_r, b_r, w_i, b_i, lam, w_pa, w_pb, w_out, final_g):
    L, B, T, D = DEPTH, BATCH, SEQ, D_MODEL
    Bd, S = DEC_BATCH, DEC_SEQ

    c_all = jnp.concatenate(
        [c_prompt, c_sample, jnp.zeros((MOD_ROWS - B - Bd, D), F32)], axis=0)
    mod = _modulation(c_all, w_ada, b_ada).reshape(L, MOD_ROWS, 3, D)
    mod_p = mod[:, :B]
    mod_s = jnp.repeat(mod[:, B:B + Bd], S, axis=1)

    wpa_bf = w_pa.astype(BF16)
    wpb_bf = w_pb.astype(BF16)
    wout_bf = w_out.astype(BF16)
    conv_b3 = conv_b.reshape(L, 1, LRU_WIDTH)
    b_r3 = b_r.reshape(L, 1, LRU_WIDTH)
    b_i3 = b_i.reshape(L, 1, LRU_WIDTH)
    lam3 = lam.reshape(L, 1, LRU_WIDTH)

    bias_p = _prompt_bias(rel_bias)
    bias_c, bias_n = _sample_bias(rel_bias)

    cache_views = (
        cache_kv_g0,
        cache_kv_g1.reshape(L, Bd, BLK, 4, 2, H_PER_GROUP, HEAD_DIM),
        cache_kv_g2.reshape(L, Bd, BLK, 16, 2, H_PER_GROUP, HEAD_DIM),
    )

    xp = x_prompt
    xs = x_sample.reshape(Bd * S, D)
    kv_p = ([], [], [])
    kv_new = ([], [], [])
    conv_p, h_p, conv_s, h_s = [], [], [], []
    def mods(l):
        shift_p, scale_p, gate_p = (mod_p[l, :, i][:, None, :] for i in range(3))
        shift_s, scale_s, gate_s = (mod_s[l, :, i] for i in range(3))
        return (shift_p, scale_p, gate_p), (shift_s, scale_s, gate_s)

    (shift_p, scale_p, gate_p), (shift_s, scale_s, gate_s) = mods(0)
    hp_ = _norm_prompt(xp, norm_g[0], scale_p, shift_p, BF16)
    hs_ = _norm_sample(xs, norm_g[0], scale_s, shift_s, BF16)
    for l in range(L):
        proj_p, proj_s, *kv_l = _in_proj(hp_.reshape(B * T, D), hs_, w_in, l)
        proj3 = proj_p.reshape(B, T, IN_COLS)
        for g in range(N_GROUPS):
            kv_p[g].append(kv_l[g])

        att_p = _attn_prompt(proj3, bias_p)
        lru_p, cv, hl = _lru_prompt(proj3, conv_w, conv_b3, w_r, b_r3, w_i, b_i3, lam3, l)
        merged_p = _mix(att_p.reshape(B * T, ATT_WIDTH), lru_p.reshape(B * T, LRU_WIDTH),
                        proj_p, wpa_bf, wpb_bf, l, 512)
        last = l == L - 1
        if not last:
            (shift_pn, scale_pn, gate_pn), (shift_sn, scale_sn, gate_sn) = mods(l + 1)
            xp, hp_ = _residual_prompt(merged_p.reshape(B, T, D), wout_bf, xp, gate_p,
                                       norm_g[l + 1], scale_pn, shift_pn, l)
        else:
            y_prompt = _residual_prompt(merged_p.reshape(B, T, D), wout_bf, xp, gate_p,
                                        final_g, None, None, l)
        conv_p.append(cv)
        h_p.append(hl.reshape(B, LRU_WIDTH))

        proj_s4 = proj_s.reshape(Bd, S, IN_COLS // HEAD_DIM, HEAD_DIM)
        outs = _attn_sample(proj_s4, cache_views, bias_c, bias_n, l)
        att_s = outs[0]
        for g in range(N_GROUPS):
            kv_new[g].append(outs[1 + g])
        ps3 = proj_s.reshape(Bd, S, IN_COLS)
        x_t = jnp.transpose(ps3[:, :, COL_XLRU:COL_XLRU + LRU_WIDTH], (1, 0, 2))
        z_t = jnp.transpose(ps3[:, :, COL_ZLRU:COL_ZLRU + LRU_WIDTH], (1, 0, 2))
        conv_t = jnp.transpose(state_conv[l], (1, 0, 2))
        lru_t, cv_t, hl_s = _lru_sample(x_t, z_t, conv_t, state_h[l], conv_w, conv_b3,
                                        w_r, b_r3, w_i, b_i3, lam3, l)
        lru_s = jnp.transpose(lru_t, (1, 0, 2)).reshape(Bd * S, LRU_WIDTH).astype(BF16)
        merged_s = _mix(att_s.reshape(Bd * S, ATT_WIDTH).astype(BF16), lru_s, proj_s,
                        wpa_bf, wpb_bf, l, Bd * S)
        if not last:
            xs, hs_ = _residual_sample(merged_s, wout_bf, xs, gate_s,
                                       norm_g[l + 1], scale_sn, shift_sn, l)
            gate_p, gate_s = gate_pn, gate_sn
        else:
            y_sample = _residual_sample(merged_s, wout_bf, xs, gate_s,
                                        final_g, None, None, l).reshape(Bd, S, D)
        conv_s.append(jnp.transpose(cv_t, (1, 0, 2)))
        h_s.append(hl_s)

    kv_s = _cache_update((cache_kv_g0, cache_kv_g1, cache_kv_g2),
                         tuple(jnp.stack(rows) for rows in kv_new))
    return (y_prompt, y_sample,
            jnp.stack(kv_p[0]), jnp.stack(kv_p[1]), jnp.stack(kv_p[2]),
            jnp.stack(conv_p), jnp.stack(h_p),
            kv_s[0], kv_s[1], kv_s[2],
            jnp.stack(conv_s), jnp.stack(h_s))
```

```python
import functools
import math

import jax
import jax.numpy as jnp
import numpy as np
from jax import lax
from jax.experimental import pallas as pl
from jax.experimental.pallas import tpu as pltpu

D_MODEL = 2048
BATCH = 4
SEQ = 2048
DEPTH = 4
DEC_BATCH = 8
DEC_SEQ = 4
HEAD_DIM = 128
H_PER_GROUP = 8
ATT_GROUPS = ((128, 1), (512, 4), (2048, 16))
N_GROUPS = 3
N_ATT_HEADS = N_GROUPS * H_PER_GROUP
QKV_WIDTH = N_ATT_HEADS * HEAD_DIM
ATT_WIDTH = H_PER_GROUP * HEAD_DIM
N_BUCKETS = 32
REL_MAX_DIST = 2048
BLK = 128
LRU_WIDTH = D_MODEL
LRU_BLOCKS = 16
LRU_BLOCK = LRU_WIDTH // LRU_BLOCKS
CONV_WIDTH = 4
LRU_C = 8.0
IN_COLS = 3 * QKV_WIDTH + ATT_WIDTH + 2 * LRU_WIDTH + 2 * D_MODEL
RMS_EPS = 1e-6
NEG = -1e30
ATT_SCALE = HEAD_DIM ** -0.5

COL_Q = 0
COL_K = QKV_WIDTH
COL_V = 2 * QKV_WIDTH
COL_ZATT = 3 * QKV_WIDTH
COL_XLRU = COL_ZATT + ATT_WIDTH
COL_ZLRU = COL_XLRU + LRU_WIDTH
COL_GATT = COL_ZLRU + LRU_WIDTH
COL_GLRU = COL_GATT + D_MODEL

N_PROMPT_ROWS = BATCH * SEQ
N_SAMPLE_ROWS = DEC_BATCH * DEC_SEQ
MOD_ROWS = 16

VMEM_LIMIT = 52 * 1024 * 1024
ATT_ILP = 16

F32 = jnp.float32
BF16 = jnp.bfloat16


def _cparams(n_grid_dims, vmem_limit=VMEM_LIMIT):
    return pltpu.CompilerParams(
        dimension_semantics=("arbitrary",) * n_grid_dims,
        vmem_limit_bytes=vmem_limit)


def _sigmoid(x):
    return 0.5 * jnp.tanh(0.5 * x) + 0.5


def _silu(x):
    half = 0.5 * x
    return half * (jnp.tanh(half) + 1.0)


def _t5_bucket(dist):
    dist = np.asarray(dist).astype(np.int32)
    max_exact = N_BUCKETS // 2
    safe = np.maximum(dist, 1).astype(np.float32)
    large = max_exact + (np.log(safe / max_exact) / np.float32(math.log(REL_MAX_DIST / max_exact))
                         * (N_BUCKETS - max_exact)).astype(np.int32)
    large = np.minimum(large, N_BUCKETS - 1)
    return np.where(dist < max_exact, dist, large).astype(np.int32)


def _mod_kernel(c_ref, w_ref, b_ref, o_ref):
    c = _silu(c_ref[...])
    o_ref[...] = jnp.dot(c.astype(BF16), w_ref[...].astype(BF16),
                         preferred_element_type=F32) + b_ref[...]


def _modulation(c_all, w_ada, b_ada):
    tn = 1024
    n_cols = 3 * D_MODEL
    return pl.pallas_call(
        _mod_kernel,
        grid=(DEPTH, n_cols // tn),
        in_specs=[
            pl.BlockSpec((MOD_ROWS, D_MODEL), lambda l, j: (0, 0)),
            pl.BlockSpec((None, D_MODEL, tn), lambda l, j: (l, 0, j)),
            pl.BlockSpec((None, 1, tn), lambda l, j: (l, 0, j)),
        ],
        out_specs=pl.BlockSpec((None, MOD_ROWS, tn), lambda l, j: (l, 0, j)),
        out_shape=jax.ShapeDtypeStruct((DEPTH, MOD_ROWS, n_cols), F32),
        compiler_params=_cparams(2),
        name="adaln_mod",
    )(c_all, w_ada, b_ada.reshape(DEPTH, 1, n_cols))


def _norm_kernel(x_ref, g_ref, sc_ref, sh_ref, o_ref):
    x = x_ref[...]
    y = x * lax.rsqrt(jnp.mean(x * x, axis=-1, keepdims=True) + RMS_EPS)
    y = y * g_ref[...]
    o_ref[...] = (y * (1.0 + sc_ref[...]) + sh_ref[...]).astype(o_ref.dtype)


def _norm_prompt(x, g, scale, shift):
    tm = 512
    row_spec = pl.BlockSpec((None, tm, D_MODEL), lambda b, i: (b, i, 0))
    vec_spec = pl.BlockSpec((None, 1, D_MODEL), lambda b, i: (b, 0, 0))
    return pl.pallas_call(
        _norm_kernel,
        grid=(BATCH, SEQ // tm),
        in_specs=[row_spec, pl.BlockSpec((1, D_MODEL), lambda b, i: (0, 0)), vec_spec, vec_spec],
        out_specs=row_spec,
        out_shape=jax.ShapeDtypeStruct(x.shape, BF16),
        compiler_params=_cparams(2),
        name="rmsnorm_prompt",
    )(x, g.reshape(1, D_MODEL), scale, shift)


def _norm_sample(x, g, scale_rows, shift_rows):
    return pl.pallas_call(
        _norm_kernel,
        out_shape=jax.ShapeDtypeStruct(x.shape, BF16),
        name="rmsnorm_sample",
    )(x, g.reshape(1, D_MODEL), scale_rows, shift_rows)


IN_PROJ_VMEM_LIMIT = 58 * 1024 * 1024
IN_TM = 1024
IN_TN = ATT_WIDTH
IN_TILES_PER_SEQ = SEQ // IN_TM
K_TILE0 = COL_K // IN_TN
V_TILE0 = COL_V // IN_TN


def _kv_rows(g):
    w = min(ATT_GROUPS[g][0], SEQ)
    assert w % IN_TM == 0 or IN_TM % w == 0
    return min(w, IN_TM)


def _in_proj_kernel(h_ref, hs_ref, w_ref, kv0_in, kv1_in, kv2_in,
                    o_ref, os_ref, kv0_ref, kv1_ref, kv2_ref, wbf_ref):
    del kv0_in, kv1_in, kv2_in
    j = pl.program_id(0)
    i = pl.program_id(1)

    @pl.when(i == 0)
    def _():
        wbf_ref[...] = w_ref[...].astype(BF16)
        os_ref[...] = jnp.dot(hs_ref[...], wbf_ref[...], preferred_element_type=F32)

    o_ref[...] = jnp.dot(h_ref[...], wbf_ref[...], preferred_element_type=F32)

    for g, kv_ref in enumerate((kv0_ref, kv1_ref, kv2_ref)):
        rows = _kv_rows(g)
        is_kv = (j == K_TILE0 + g) | (j == V_TILE0 + g)
        if min(ATT_GROUPS[g][0], SEQ) < SEQ:
            is_kv = is_kv & (i % IN_TILES_PER_SEQ == IN_TILES_PER_SEQ - 1)

        @pl.when(is_kv)
        def _(kv_ref=kv_ref, rows=rows):
            flat = kv_ref.reshape(rows * H_PER_GROUP, HEAD_DIM)
            for h in range(H_PER_GROUP):
                flat[pl.ds(h, rows, stride=H_PER_GROUP), :] = (
                    o_ref[IN_TM - rows:, h * HEAD_DIM:(h + 1) * HEAD_DIM])


def _kv_index_map(g, layer):
    w = min(ATT_GROUPS[g][0], SEQ)
    n_i = N_PROMPT_ROWS // IN_TM
    full = w >= SEQ

    def index_map(j, i):
        k_col = K_TILE0 + g
        v_col = V_TILE0 + g
        c = jnp.where(j >= v_col, 1, 0)
        in_col = (j == k_col) | (j == v_col)
        if full:
            ii = jnp.where(in_col, i, jnp.where(j < k_col, 0, n_i - 1))
            return (layer, ii // IN_TILES_PER_SEQ, ii % IN_TILES_PER_SEQ, c, 0, 0)
        b_col = jnp.maximum(i - (IN_TILES_PER_SEQ - 1), 0) // IN_TILES_PER_SEQ
        b = jnp.where(in_col, b_col, jnp.where(j < k_col, 0, BATCH - 1))
        return (layer, b, 0, c, 0, 0)

    return index_map


def _in_proj(h_p, h_s, w_in, kv_stacks, layer):
    tm, tn = IN_TM, IN_TN
    out_specs = [
        pl.BlockSpec((tm, tn), lambda j, i: (i, j)),
        pl.BlockSpec((N_SAMPLE_ROWS, tn), lambda j, i: (0, j)),
    ]
    out_shape = [
        jax.ShapeDtypeStruct((N_PROMPT_ROWS, IN_COLS), F32),
        jax.ShapeDtypeStruct((N_SAMPLE_ROWS, IN_COLS), F32),
    ]
    for g, kv in enumerate(kv_stacks):
        out_specs.append(pl.BlockSpec((None, None, _kv_rows(g), None, H_PER_GROUP, HEAD_DIM),
                                      _kv_index_map(g, layer)))
        out_shape.append(jax.ShapeDtypeStruct(kv.shape, kv.dtype))
    n_in = 3
    return pl.pallas_call(
        _in_proj_kernel,
        grid=(IN_COLS // tn, N_PROMPT_ROWS // tm),
        in_specs=[
            pl.BlockSpec((tm, D_MODEL), lambda j, i: (i, 0)),
            pl.BlockSpec((N_SAMPLE_ROWS, D_MODEL), lambda j, i: (0, 0)),
            pl.BlockSpec((None, D_MODEL, tn), lambda j, i: (layer, 0, j)),
        ] + [pl.BlockSpec(memory_space=pl.ANY)] * N_GROUPS,
        out_specs=out_specs,
        out_shape=out_shape,
        input_output_aliases={n_in + g: 2 + g for g in range(N_GROUPS)},
        scratch_shapes=[pltpu.VMEM((D_MODEL, tn), BF16)],
        compiler_params=_cparams(2, IN_PROJ_VMEM_LIMIT),
        name="in_proj",
    )(h_p, h_s, w_in, *kv_stacks)


def _attn_prompt_kernel(q0, k0, v0, q1, k1, v1, q2, k2, v2, z_ref, bias_ref, o_ref,
                        acc_s, m_s, s_s):
    nt = (((1,), (1,)), ((), ()))

    def load(ref, base, d):
        if d == 1:
            return ref[pl.ds(base, BLK), :].astype(BF16)
        return ref[pl.ds(base, BLK, stride=d), :].astype(BF16)

    def store(g, base, d, acc, m, s):
        idx = pl.ds(base, BLK) if d == 1 else pl.ds(base, BLK, stride=d)
        acc_s[g, idx, :] = acc
        m_s[g, idx, :] = jnp.broadcast_to(m, (BLK, HEAD_DIM))
        s_s[g, idx, :] = jnp.broadcast_to(s, (BLK, HEAD_DIM))

    def block(g, refs, d, base, pbase, has_prev):
        q_ref, k_ref, v_ref = refs
        q = load(q_ref, base, d)
        lc = lax.dot_general(q, load(k_ref, base, d), nt, preferred_element_type=F32)
        lc = lc * ATT_SCALE + bias_ref[g, :, BLK:]
        if has_prev is None:
            return lc, None
        lp = lax.dot_general(q, load(k_ref, pbase, d), nt, preferred_element_type=F32)
        lp = lp * ATT_SCALE + bias_ref[g, :, :BLK]
        return lc, jnp.where(has_prev, lp, NEG)

    def softmax(lc, lp):
        if lp is None:
            m = jnp.max(lc, axis=-1, keepdims=True)
            pc = jnp.exp(lc - m)
            return pc, None, m, jnp.sum(pc, axis=-1, keepdims=True)
        m = jnp.max(jnp.maximum(lc, lp), axis=-1, keepdims=True)
        pc = jnp.exp(lc - m)
        pp = jnp.exp(lp - m)
        return pc, pp, m, jnp.sum(pc + pp, axis=-1, keepdims=True)

    def weighted(refs, d, base, pbase, pc, pp):
        v_ref = refs[2]
        acc = jnp.dot(pc.astype(BF16), load(v_ref, base, d), preferred_element_type=F32)
        if pp is not None:
            acc = acc + jnp.dot(pp.astype(BF16), load(v_ref, pbase, d), preferred_element_type=F32)
        return acc

    groups = ((q0, k0, v0), (q1, k1, v1), (q2, k2, v2))

    def run_blocks(g, d, specs):
        refs = groups[g]
        logits = [block(g, refs, d, base, pbase, has_prev) for base, pbase, has_prev in specs]
        probs = [softmax(lc, lp) for lc, lp in logits]
        accs = [weighted(refs, d, base, pbase, pc, pp)
                for (base, pbase, _), (pc, pp, _, _) in zip(specs, probs)]
        for (base, _, _), acc, (_, _, m, s) in zip(specs, accs, probs):
            store(g, base, d, acc, m, s)

    for g, (_, d) in enumerate(ATT_GROUPS):
        span = BLK * d
        n_blocks = SEQ // span
        if n_blocks == 1:
            def body_r(rr, carry, g=g, d=d):
                run_blocks(g, d, [(rr * ATT_ILP + u, None, None) for u in range(ATT_ILP)])
                return carry
            lax.fori_loop(0, d // ATT_ILP, body_r, 0)
        elif d == 1:
            def body_n(nn, carry, g=g, span=span):
                specs = []
                for u in range(ATT_ILP):
                    n = nn * ATT_ILP + u
                    specs.append((pl.multiple_of(n * span, BLK),
                                  pl.multiple_of(jnp.maximum(n - 1, 0) * span, BLK), n > 0))
                run_blocks(g, 1, specs)
                return carry
            lax.fori_loop(0, n_blocks // ATT_ILP, body_n, 0)
        else:
            n_per_iter = ATT_ILP // d
            assert n_per_iter * d == ATT_ILP and n_blocks % n_per_iter == 0
            def body_n(nn, carry, g=g, d=d, span=span, n_per_iter=n_per_iter):
                specs = []
                for u in range(n_per_iter):
                    n = nn * n_per_iter + u
                    specs += [(n * span + r, jnp.maximum(n - 1, 0) * span + r, n > 0)
                              for r in range(d)]
                run_blocks(g, d, specs)
                return carry
            lax.fori_loop(0, n_blocks // n_per_iter, body_n, 0)

    rows = 256

    def merge(c, carry):
        r0 = pl.multiple_of(c * rows, rows)
        sl = pl.ds(r0, rows)
        m0, m1, m2 = m_s[0, sl, :], m_s[1, sl, :], m_s[2, sl, :]
        mm = jnp.maximum(jnp.maximum(m0, m1), m2)
        w0, w1, w2 = jnp.exp(m0 - mm), jnp.exp(m1 - mm), jnp.exp(m2 - mm)
        num = w0 * acc_s[0, sl, :] + w1 * acc_s[1, sl, :] + w2 * acc_s[2, sl, :]
        den = w0 * s_s[0, sl, :] + w1 * s_s[1, sl, :] + w2 * s_s[2, sl, :]
        att = num / den
        o_ref[sl, :] = (att * _silu(z_ref[sl, :])).astype(o_ref.dtype)
        return carry

    lax.fori_loop(0, SEQ // rows, merge, 0)


def _attn_prompt(proj3, bias_p):
    def head_spec(col0):
        return pl.BlockSpec((None, SEQ, HEAD_DIM),
                            lambda b, h, c=col0 // HEAD_DIM: (b, 0, c + h))

    in_specs = []
    for g in range(N_GROUPS):
        for col in (COL_Q, COL_K, COL_V):
            in_specs.append(head_spec(col + g * ATT_WIDTH))
    in_specs.append(head_spec(COL_ZATT))
    in_specs.append(pl.BlockSpec((N_GROUPS, None, BLK, 2 * BLK), lambda b, h: (0, h, 0, 0)))
    scratch = [pltpu.VMEM((N_GROUPS, SEQ, HEAD_DIM), F32) for _ in range(3)]
    return pl.pallas_call(
        _attn_prompt_kernel,
        grid=(BATCH, H_PER_GROUP),
        in_specs=in_specs,
        out_specs=pl.BlockSpec((None, SEQ, HEAD_DIM), lambda b, h: (b, 0, h)),
        out_shape=jax.ShapeDtypeStruct((BATCH, SEQ, ATT_WIDTH), BF16),
        scratch_shapes=scratch,
        compiler_params=_cparams(2),
        name="attn_prompt",
    )(*([proj3] * 10), bias_p)


def _lookup(tbl, idx):
    idx = np.asarray(idx, np.int32)
    onehot = (jnp.asarray(idx.reshape(-1, 1)) == jnp.arange(tbl.shape[0], dtype=jnp.int32)[None, :])
    out = jnp.dot(onehot.astype(F32), tbl.astype(F32), precision=lax.Precision.HIGHEST)
    return out.reshape(idx.shape + tbl.shape[1:])


def _prompt_bias(rel_bias):
    qi = np.arange(BLK)[:, None]
    ki = np.arange(2 * BLK)[None, :]
    du = qi + BLK - ki
    band = (du >= 0) & (du <= BLK)
    out = []
    for g, (win, dil) in enumerate(ATT_GROUPS):
        assert win // dil == BLK
        tbl = rel_bias[:, g * H_PER_GROUP:(g + 1) * H_PER_GROUP]
        b = _lookup(tbl, _t5_bucket(np.clip(du, 0, None) * dil))
        b = jnp.where(band[:, :, None], b, NEG)
        out.append(jnp.transpose(b, (2, 0, 1)))
    return jnp.stack(out)


def _softplus(x):
    return jnp.maximum(x, 0.0) + jnp.log1p(jnp.exp(-jnp.abs(x)))


def _lru_gates(xc, wr_ref, br, wi_ref, bi, nsp, n_blocks):
    k = (-0.5 * LRU_C) * nsp
    half_br = 0.5 * br
    half_bi = 0.5 * bi
    a_parts, u_parts = [], []
    for n in range(n_blocks):
        sl = slice(n * LRU_BLOCK, (n + 1) * LRU_BLOCK)
        xn = xc[:, sl]
        xb = xn.astype(BF16)
        wr_half = (0.5 * wr_ref[n]).astype(BF16)
        wi_half = (0.5 * wi_ref[n]).astype(BF16)
        t_r = jnp.tanh(jnp.dot(xb, wr_half, preferred_element_type=F32) + half_br[:, sl])
        t_i = jnp.tanh(jnp.dot(xb, wi_half, preferred_element_type=F32) + half_bi[:, sl])
        log_a = k[:, sl] * t_r + k[:, sl]
        a = jnp.exp(log_a)
        one_minus_a2 = jnp.tanh(log_a) * (-1.0 - a * a)
        root = jnp.where(one_minus_a2 > 0.0, one_minus_a2 * lax.rsqrt(one_minus_a2), 0.0)
        a_parts.append(a)
        u_parts.append(root * ((0.5 * xn) * (t_i + 1.0)))
    return a_parts, u_parts


def _scan8(a, u, rows):
    for s in (1, 2, 4):
        keep = rows >= s
        a_sh = pltpu.roll(a, s, 0)
        u_sh = pltpu.roll(u, s, 0)
        u = jnp.where(keep, a * u_sh + u, u)
        a = jnp.where(keep, a * a_sh, a)
    return a, u


def _lru_prompt_kernel(x_ref, z_ref, cw_ref, cb_ref, wr_ref, br_ref, wi_ref, bi_ref, lam_ref,
                       o_ref, conv_ref, hl_ref, xpad, a_s, u_s):
    T, C = x_ref.shape
    n_blocks = C // LRU_BLOCK
    R = 256
    pad = 8

    xpad[0:pad, :] = jnp.zeros((pad, C), F32)

    def copy(c, carry):
        r0 = pl.multiple_of(c * R, R)
        xpad[pl.ds(r0 + pad, R), :] = x_ref[pl.ds(r0, R), :]
        return carry

    lax.fori_loop(0, T // R, copy, 0)
    conv_ref[...] = x_ref[T - (CONV_WIDTH - 1):T, :]

    nsp = _softplus(-lam_ref[...])
    br = br_ref[...]
    bi = bi_ref[...]
    cb = cb_ref[...]

    def gates(c, carry):
        r0 = pl.multiple_of(c * R, R)
        xw = xpad[pl.ds(r0, R + pad), :]
        xc = xw[pad:] * cw_ref[CONV_WIDTH - 1:CONV_WIDTH, :]
        for tap in range(CONV_WIDTH - 1):
            shifted = pltpu.roll(xw, CONV_WIDTH - 1 - tap, 0)[pad:]
            xc = xc + shifted * cw_ref[tap:tap + 1, :]
        xc = xc + cb
        a_parts, u_parts = _lru_gates(xc, wr_ref, br, wi_ref, bi, nsp, n_blocks)
        for n in range(n_blocks):
            sl = slice(n * LRU_BLOCK, (n + 1) * LRU_BLOCK)
            a_s[pl.ds(r0, R), sl] = a_parts[n]
            u_s[pl.ds(r0, R), sl] = u_parts[n]
        return carry

    lax.fori_loop(0, T // R, gates, 0)

    rows = lax.broadcasted_iota(jnp.int32, (8, C), 0)

    def scan(c, h_prev):
        t0 = pl.multiple_of(c * 16, 16)
        hs = []
        for half in range(2):
            sl = pl.ds(t0 + 8 * half, 8)
            a, u = _scan8(a_s[sl, :], u_s[sl, :], rows)
            h = a * h_prev + u
            hs.append(h)
            h_prev = h[7:8, :]
        h16 = jnp.concatenate(hs, axis=0)
        z = z_ref[pl.ds(t0, 16), :]
        o_ref[pl.ds(t0, 16), :] = (h16 * _silu(z)).astype(o_ref.dtype)
        return h_prev

    h_last = lax.fori_loop(0, T // 16, scan, jnp.zeros((1, C), F32))
    hl_ref[...] = h_last


def _lru_prompt(proj3, conv_w, conv_b, w_r, b_r, w_i, b_i, lam, layer):
    tc = 512
    nb = tc // LRU_BLOCK
    n_ct = LRU_WIDTH // tc

    def col_spec(col0):
        return pl.BlockSpec((None, SEQ, tc), lambda b, c, c0=col0 // tc: (b, 0, c0 + c))

    vec_spec = pl.BlockSpec((None, 1, tc), lambda b, c: (layer, 0, c))
    w_spec = pl.BlockSpec((None, nb, LRU_BLOCK, LRU_BLOCK), lambda b, c: (layer, c, 0, 0))
    return pl.pallas_call(
        _lru_prompt_kernel,
        grid=(BATCH, n_ct),
        in_specs=[
            col_spec(COL_XLRU), col_spec(COL_ZLRU),
            pl.BlockSpec((None, CONV_WIDTH, tc), lambda b, c: (layer, 0, c)),
            vec_spec, w_spec, vec_spec, w_spec, vec_spec, vec_spec,
        ],
        out_specs=[
            pl.BlockSpec((None, SEQ, tc), lambda b, c: (b, 0, c)),
            pl.BlockSpec((None, CONV_WIDTH - 1, tc), lambda b, c: (b, 0, c)),
            pl.BlockSpec((None, 1, tc), lambda b, c: (b, 0, c)),
        ],
        out_shape=[
            jax.ShapeDtypeStruct((BATCH, SEQ, LRU_WIDTH), BF16),
            jax.ShapeDtypeStruct((BATCH, CONV_WIDTH - 1, LRU_WIDTH), F32),
            jax.ShapeDtypeStruct((BATCH, 1, LRU_WIDTH), F32),
        ],
        scratch_shapes=[
            pltpu.VMEM((SEQ + 8, tc), F32),
            pltpu.VMEM((SEQ, tc), F32),
            pltpu.VMEM((SEQ, tc), F32),
        ],
        compiler_params=_cparams(2),
        name="lru_prompt",
    )(proj3, proj3, conv_w, conv_b, w_r, b_r, w_i, b_i, lam)


def _mix_kernel(att_ref, lru_ref, ga_ref, gl_ref, wpa_ref, wpb_ref, o_ref):
    y_att = jnp.dot(att_ref[...], wpa_ref[...], preferred_element_type=F32)
    y_lru = jnp.dot(lru_ref[...], wpb_ref[...], preferred_element_type=F32)
    merged = _sigmoid(ga_ref[...]) * y_att + _sigmoid(gl_ref[...]) * y_lru
    o_ref[...] = merged.astype(o_ref.dtype)


def _mix(att, lru, proj, wpa, wpb, layer, tm):
    m_rows = att.shape[0]
    return pl.pallas_call(
        _mix_kernel,
        grid=(m_rows // tm,),
        in_specs=[
            pl.BlockSpec((tm, ATT_WIDTH), lambda i: (i, 0)),
            pl.BlockSpec((tm, LRU_WIDTH), lambda i: (i, 0)),
            pl.BlockSpec((tm, D_MODEL), lambda i: (i, COL_GATT // D_MODEL)),
            pl.BlockSpec((tm, D_MODEL), lambda i: (i, COL_GLRU // D_MODEL)),
            pl.BlockSpec((None, ATT_WIDTH, D_MODEL), lambda i: (layer, 0, 0),
                         pipeline_mode=pl.Buffered(1)),
            pl.BlockSpec((None, LRU_WIDTH, D_MODEL), lambda i: (layer, 0, 0),
                         pipeline_mode=pl.Buffered(1)),
        ],
        out_specs=pl.BlockSpec((tm, D_MODEL), lambda i: (i, 0)),
        out_shape=jax.ShapeDtypeStruct((m_rows, D_MODEL), BF16),
        compiler_params=_cparams(1),
        name="branch_mix",
    )(att, lru, proj, proj, wpa, wpb)


def _residual_kernel(m_ref, w_ref, x_ref, gate_ref, g_ref, *rest, last):
    out = jnp.dot(m_ref[...], w_ref[...], preferred_element_type=F32)
    x_new = x_ref[...] + gate_ref[...] * out
    y = x_new * lax.rsqrt(jnp.mean(x_new * x_new, axis=-1, keepdims=True) + RMS_EPS)
    y = y * g_ref[...]
    if last:
        (y_ref,) = rest
        y_ref[...] = y
    else:
        sc_ref, sh_ref, xo_ref, ho_ref = rest
        xo_ref[...] = x_new
        ho_ref[...] = (y * (1.0 + sc_ref[...]) + sh_ref[...]).astype(ho_ref.dtype)


def _residual_prompt(merged, w_out, x, gate, g, scale, shift, layer):
    tm = 512
    last = scale is None
    row_spec = pl.BlockSpec((None, tm, D_MODEL), lambda b, i: (b, i, 0))
    vec_spec = pl.BlockSpec((None, 1, D_MODEL), lambda b, i: (b, 0, 0))
    in_specs = [
        row_spec,
        pl.BlockSpec((None, D_MODEL, D_MODEL), lambda b, i: (layer, 0, 0),
                     pipeline_mode=pl.Buffered(1)),
        row_spec,
        vec_spec,
        pl.BlockSpec((1, D_MODEL), lambda b, i: (0, 0)),
    ]
    args = [merged, w_out, x, gate, g.reshape(1, D_MODEL)]
    if last:
        out_specs = row_spec
        out_shape = jax.ShapeDtypeStruct(x.shape, F32)
    else:
        in_specs += [vec_spec, vec_spec]
        args += [scale, shift]
        out_specs = [row_spec, row_spec]
        out_shape = [jax.ShapeDtypeStruct(x.shape, F32), jax.ShapeDtypeStruct(x.shape, BF16)]
    return pl.pallas_call(
        functools.partial(_residual_kernel, last=last),
        grid=(BATCH, SEQ // tm),
        in_specs=in_specs,
        out_specs=out_specs,
        out_shape=out_shape,
        compiler_params=_cparams(2),
        name="residual_prompt",
    )(*args)


def _residual_sample(merged, w_out, x, gate_rows, g, scale_rows, shift_rows, layer):
    last = scale_rows is None
    full = pl.BlockSpec((N_SAMPLE_ROWS, D_MODEL), lambda i: (0, 0))
    in_specs = [full, pl.BlockSpec((None, D_MODEL, D_MODEL), lambda i: (layer, 0, 0)), full, full,
                pl.BlockSpec((1, D_MODEL), lambda i: (0, 0))]
    args = [merged, w_out, x, gate_rows, g.reshape(1, D_MODEL)]
    if last:
        out_specs = full
        out_shape = jax.ShapeDtypeStruct(x.shape, F32)
    else:
        in_specs += [full, full]
        args += [scale_rows, shift_rows]
        out_specs = [full, full]
        out_shape = [jax.ShapeDtypeStruct(x.shape, F32), jax.ShapeDtypeStruct(x.shape, BF16)]
    return pl.pallas_call(
        functools.partial(_residual_kernel, last=last),
        grid=(1,),
        in_specs=in_specs,
        out_specs=out_specs,
        out_shape=out_shape,
        compiler_params=_cparams(1),
        name="residual_sample",
    )(*args)


CACHE_CHUNK_ROWS = 512
CACHE_SLOTS = 6
CACHE_LOOKAHEAD = 3


def _cache_update_kernel(c0, c1, c2, n0, n1, n2, o0, o1, o2, buf, in_sem, out_sem, new_sem):
    chunks = []
    new_copies = []
    for gi, (c, n, o) in enumerate(((c0, n0, o0), (c1, n1, o1), (c2, n2, o2))):
        keep = c.shape[2] - DEC_SEQ
        for l in range(DEPTH):
            new_copies.append(pltpu.make_async_copy(
                n.at[l], o.at[l, :, pl.ds(keep, DEC_SEQ)], new_sem.at[gi, l]))
            for b in range(DEC_BATCH):
                for r in range(0, keep, CACHE_CHUNK_ROWS):
                    rows = min(CACHE_CHUNK_ROWS, keep - r)
                    chunks.append((c.at[l, b, pl.ds(DEC_SEQ + r, rows)],
                                   o.at[l, b, pl.ds(r, rows)], rows))

    def read(i):
        src, _, rows = chunks[i]
        slot = i % CACHE_SLOTS
        return pltpu.make_async_copy(src, buf.at[slot, pl.ds(0, rows)], in_sem.at[slot])

    def write(i):
        _, dst, rows = chunks[i]
        slot = i % CACHE_SLOTS
        return pltpu.make_async_copy(buf.at[slot, pl.ds(0, rows)], dst, out_sem.at[slot])

    for cp in new_copies:
        cp.start()
    n_chunks = len(chunks)
    for i in range(n_chunks + CACHE_LOOKAHEAD):
        if i < n_chunks:
            if i >= CACHE_SLOTS:
                write(i - CACHE_SLOTS).wait()
            read(i).start()
        j = i - CACHE_LOOKAHEAD
        if j >= 0:
            read(j).wait()
            write(j).start()
    for j in range(max(n_chunks - CACHE_SLOTS, 0), n_chunks):
        write(j).wait()
    for cp in new_copies:
        cp.wait()


def _cache_update(caches, new_rows):
    any_spec = pl.BlockSpec(memory_space=pl.ANY)
    vmem_spec = pl.BlockSpec(memory_space=pltpu.VMEM)
    return pl.pallas_call(
        _cache_update_kernel,
        in_specs=[any_spec] * 3 + [vmem_spec] * 3,
        out_specs=[any_spec] * 3,
        out_shape=[jax.ShapeDtypeStruct(c.shape, c.dtype) for c in caches],
        scratch_shapes=[
            pltpu.VMEM((CACHE_SLOTS, CACHE_CHUNK_ROWS, 2, H_PER_GROUP, HEAD_DIM), F32),
            pltpu.SemaphoreType.DMA((CACHE_SLOTS,)),
            pltpu.SemaphoreType.DMA((CACHE_SLOTS,)),
            pltpu.SemaphoreType.DMA((N_GROUPS, DEPTH)),
        ],
        compiler_params=pltpu.CompilerParams(vmem_limit_bytes=VMEM_LIMIT),
        name="cache_update",
    )(*caches, *new_rows)


def _attn_sample_kernel(q_ref, k_ref, v_ref, z_ref, c0_ref, c1_ref, c2_ref, bc_ref, bn_ref,
                        o_ref, kv0_ref, kv1_ref, kv2_ref):
    hp = H_PER_GROUP
    kv_refs = (kv0_ref, kv1_ref, kv2_ref)
    k_new, v_new = [], []
    for g in range(N_GROUPS):
        kn = k_ref[:, g * hp:(g + 1) * hp, :]
        vn = v_ref[:, g * hp:(g + 1) * hp, :]
        kv_refs[g][:, 0] = kn
        kv_refs[g][:, 1] = vn
        k_new.append(kn)
        v_new.append(vn)

    for s in range(DEC_SEQ):
        accs, ms, ss = [], [], []
        for g in range(N_GROUPS):
            qg = q_ref[s, g * hp:(g + 1) * hp, :]
            if g == 0:
                kc, vc = c0_ref[:, 0], c0_ref[:, 1]
            elif g == 1:
                kc, vc = c1_ref[:, s, 0], c1_ref[:, s, 1]
            else:
                kc, vc = c2_ref[:, s, 0], c2_ref[:, s, 1]
            lc = jnp.sum(kc * qg[None], axis=-1, keepdims=True) * ATT_SCALE + bc_ref[g, s]
            ln = jnp.sum(k_new[g] * qg[None], axis=-1, keepdims=True) * ATT_SCALE + bn_ref[g, s]
            m = jnp.maximum(jnp.max(lc, axis=0), jnp.max(ln, axis=0))
            pc = jnp.exp(lc - m[None])
            pn = jnp.exp(ln - m[None])
            ss.append(jnp.sum(pc, axis=0) + jnp.sum(pn, axis=0))
            accs.append(jnp.sum(pc * vc, axis=0) + jnp.sum(pn * v_new[g], axis=0))
            ms.append(m)
        mm = jnp.maximum(jnp.maximum(ms[0], ms[1]), ms[2])
        ws = [jnp.exp(m - mm) for m in ms]
        num = ws[0] * accs[0] + ws[1] * accs[1] + ws[2] * accs[2]
        den = ws[0] * ss[0] + ws[1] * ss[1] + ws[2] * ss[2]
        o_ref[s] = (num / den) * _silu(z_ref[s])


def _attn_sample(proj_s4, caches, bias_c, bias_n, layer):
    c0, c1, c2 = caches
    hp = H_PER_GROUP

    def head_spec(col0, n_heads):
        return pl.BlockSpec((None, DEC_SEQ, n_heads, HEAD_DIM),
                            lambda b, c=col0 // (HEAD_DIM * n_heads): (b, 0, c, 0))

    in_specs = [
        head_spec(COL_Q, N_ATT_HEADS), head_spec(COL_K, N_ATT_HEADS), head_spec(COL_V, N_ATT_HEADS),
        head_spec(COL_ZATT, hp),
        pl.BlockSpec((None, None, BLK, 2, hp, HEAD_DIM), lambda b: (layer, b, 0, 0, 0, 0)),
        pl.BlockSpec((None, None, BLK, DEC_SEQ, 2, hp, HEAD_DIM), lambda b: (layer, b, 0, 0, 0, 0, 0)),
        pl.BlockSpec((None, None, BLK, DEC_SEQ, 2, hp, HEAD_DIM), lambda b: (layer, b, 0, 0, 0, 0, 0)),
        pl.BlockSpec(bias_c.shape, lambda b: (0,) * 5),
        pl.BlockSpec(bias_n.shape, lambda b: (0,) * 5),
    ]
    out_specs = [pl.BlockSpec((None, DEC_SEQ, hp, HEAD_DIM), lambda b: (b, 0, 0, 0))]
    out_shape = [jax.ShapeDtypeStruct((DEC_BATCH, DEC_SEQ, hp, HEAD_DIM), F32)]
    for _ in range(N_GROUPS):
        out_specs.append(pl.BlockSpec((None, DEC_SEQ, 2, hp, HEAD_DIM), lambda b: (b, 0, 0, 0, 0)))
        out_shape.append(jax.ShapeDtypeStruct((DEC_BATCH, DEC_SEQ, 2, hp, HEAD_DIM), F32))
    return pl.pallas_call(
        _attn_sample_kernel,
        grid=(DEC_BATCH,),
        in_specs=in_specs,
        out_specs=out_specs,
        out_shape=out_shape,
        compiler_params=_cparams(1),
        name="attn_sample",
    )(proj_s4, proj_s4, proj_s4, proj_s4, c0, c1, c2, bias_c, bias_n)


def _sample_bias(rel_bias):
    s_idx = np.arange(DEC_SEQ)
    bc, bn = [], []
    for g, (win, dil) in enumerate(ATT_GROUPS):
        tbl = rel_bias[:, g * H_PER_GROUP:(g + 1) * H_PER_GROUP].astype(F32)
        key = np.arange(BLK)
        if dil == 1:
            j = BLK + s_idx[:, None] - key[None, :]
            valid = j <= BLK
        else:
            j = np.broadcast_to(BLK - key[None, :], (DEC_SEQ, BLK))
            valid = np.ones_like(j, dtype=bool)
        b = _lookup(tbl, _t5_bucket(np.clip(j, 0, None) * dil))
        bc.append(jnp.where(valid[:, :, None], b, NEG))
        jn = s_idx[:, None] - s_idx[None, :]
        valid_n = (jn >= 0) & (jn * dil <= win) & ((jn == 0) | (dil == 1))
        b = _lookup(tbl, _t5_bucket(np.clip(jn, 0, None) * dil))
        bn.append(jnp.where(valid_n[:, :, None], b, NEG))
    bc = jnp.stack(bc)
    bn = jnp.stack(bn)
    bc = jnp.broadcast_to(bc[..., None], bc.shape + (HEAD_DIM,))
    bn = jnp.broadcast_to(bn[..., None], bn.shape + (HEAD_DIM,))
    return bc, bn


def _lru_sample_kernel(x_ref, z_ref, cs_ref, h0_ref, cw_ref, cb_ref, wr_ref, br_ref, wi_ref, bi_ref,
                       lam_ref, o_ref, conv_ref, hl_ref):
    S = DEC_SEQ
    xp = [cs_ref[t] for t in range(CONV_WIDTH - 1)] + [x_ref[t] for t in range(S)]
    xc = []
    for t in range(S):
        y = xp[t] * cw_ref[0:1, :]
        for tap in range(1, CONV_WIDTH):
            y = y + xp[t + tap] * cw_ref[tap:tap + 1, :]
        xc.append(y + cb_ref[...])
    xcat = jnp.concatenate(xc, axis=0)
    nsp = _softplus(-lam_ref[...])
    a_parts, u_parts = _lru_gates(xcat, wr_ref, br_ref[...], wi_ref, bi_ref[...], nsp, LRU_BLOCKS)
    a = jnp.concatenate(a_parts, axis=1)
    u = jnp.concatenate(u_parts, axis=1)
    nb = DEC_BATCH
    h = h0_ref[...]
    for t in range(S):
        h = a[t * nb:(t + 1) * nb] * h + u[t * nb:(t + 1) * nb]
        o_ref[t] = h * _silu(z_ref[t])
    for t in range(CONV_WIDTH - 1):
        conv_ref[t] = xp[S + t]
    hl_ref[...] = h


def _lru_sample(x_t, z_t, conv_t, h0, conv_w, conv_b, w_r, b_r, w_i, b_i, lam, layer):
    C = LRU_WIDTH

    def full(shape):
        return pl.BlockSpec(shape, lambda i: (0,) * len(shape))

    vec_spec = pl.BlockSpec((None, 1, C), lambda i: (layer, 0, 0))
    w_spec = pl.BlockSpec((None, LRU_BLOCKS, LRU_BLOCK, LRU_BLOCK), lambda i: (layer, 0, 0, 0))
    return pl.pallas_call(
        _lru_sample_kernel,
        grid=(1,),
        in_specs=[
            full((DEC_SEQ, DEC_BATCH, C)), full((DEC_SEQ, DEC_BATCH, C)),
            full((CONV_WIDTH - 1, DEC_BATCH, C)), full((DEC_BATCH, C)),
            pl.BlockSpec((None, CONV_WIDTH, C), lambda i: (layer, 0, 0)),
            vec_spec, w_spec, vec_spec, w_spec, vec_spec, vec_spec,
        ],
        out_specs=[full((DEC_SEQ, DEC_BATCH, C)), full((CONV_WIDTH - 1, DEC_BATCH, C)),
                   full((DEC_BATCH, C))],
        out_shape=[
            jax.ShapeDtypeStruct((DEC_SEQ, DEC_BATCH, C), F32),
            jax.ShapeDtypeStruct((CONV_WIDTH - 1, DEC_BATCH, C), F32),
            jax.ShapeDtypeStruct((DEC_BATCH, C), F32),
        ],
        compiler_params=_cparams(1),
        name="lru_sample",
    )(x_t, z_t, conv_t, h0, conv_w, conv_b, w_r, b_r, w_i, b_i, lam)


def kernel(x_prompt, x_sample, c_prompt, c_sample, cache_kv_g0, cache_kv_g1, cache_kv_g2, state_conv, state_h, rel_bias, w_ada, b_ada, norm_g, w_in, conv_w, conv_b, w_r, b_r, w_i, b_i, lam, w_pa, w_pb, w_out, final_g):
    L, B, T, D = DEPTH, BATCH, SEQ, D_MODEL
    Bd, S = DEC_BATCH, DEC_SEQ

    c_all = jnp.concatenate(
        [c_prompt, c_sample, jnp.zeros((MOD_ROWS - B - Bd, D), F32)], axis=0)
    mod = _modulation(c_all, w_ada, b_ada).reshape(L, MOD_ROWS, 3, D)
    mod_p = mod[:, :B]
    mod_s = jnp.repeat(mod[:, B:B + Bd], S, axis=1)

    wpa_bf = w_pa.astype(BF16)
    wpb_bf = w_pb.astype(BF16)
    wout_bf = w_out.astype(BF16)
    conv_b3 = conv_b.reshape(L, 1, LRU_WIDTH)
    b_r3 = b_r.reshape(L, 1, LRU_WIDTH)
    b_i3 = b_i.reshape(L, 1, LRU_WIDTH)
    lam3 = lam.reshape(L, 1, LRU_WIDTH)

    bias_p = _prompt_bias(rel_bias)
    bias_c, bias_n = _sample_bias(rel_bias)

    cache_views = (
        cache_kv_g0,
        cache_kv_g1.reshape(L, Bd, BLK, 4, 2, H_PER_GROUP, HEAD_DIM),
        cache_kv_g2.reshape(L, Bd, BLK, 16, 2, H_PER_GROUP, HEAD_DIM),
    )

    xp = x_prompt
    xs = x_sample.reshape(Bd * S, D)
    kv_p = [jnp.zeros((L, B, min(win, T), 2, H_PER_GROUP, HEAD_DIM), F32) for win, _ in ATT_GROUPS]
    kv_new = ([], [], [])
    conv_p, h_p, conv_s, h_s = [], [], [], []
    def mods(l):
        shift_p, scale_p, gate_p = (mod_p[l, :, i][:, None, :] for i in range(3))
        shift_s, scale_s, gate_s = (mod_s[l, :, i] for i in range(3))
        return (shift_p, scale_p, gate_p), (shift_s, scale_s, gate_s)

    (shift_p, scale_p, gate_p), (shift_s, scale_s, gate_s) = mods(0)
    hp_ = _norm_prompt(xp, norm_g[0], scale_p, shift_p)
    hs_ = _norm_sample(xs, norm_g[0], scale_s, shift_s)
    for l in range(L):
        proj_p, proj_s, *kv_p = _in_proj(hp_.reshape(B * T, D), hs_, w_in, kv_p, l)
        proj3 = proj_p.reshape(B, T, IN_COLS)

        att_p = _attn_prompt(proj3, bias_p)
        lru_p, cv, hl = _lru_prompt(proj3, conv_w, conv_b3, w_r, b_r3, w_i, b_i3, lam3, l)
        merged_p = _mix(att_p.reshape(B * T, ATT_WIDTH), lru_p.reshape(B * T, LRU_WIDTH),
                        proj_p, wpa_bf, wpb_bf, l, 512)
        last = l == L - 1
        if not last:
            (shift_pn, scale_pn, gate_pn), (shift_sn, scale_sn, gate_sn) = mods(l + 1)
            xp, hp_ = _residual_prompt(merged_p.reshape(B, T, D), wout_bf, xp, gate_p,
                                       norm_g[l + 1], scale_pn, shift_pn, l)
        else:
            y_prompt = _residual_prompt(merged_p.reshape(B, T, D), wout_bf, xp, gate_p,
                                        final_g, None, None, l)
        conv_p.append(cv)
        h_p.append(hl.reshape(B, LRU_WIDTH))

        proj_s4 = proj_s.reshape(Bd, S, IN_COLS // HEAD_DIM, HEAD_DIM)
        outs = _attn_sample(proj_s4, cache_views, bias_c, bias_n, l)
        att_s = outs[0]
        for g in range(N_GROUPS):
            kv_new[g].append(outs[1 + g])
        ps3 = proj_s.reshape(Bd, S, IN_COLS)
        x_t = jnp.transpose(ps3[:, :, COL_XLRU:COL_XLRU + LRU_WIDTH], (1, 0, 2))
        z_t = jnp.transpose(ps3[:, :, COL_ZLRU:COL_ZLRU + LRU_WIDTH], (1, 0, 2))
        conv_t = jnp.transpose(state_conv[l], (1, 0, 2))
        lru_t, cv_t, hl_s = _lru_sample(x_t, z_t, conv_t, state_h[l], conv_w, conv_b3,
                                        w_r, b_r3, w_i, b_i3, lam3, l)
        lru_s = jnp.transpose(lru_t, (1, 0, 2)).reshape(Bd * S, LRU_WIDTH).astype(BF16)
        merged_s = _mix(att_s.reshape(Bd * S, ATT_WIDTH).astype(BF16), lru_s, proj_s,
                        wpa_bf, wpb_bf, l, Bd * S)
        if not last:
            xs, hs_ = _residual_sample(merged_s, wout_bf, xs, gate_s,
                                       norm_g[l + 1], scale_sn, shift_sn, l)
            gate_p, gate_s = gate_pn, gate_sn
        else:
            y_sample = _residual_sample(merged_s, wout_bf, xs, gate_s,
                                        final_g, None, None, l).reshape(Bd, S, D)
        conv_s.append(jnp.transpose(cv_t, (1, 0, 2)))
        h_s.append(hl_s)

    kv_s = _cache_update((cache_kv_g0, cache_kv_g1, cache_kv_g2),
                         tuple(jnp.stack(rows) for rows in kv_new))
    return (y_prompt, y_sample,
            kv_p[0], kv_p[1], kv_p[2],
            jnp.stack(conv_p), jnp.stack(h_p),
            kv_s[0], kv_s[1], kv_s[2],
            jnp.stack(conv_s), jnp.stack(h_s))
```

```python
import functools
import math

import jax
import jax.numpy as jnp
import numpy as np
from jax import lax
from jax.experimental import pallas as pl
from jax.experimental.pallas import tpu as pltpu

D_MODEL = 2048
BATCH = 4
SEQ = 2048
DEPTH = 4
DEC_BATCH = 8
DEC_SEQ = 4
HEAD_DIM = 128
H_PER_GROUP = 8
ATT_GROUPS = ((128, 1), (512, 4), (2048, 16))
N_GROUPS = 3
N_ATT_HEADS = N_GROUPS * H_PER_GROUP
QKV_WIDTH = N_ATT_HEADS * HEAD_DIM
ATT_WIDTH = H_PER_GROUP * HEAD_DIM
N_BUCKETS = 32
REL_MAX_DIST = 2048
BLK = 128
LRU_WIDTH = D_MODEL
LRU_BLOCKS = 16
LRU_BLOCK = LRU_WIDTH // LRU_BLOCKS
CONV_WIDTH = 4
LRU_C = 8.0
IN_COLS = 3 * QKV_WIDTH + ATT_WIDTH + 2 * LRU_WIDTH + 2 * D_MODEL
RMS_EPS = 1e-6
NEG = -1e30
ATT_SCALE = HEAD_DIM ** -0.5

COL_Q = 0
COL_K = QKV_WIDTH
COL_V = 2 * QKV_WIDTH
COL_ZATT = 3 * QKV_WIDTH
COL_XLRU = COL_ZATT + ATT_WIDTH
COL_ZLRU = COL_XLRU + LRU_WIDTH
COL_GATT = COL_ZLRU + LRU_WIDTH
COL_GLRU = COL_GATT + D_MODEL

N_PROMPT_ROWS = BATCH * SEQ
N_SAMPLE_ROWS = DEC_BATCH * DEC_SEQ
MOD_ROWS = 16

VMEM_LIMIT = 52 * 1024 * 1024
ATT_ILP = 16
ATT_ILP_MERGE = 16

F32 = jnp.float32
BF16 = jnp.bfloat16


def _cparams(n_grid_dims, vmem_limit=VMEM_LIMIT):
    return pltpu.CompilerParams(
        dimension_semantics=("arbitrary",) * n_grid_dims,
        vmem_limit_bytes=vmem_limit)


def _sigmoid(x):
    return 0.5 * jnp.tanh(0.5 * x) + 0.5


def _silu(x):
    half = 0.5 * x
    return half * (jnp.tanh(half) + 1.0)


def _t5_bucket(dist):
    dist = np.asarray(dist).astype(np.int32)
    max_exact = N_BUCKETS // 2
    safe = np.maximum(dist, 1).astype(np.float32)
    large = max_exact + (np.log(safe / max_exact) / np.float32(math.log(REL_MAX_DIST / max_exact))
                         * (N_BUCKETS - max_exact)).astype(np.int32)
    large = np.minimum(large, N_BUCKETS - 1)
    return np.where(dist < max_exact, dist, large).astype(np.int32)


def _mod_kernel(c_ref, w_ref, b_ref, o_ref):
    c = _silu(c_ref[...])
    o_ref[...] = jnp.dot(c.astype(BF16), w_ref[...].astype(BF16),
                         preferred_element_type=F32) + b_ref[...]


def _modulation(c_all, w_ada, b_ada):
    tn = 1024
    n_cols = 3 * D_MODEL
    return pl.pallas_call(
        _mod_kernel,
        grid=(DEPTH, n_cols // tn),
        in_specs=[
            pl.BlockSpec((MOD_ROWS, D_MODEL), lambda l, j: (0, 0)),
            pl.BlockSpec((None, D_MODEL, tn), lambda l, j: (l, 0, j)),
            pl.BlockSpec((None, 1, tn), lambda l, j: (l, 0, j)),
        ],
        out_specs=pl.BlockSpec((None, MOD_ROWS, tn), lambda l, j: (l, 0, j)),
        out_shape=jax.ShapeDtypeStruct((DEPTH, MOD_ROWS, n_cols), F32),
        compiler_params=_cparams(2),
        name="adaln_mod",
    )(c_all, w_ada, b_ada.reshape(DEPTH, 1, n_cols))


def _norm_kernel(x_ref, g_ref, sc_ref, sh_ref, o_ref):
    x = x_ref[...]
    y = x * lax.rsqrt(jnp.mean(x * x, axis=-1, keepdims=True) + RMS_EPS)
    y = y * g_ref[...]
    o_ref[...] = (y * (1.0 + sc_ref[...]) + sh_ref[...]).astype(o_ref.dtype)


def _norm_prompt(x, g, scale, shift):
    tm = 512
    row_spec = pl.BlockSpec((None, tm, D_MODEL), lambda b, i: (b, i, 0))
    vec_spec = pl.BlockSpec((None, 1, D_MODEL), lambda b, i: (b, 0, 0))
    return pl.pallas_call(
        _norm_kernel,
        grid=(BATCH, SEQ // tm),
        in_specs=[row_spec, pl.BlockSpec((1, D_MODEL), lambda b, i: (0, 0)), vec_spec, vec_spec],
        out_specs=row_spec,
        out_shape=jax.ShapeDtypeStruct(x.shape, BF16),
        compiler_params=_cparams(2),
        name="rmsnorm_prompt",
    )(x, g.reshape(1, D_MODEL), scale, shift)


def _norm_sample(x, g, scale_rows, shift_rows):
    return pl.pallas_call(
        _norm_kernel,
        out_shape=jax.ShapeDtypeStruct(x.shape, BF16),
        name="rmsnorm_sample",
    )(x, g.reshape(1, D_MODEL), scale_rows, shift_rows)


IN_PROJ_VMEM_LIMIT = 58 * 1024 * 1024
IN_TM = 1024
IN_TN = ATT_WIDTH
IN_TILES_PER_SEQ = SEQ // IN_TM
K_TILE0 = COL_K // IN_TN
V_TILE0 = COL_V // IN_TN


def _kv_rows(g):
    w = min(ATT_GROUPS[g][0], SEQ)
    assert w % IN_TM == 0 or IN_TM % w == 0
    return min(w, IN_TM)


def _in_proj_kernel(h_ref, hs_ref, w_ref, kv0_in, kv1_in, kv2_in,
                    o_ref, os_ref, kv0_ref, kv1_ref, kv2_ref, wbf_ref):
    del kv0_in, kv1_in, kv2_in
    j = pl.program_id(0)
    i = pl.program_id(1)

    @pl.when(i == 0)
    def _():
        wbf_ref[...] = w_ref[...].astype(BF16)
        os_ref[...] = jnp.dot(hs_ref[...], wbf_ref[...], preferred_element_type=F32)

    o_ref[...] = jnp.dot(h_ref[...], wbf_ref[...], preferred_element_type=F32)

    for g, kv_ref in enumerate((kv0_ref, kv1_ref, kv2_ref)):
        rows = _kv_rows(g)
        is_kv = (j == K_TILE0 + g) | (j == V_TILE0 + g)
        if min(ATT_GROUPS[g][0], SEQ) < SEQ:
            is_kv = is_kv & (i % IN_TILES_PER_SEQ == IN_TILES_PER_SEQ - 1)

        @pl.when(is_kv)
        def _(kv_ref=kv_ref, rows=rows):
            flat = kv_ref.reshape(rows * H_PER_GROUP, HEAD_DIM)
            for h in range(H_PER_GROUP):
                flat[pl.ds(h, rows, stride=H_PER_GROUP), :] = (
                    o_ref[IN_TM - rows:, h * HEAD_DIM:(h + 1) * HEAD_DIM])


def _kv_index_map(g, layer):
    w = min(ATT_GROUPS[g][0], SEQ)
    n_i = N_PROMPT_ROWS // IN_TM
    full = w >= SEQ

    def index_map(j, i):
        k_col = K_TILE0 + g
        v_col = V_TILE0 + g
        c = jnp.where(j >= v_col, 1, 0)
        in_col = (j == k_col) | (j == v_col)
        if full:
            ii = jnp.where(in_col, i, jnp.where(j < k_col, 0, n_i - 1))
            return (layer, ii // IN_TILES_PER_SEQ, ii % IN_TILES_PER_SEQ, c, 0, 0)
        b_col = jnp.maximum(i - (IN_TILES_PER_SEQ - 1), 0) // IN_TILES_PER_SEQ
        b = jnp.where(in_col, b_col, jnp.where(j < k_col, 0, BATCH - 1))
        return (layer, b, 0, c, 0, 0)

    return index_map


def _in_proj(h_p, h_s, w_in, kv_stacks, layer):
    tm, tn = IN_TM, IN_TN
    out_specs = [
        pl.BlockSpec((tm, tn), lambda j, i: (i, j)),
        pl.BlockSpec((N_SAMPLE_ROWS, tn), lambda j, i: (0, j)),
    ]
    out_shape = [
        jax.ShapeDtypeStruct((N_PROMPT_ROWS, IN_COLS), F32),
        jax.ShapeDtypeStruct((N_SAMPLE_ROWS, IN_COLS), F32),
    ]
    for g, kv in enumerate(kv_stacks):
        out_specs.append(pl.BlockSpec((None, None, _kv_rows(g), None, H_PER_GROUP, HEAD_DIM),
                                      _kv_index_map(g, layer)))
        out_shape.append(jax.ShapeDtypeStruct(kv.shape, kv.dtype))
    n_in = 3
    return pl.pallas_call(
        _in_proj_kernel,
        grid=(IN_COLS // tn, N_PROMPT_ROWS // tm),
        in_specs=[
            pl.BlockSpec((tm, D_MODEL), lambda j, i: (i, 0)),
            pl.BlockSpec((N_SAMPLE_ROWS, D_MODEL), lambda j, i: (0, 0)),
            pl.BlockSpec((None, D_MODEL, tn), lambda j, i: (layer, 0, j)),
        ] + [pl.BlockSpec(memory_space=pl.ANY)] * N_GROUPS,
        out_specs=out_specs,
        out_shape=out_shape,
        input_output_aliases={n_in + g: 2 + g for g in range(N_GROUPS)},
        scratch_shapes=[pltpu.VMEM((D_MODEL, tn), BF16)],
        compiler_params=_cparams(2, IN_PROJ_VMEM_LIMIT),
        name="in_proj",
    )(h_p, h_s, w_in, *kv_stacks)


def _attn_prompt_kernel(q0, k0, v0, q1, k1, v1, q2, k2, v2, z_ref, bias_ref, o_ref,
                        acc_s, m_s, s_s):
    nt = (((1,), (1,)), ((), ()))

    def load(ref, base, d):
        if d == 1:
            return ref[pl.ds(base, BLK), :].astype(BF16)
        return ref[pl.ds(base, BLK, stride=d), :].astype(BF16)

    def store(g, base, d, acc, m, s):
        idx = pl.ds(base, BLK, stride=d)
        acc_s[g - 1, idx, :] = acc
        m_s[g - 1, idx, :] = jnp.broadcast_to(m, (BLK, HEAD_DIM))
        s_s[g - 1, idx, :] = jnp.broadcast_to(s, (BLK, HEAD_DIM))

    def finish(base, acc, m, s):
        sl = pl.ds(base, BLK)
        m1, m2 = m_s[0, sl, :], m_s[1, sl, :]
        mm = jnp.maximum(jnp.maximum(m, m1), m2)
        w0, w1, w2 = jnp.exp(m - mm), jnp.exp(m1 - mm), jnp.exp(m2 - mm)
        num = w0 * acc + w1 * acc_s[0, sl, :] + w2 * acc_s[1, sl, :]
        den = w0 * s + w1 * s_s[0, sl, :] + w2 * s_s[1, sl, :]
        o_ref[sl, :] = ((num / den) * _silu(z_ref[sl, :])).astype(o_ref.dtype)

    def block(g, refs, d, base, pbase, has_prev):
        q_ref, k_ref, v_ref = refs
        q = load(q_ref, base, d)
        lc = lax.dot_general(q, load(k_ref, base, d), nt, preferred_element_type=F32)
        lc = lc * ATT_SCALE + bias_ref[g, :, BLK:]
        if has_prev is None:
            return lc, None
        lp = lax.dot_general(q, load(k_ref, pbase, d), nt, preferred_element_type=F32)
        lp = lp * ATT_SCALE + bias_ref[g, :, :BLK]
        return lc, jnp.where(has_prev, lp, NEG)

    def softmax(lc, lp):
        if lp is None:
            m = jnp.max(lc, axis=-1, keepdims=True)
            pc = jnp.exp(lc - m)
            return pc, None, m, jnp.sum(pc, axis=-1, keepdims=True)
        m = jnp.max(jnp.maximum(lc, lp), axis=-1, keepdims=True)
        pc = jnp.exp(lc - m)
        pp = jnp.exp(lp - m)
        return pc, pp, m, jnp.sum(pc + pp, axis=-1, keepdims=True)

    def weighted(refs, d, base, pbase, pc, pp):
        v_ref = refs[2]
        acc = jnp.dot(pc.astype(BF16), load(v_ref, base, d), preferred_element_type=F32)
        if pp is not None:
            acc = acc + jnp.dot(pp.astype(BF16), load(v_ref, pbase, d), preferred_element_type=F32)
        return acc

    groups = ((q0, k0, v0), (q1, k1, v1), (q2, k2, v2))

    def run_blocks(g, d, specs):
        refs = groups[g]
        logits = [block(g, refs, d, base, pbase, has_prev) for base, pbase, has_prev in specs]
        probs = [softmax(lc, lp) for lc, lp in logits]
        accs = [weighted(refs, d, base, pbase, pc, pp)
                for (base, pbase, _), (pc, pp, _, _) in zip(specs, probs)]
        for (base, _, _), acc, (_, _, m, s) in zip(specs, accs, probs):
            if d == 1:
                finish(base, acc, m, s)
            else:
                store(g, base, d, acc, m, s)

    assert ATT_GROUPS[0][1] == 1 and all(d > 1 for _, d in ATT_GROUPS[1:])
    for g in reversed(range(N_GROUPS)):
        d = ATT_GROUPS[g][1]
        span = BLK * d
        n_blocks = SEQ // span
        if n_blocks == 1:
            def body_r(rr, carry, g=g, d=d):
                run_blocks(g, d, [(rr * ATT_ILP + u, None, None) for u in range(ATT_ILP)])
                return carry
            lax.fori_loop(0, d // ATT_ILP, body_r, 0)
        elif d == 1:
            def body_n(nn, carry, g=g, span=span):
                specs = []
                for u in range(ATT_ILP_MERGE):
                    n = nn * ATT_ILP_MERGE + u
                    specs.append((pl.multiple_of(n * span, BLK),
                                  pl.multiple_of(jnp.maximum(n - 1, 0) * span, BLK), n > 0))
                run_blocks(g, 1, specs)
                return carry
            lax.fori_loop(0, n_blocks // ATT_ILP_MERGE, body_n, 0)
        else:
            n_per_iter = ATT_ILP // d
            assert n_per_iter * d == ATT_ILP and n_blocks % n_per_iter == 0
            def body_n(nn, carry, g=g, d=d, span=span, n_per_iter=n_per_iter):
                specs = []
                for u in range(n_per_iter):
                    n = nn * n_per_iter + u
                    specs += [(n * span + r, jnp.maximum(n - 1, 0) * span + r, n > 0)
                              for r in range(d)]
                run_blocks(g, d, specs)
                return carry
            lax.fori_loop(0, n_blocks // n_per_iter, body_n, 0)


def _attn_prompt(proj3, bias_p):
    def head_spec(col0):
        return pl.BlockSpec((None, SEQ, HEAD_DIM),
                            lambda b, h, c=col0 // HEAD_DIM: (b, 0, c + h))

    in_specs = []
    for g in range(N_GROUPS):
        for col in (COL_Q, COL_K, COL_V):
            in_specs.append(head_spec(col + g * ATT_WIDTH))
    in_specs.append(head_spec(COL_ZATT))
    in_specs.append(pl.BlockSpec((N_GROUPS, None, BLK, 2 * BLK), lambda b, h: (0, h, 0, 0)))
    scratch = [pltpu.VMEM((N_GROUPS - 1, SEQ, HEAD_DIM), F32) for _ in range(3)]
    return pl.pallas_call(
        _attn_prompt_kernel,
        grid=(BATCH, H_PER_GROUP),
        in_specs=in_specs,
        out_specs=pl.BlockSpec((None, SEQ, HEAD_DIM), lambda b, h: (b, 0, h)),
        out_shape=jax.ShapeDtypeStruct((BATCH, SEQ, ATT_WIDTH), BF16),
        scratch_shapes=scratch,
        compiler_params=_cparams(2),
        name="attn_prompt",
    )(*([proj3] * 10), bias_p)


def _lookup(tbl, idx):
    idx = np.asarray(idx, np.int32)
    onehot = (jnp.asarray(idx.reshape(-1, 1)) == jnp.arange(tbl.shape[0], dtype=jnp.int32)[None, :])
    out = jnp.dot(onehot.astype(F32), tbl.astype(F32), precision=lax.Precision.HIGHEST)
    return out.reshape(idx.shape + tbl.shape[1:])


def _prompt_bias(rel_bias):
    qi = np.arange(BLK)[:, None]
    ki = np.arange(2 * BLK)[None, :]
    du = qi + BLK - ki
    band = (du >= 0) & (du <= BLK)
    out = []
    for g, (win, dil) in enumerate(ATT_GROUPS):
        assert win // dil == BLK
        tbl = rel_bias[:, g * H_PER_GROUP:(g + 1) * H_PER_GROUP]
        b = _lookup(tbl, _t5_bucket(np.clip(du, 0, None) * dil))
        b = jnp.where(band[:, :, None], b, NEG)
        out.append(jnp.transpose(b, (2, 0, 1)))
    return jnp.stack(out)


def _softplus(x):
    return jnp.maximum(x, 0.0) + jnp.log1p(jnp.exp(-jnp.abs(x)))


def _lru_gates(xc, wr_ref, br, wi_ref, bi, nsp, n_blocks):
    k = (-0.5 * LRU_C) * nsp
    half_br = 0.5 * br
    half_bi = 0.5 * bi
    a_parts, u_parts = [], []
    for n in range(n_blocks):
        sl = slice(n * LRU_BLOCK, (n + 1) * LRU_BLOCK)
        xn = xc[:, sl]
        xb = xn.astype(BF16)
        wr_half = (0.5 * wr_ref[n]).astype(BF16)
        wi_half = (0.5 * wi_ref[n]).astype(BF16)
        t_r = jnp.tanh(jnp.dot(xb, wr_half, preferred_element_type=F32) + half_br[:, sl])
        t_i = jnp.tanh(jnp.dot(xb, wi_half, preferred_element_type=F32) + half_bi[:, sl])
        log_a = k[:, sl] * t_r + k[:, sl]
        a = jnp.exp(log_a)
        one_minus_a2 = jnp.tanh(log_a) * (-1.0 - a * a)
        root = jnp.where(one_minus_a2 > 0.0, one_minus_a2 * lax.rsqrt(one_minus_a2), 0.0)
        a_parts.append(a)
        u_parts.append(root * ((0.5 * xn) * (t_i + 1.0)))
    return a_parts, u_parts


def _scan8(a, u, rows):
    for s in (1, 2, 4):
        keep = rows >= s
        a_sh = pltpu.roll(a, s, 0)
        u_sh = pltpu.roll(u, s, 0)
        u = jnp.where(keep, a * u_sh + u, u)
        a = jnp.where(keep, a * a_sh, a)
    return a, u


def _lru_prompt_kernel(x_ref, z_ref, cw_ref, cb_ref, wr_ref, br_ref, wi_ref, bi_ref, lam_ref,
                       o_ref, conv_ref, hl_ref, a_s, u_s):
    T, C = x_ref.shape
    n_blocks = C // LRU_BLOCK
    R = 256
    pad = 8

    conv_ref[...] = x_ref[T - (CONV_WIDTH - 1):T, :]

    nsp = _softplus(-lam_ref[...])
    br = br_ref[...]
    bi = bi_ref[...]
    cb = cb_ref[...]

    def gates(c, carry, first=False):
        if first:
            r0 = 0
            xw = jnp.concatenate([jnp.zeros((pad, C), F32), x_ref[0:R, :]], axis=0)
        else:
            r0 = pl.multiple_of(c * R, R)
            xw = x_ref[pl.ds(pl.multiple_of(r0 - pad, pad), R + pad), :]
        xc = xw[pad:] * cw_ref[CONV_WIDTH - 1:CONV_WIDTH, :]
        for tap in range(CONV_WIDTH - 1):
            shifted = pltpu.roll(xw, CONV_WIDTH - 1 - tap, 0)[pad:]
            xc = xc + shifted * cw_ref[tap:tap + 1, :]
        xc = xc + cb
        a_parts, u_parts = _lru_gates(xc, wr_ref, br, wi_ref, bi, nsp, n_blocks)
        for n in range(n_blocks):
            sl = slice(n * LRU_BLOCK, (n + 1) * LRU_BLOCK)
            a_s[pl.ds(r0, R), sl] = a_parts[n]
            u_s[pl.ds(r0, R), sl] = u_parts[n]
        return carry

    gates(0, 0, first=True)
    lax.fori_loop(1, T // R, gates, 0)

    rows = lax.broadcasted_iota(jnp.int32, (8, C), 0)

    def scan(c, h_prev):
        t0 = pl.multiple_of(c * 16, 16)
        hs = []
        for half in range(2):
            sl = pl.ds(t0 + 8 * half, 8)
            a, u = _scan8(a_s[sl, :], u_s[sl, :], rows)
            h = a * h_prev + u
            hs.append(h)
            h_prev = h[7:8, :]
        h16 = jnp.concatenate(hs, axis=0)
        z = z_ref[pl.ds(t0, 16), :]
        o_ref[pl.ds(t0, 16), :] = (h16 * _silu(z)).astype(o_ref.dtype)
        return h_prev

    h_last = lax.fori_loop(0, T // 16, scan, jnp.zeros((1, C), F32))
    hl_ref[...] = h_last


def _lru_prompt(proj3, conv_w, conv_b, w_r, b_r, w_i, b_i, lam, layer):
    tc = 512
    nb = tc // LRU_BLOCK
    n_ct = LRU_WIDTH // tc

    def col_spec(col0):
        return pl.BlockSpec((None, SEQ, tc), lambda b, c, c0=col0 // tc: (b, 0, c0 + c))

    vec_spec = pl.BlockSpec((None, 1, tc), lambda b, c: (layer, 0, c))
    w_spec = pl.BlockSpec((None, nb, LRU_BLOCK, LRU_BLOCK), lambda b, c: (layer, c, 0, 0))
    return pl.pallas_call(
        _lru_prompt_kernel,
        grid=(BATCH, n_ct),
        in_specs=[
            col_spec(COL_XLRU), col_spec(COL_ZLRU),
            pl.BlockSpec((None, CONV_WIDTH, tc), lambda b, c: (layer, 0, c)),
            vec_spec, w_spec, vec_spec, w_spec, vec_spec, vec_spec,
        ],
        out_specs=[
            pl.BlockSpec((None, SEQ, tc), lambda b, c: (b, 0, c)),
            pl.BlockSpec((None, CONV_WIDTH - 1, tc), lambda b, c: (b, 0, c)),
            pl.BlockSpec((None, 1, tc), lambda b, c: (b, 0, c)),
        ],
        out_shape=[
            jax.ShapeDtypeStruct((BATCH, SEQ, LRU_WIDTH), BF16),
            jax.ShapeDtypeStruct((BATCH, CONV_WIDTH - 1, LRU_WIDTH), F32),
            jax.ShapeDtypeStruct((BATCH, 1, LRU_WIDTH), F32),
        ],
        scratch_shapes=[
            pltpu.VMEM((SEQ, tc), F32),
            pltpu.VMEM((SEQ, tc), F32),
        ],
        compiler_params=_cparams(2),
        name="lru_prompt",
    )(proj3, proj3, conv_w, conv_b, w_r, b_r, w_i, b_i, lam)


def _mix_kernel(att_ref, lru_ref, ga_ref, gl_ref, wpa_ref, wpb_ref, o_ref):
    y_att = jnp.dot(att_ref[...], wpa_ref[...], preferred_element_type=F32)
    y_lru = jnp.dot(lru_ref[...], wpb_ref[...], preferred_element_type=F32)
    merged = _sigmoid(ga_ref[...]) * y_att + _sigmoid(gl_ref[...]) * y_lru
    o_ref[...] = merged.astype(o_ref.dtype)


def _mix(att, lru, proj, wpa, wpb, layer, tm):
    m_rows = att.shape[0]
    return pl.pallas_call(
        _mix_kernel,
        grid=(m_rows // tm,),
        in_specs=[
            pl.BlockSpec((tm, ATT_WIDTH), lambda i: (i, 0)),
            pl.BlockSpec((tm, LRU_WIDTH), lambda i: (i, 0)),
            pl.BlockSpec((tm, D_MODEL), lambda i: (i, COL_GATT // D_MODEL)),
            pl.BlockSpec((tm, D_MODEL), lambda i: (i, COL_GLRU // D_MODEL)),
            pl.BlockSpec((None, ATT_WIDTH, D_MODEL), lambda i: (layer, 0, 0),
                         pipeline_mode=pl.Buffered(1)),
            pl.BlockSpec((None, LRU_WIDTH, D_MODEL), lambda i: (layer, 0, 0),
                         pipeline_mode=pl.Buffered(1)),
        ],
        out_specs=pl.BlockSpec((tm, D_MODEL), lambda i: (i, 0)),
        out_shape=jax.ShapeDtypeStruct((m_rows, D_MODEL), BF16),
        compiler_params=_cparams(1),
        name="branch_mix",
    )(att, lru, proj, proj, wpa, wpb)


def _residual_kernel(m_ref, w_ref, x_ref, gate_ref, g_ref, *rest, last):
    out = jnp.dot(m_ref[...], w_ref[...], preferred_element_type=F32)
    x_new = x_ref[...] + gate_ref[...] * out
    y = x_new * lax.rsqrt(jnp.mean(x_new * x_new, axis=-1, keepdims=True) + RMS_EPS)
    y = y * g_ref[...]
    if last:
        (y_ref,) = rest
        y_ref[...] = y
    else:
        sc_ref, sh_ref, xo_ref, ho_ref = rest
        xo_ref[...] = x_new
        ho_ref[...] = (y * (1.0 + sc_ref[...]) + sh_ref[...]).astype(ho_ref.dtype)


def _residual_prompt(merged, w_out, x, gate, g, scale, shift, layer):
    tm = 512
    last = scale is None
    row_spec = pl.BlockSpec((None, tm, D_MODEL), lambda b, i: (b, i, 0))
    vec_spec = pl.BlockSpec((None, 1, D_MODEL), lambda b, i: (b, 0, 0))
    in_specs = [
        row_spec,
        pl.BlockSpec((None, D_MODEL, D_MODEL), lambda b, i: (layer, 0, 0),
                     pipeline_mode=pl.Buffered(1)),
        row_spec,
        vec_spec,
        pl.BlockSpec((1, D_MODEL), lambda b, i: (0, 0)),
    ]
    args = [merged, w_out, x, gate, g.reshape(1, D_MODEL)]
    if last:
        out_specs = row_spec
        out_shape = jax.ShapeDtypeStruct(x.shape, F32)
    else:
        in_specs += [vec_spec, vec_spec]
        args += [scale, shift]
        out_specs = [row_spec, row_spec]
        out_shape = [jax.ShapeDtypeStruct(x.shape, F32), jax.ShapeDtypeStruct(x.shape, BF16)]
    return pl.pallas_call(
        functools.partial(_residual_kernel, last=last),
        grid=(BATCH, SEQ // tm),
        in_specs=in_specs,
        out_specs=out_specs,
        out_shape=out_shape,
        compiler_params=_cparams(2),
        name="residual_prompt",
    )(*args)


def _residual_sample(merged, w_out, x, gate_rows, g, scale_rows, shift_rows, layer):
    last = scale_rows is None
    full = pl.BlockSpec((N_SAMPLE_ROWS, D_MODEL), lambda i: (0, 0))
    in_specs = [full, pl.BlockSpec((None, D_MODEL, D_MODEL), lambda i: (layer, 0, 0)), full, full,
                pl.BlockSpec((1, D_MODEL), lambda i: (0, 0))]
    args = [merged, w_out, x, gate_rows, g.reshape(1, D_MODEL)]
    if last:
        out_specs = full
        out_shape = jax.ShapeDtypeStruct(x.shape, F32)
    else:
        in_specs += [full, full]
        args += [scale_rows, shift_rows]
        out_specs = [full, full]
        out_shape = [jax.ShapeDtypeStruct(x.shape, F32), jax.ShapeDtypeStruct(x.shape, BF16)]
    return pl.pallas_call(
        functools.partial(_residual_kernel, last=last),
        grid=(1,),
        in_specs=in_specs,
        out_specs=out_specs,
        out_shape=out_shape,
        compiler_params=_cparams(1),
        name="residual_sample",
    )(*args)


CACHE_CHUNK_ROWS = 512
CACHE_SLOTS = 6
CACHE_LOOKAHEAD = 3


def _cache_update_kernel(c0, c1, c2, n0, n1, n2, o0, o1, o2, buf, in_sem, out_sem, new_sem):
    chunks = []
    new_copies = []
    for gi, (c, n, o) in enumerate(((c0, n0, o0), (c1, n1, o1), (c2, n2, o2))):
        keep = c.shape[2] - DEC_SEQ
        for l in range(DEPTH):
            new_copies.append(pltpu.make_async_copy(
                n.at[l], o.at[l, :, pl.ds(keep, DEC_SEQ)], new_sem.at[gi, l]))
            for b in range(DEC_BATCH):
                for r in range(0, keep, CACHE_CHUNK_ROWS):
                    rows = min(CACHE_CHUNK_ROWS, keep - r)
                    chunks.append((c.at[l, b, pl.ds(DEC_SEQ + r, rows)],
                                   o.at[l, b, pl.ds(r, rows)], rows))

    def read(i):
        src, _, rows = chunks[i]
        slot = i % CACHE_SLOTS
        return pltpu.make_async_copy(src, buf.at[slot, pl.ds(0, rows)], in_sem.at[slot])

    def write(i):
        _, dst, rows = chunks[i]
        slot = i % CACHE_SLOTS
        return pltpu.make_async_copy(buf.at[slot, pl.ds(0, rows)], dst, out_sem.at[slot])

    for cp in new_copies:
        cp.start()
    n_chunks = len(chunks)
    for i in range(n_chunks + CACHE_LOOKAHEAD):
        if i < n_chunks:
            if i >= CACHE_SLOTS:
                write(i - CACHE_SLOTS).wait()
            read(i).start()
        j = i - CACHE_LOOKAHEAD
        if j >= 0:
            read(j).wait()
            write(j).start()
    for j in range(max(n_chunks - CACHE_SLOTS, 0), n_chunks):
        write(j).wait()
    for cp in new_copies:
        cp.wait()


def _cache_update(caches, new_rows):
    any_spec = pl.BlockSpec(memory_space=pl.ANY)
    vmem_spec = pl.BlockSpec(memory_space=pltpu.VMEM)
    return pl.pallas_call(
        _cache_update_kernel,
        in_specs=[any_spec] * 3 + [vmem_spec] * 3,
        out_specs=[any_spec] * 3,
        out_shape=[jax.ShapeDtypeStruct(c.shape, c.dtype) for c in caches],
        scratch_shapes=[
            pltpu.VMEM((CACHE_SLOTS, CACHE_CHUNK_ROWS, 2, H_PER_GROUP, HEAD_DIM), F32),
            pltpu.SemaphoreType.DMA((CACHE_SLOTS,)),
            pltpu.SemaphoreType.DMA((CACHE_SLOTS,)),
            pltpu.SemaphoreType.DMA((N_GROUPS, DEPTH)),
        ],
        compiler_params=pltpu.CompilerParams(vmem_limit_bytes=VMEM_LIMIT),
        name="cache_update",
    )(*caches, *new_rows)


def _attn_sample_kernel(q_ref, k_ref, v_ref, z_ref, c0_ref, c1_ref, c2_ref, bc_ref, bn_ref,
                        o_ref, kv0_ref, kv1_ref, kv2_ref):
    hp = H_PER_GROUP
    kv_refs = (kv0_ref, kv1_ref, kv2_ref)
    k_new, v_new = [], []
    for g in range(N_GROUPS):
        kn = k_ref[:, g * hp:(g + 1) * hp, :]
        vn = v_ref[:, g * hp:(g + 1) * hp, :]
        kv_refs[g][:, 0] = kn
        kv_refs[g][:, 1] = vn
        k_new.append(kn)
        v_new.append(vn)

    for s in range(DEC_SEQ):
        accs, ms, ss = [], [], []
        for g in range(N_GROUPS):
            qg = q_ref[s, g * hp:(g + 1) * hp, :]
            if g == 0:
                kc, vc = c0_ref[:, 0], c0_ref[:, 1]
            elif g == 1:
                kc, vc = c1_ref[:, s, 0], c1_ref[:, s, 1]
            else:
                kc, vc = c2_ref[:, s, 0], c2_ref[:, s, 1]
            lc = jnp.sum(kc * qg[None], axis=-1, keepdims=True) * ATT_SCALE + bc_ref[g, s]
            ln = jnp.sum(k_new[g] * qg[None], axis=-1, keepdims=True) * ATT_SCALE + bn_ref[g, s]
            m = jnp.maximum(jnp.max(lc, axis=0), jnp.max(ln, axis=0))
            pc = jnp.exp(lc - m[None])
            pn = jnp.exp(ln - m[None])
            ss.append(jnp.sum(pc, axis=0) + jnp.sum(pn, axis=0))
            accs.append(jnp.sum(pc * vc, axis=0) + jnp.sum(pn * v_new[g], axis=0))
            ms.append(m)
        mm = jnp.maximum(jnp.maximum(ms[0], ms[1]), ms[2])
        ws = [jnp.exp(m - mm) for m in ms]
        num = ws[0] * accs[0] + ws[1] * accs[1] + ws[2] * accs[2]
        den = ws[0] * ss[0] + ws[1] * ss[1] + ws[2] * ss[2]
        o_ref[s] = (num / den) * _silu(z_ref[s])


def _attn_sample(proj_s4, caches, bias_c, bias_n, layer):
    c0, c1, c2 = caches
    hp = H_PER_GROUP

    def head_spec(col0, n_heads):
        return pl.BlockSpec((None, DEC_SEQ, n_heads, HEAD_DIM),
                            lambda b, c=col0 // (HEAD_DIM * n_heads): (b, 0, c, 0))

    in_specs = [
        head_spec(COL_Q, N_ATT_HEADS), head_spec(COL_K, N_ATT_HEADS), head_spec(COL_V, N_ATT_HEADS),
        head_spec(COL_ZATT, hp),
        pl.BlockSpec((None, None, BLK, 2, hp, HEAD_DIM), lambda b: (layer, b, 0, 0, 0, 0)),
        pl.BlockSpec((None, None, BLK, DEC_SEQ, 2, hp, HEAD_DIM), lambda b: (layer, b, 0, 0, 0, 0, 0)),
        pl.BlockSpec((None, None, BLK, DEC_SEQ, 2, hp, HEAD_DIM), lambda b: (layer, b, 0, 0, 0, 0, 0)),
        pl.BlockSpec(bias_c.shape, lambda b: (0,) * 5),
        pl.BlockSpec(bias_n.shape, lambda b: (0,) * 5),
    ]
    out_specs = [pl.BlockSpec((None, DEC_SEQ, hp, HEAD_DIM), lambda b: (b, 0, 0, 0))]
    out_shape = [jax.ShapeDtypeStruct((DEC_BATCH, DEC_SEQ, hp, HEAD_DIM), F32)]
    for _ in range(N_GROUPS):
        out_specs.append(pl.BlockSpec((None, DEC_SEQ, 2, hp, HEAD_DIM), lambda b: (b, 0, 0, 0, 0)))
        out_shape.append(jax.ShapeDtypeStruct((DEC_BATCH, DEC_SEQ, 2, hp, HEAD_DIM), F32))
    return pl.pallas_call(
        _attn_sample_kernel,
        grid=(DEC_BATCH,),
        in_specs=in_specs,
        out_specs=out_specs,
        out_shape=out_shape,
        compiler_params=_cparams(1),
        name="attn_sample",
    )(proj_s4, proj_s4, proj_s4, proj_s4, c0, c1, c2, bias_c, bias_n)


def _sample_bias(rel_bias):
    s_idx = np.arange(DEC_SEQ)
    bc, bn = [], []
    for g, (win, dil) in enumerate(ATT_GROUPS):
        tbl = rel_bias[:, g * H_PER_GROUP:(g + 1) * H_PER_GROUP].astype(F32)
        key = np.arange(BLK)
        if dil == 1:
            j = BLK + s_idx[:, None] - key[None, :]
            valid = j <= BLK
        else:
            j = np.broadcast_to(BLK - key[None, :], (DEC_SEQ, BLK))
            valid = np.ones_like(j, dtype=bool)
        b = _lookup(tbl, _t5_bucket(np.clip(j, 0, None) * dil))
        bc.append(jnp.where(valid[:, :, None], b, NEG))
        jn = s_idx[:, None] - s_idx[None, :]
        valid_n = (jn >= 0) & (jn * dil <= win) & ((jn == 0) | (dil == 1))
        b = _lookup(tbl, _t5_bucket(np.clip(jn, 0, None) * dil))
        bn.append(jnp.where(valid_n[:, :, None], b, NEG))
    bc = jnp.stack(bc)
    bn = jnp.stack(bn)
    bc = jnp.broadcast_to(bc[..., None], bc.shape + (HEAD_DIM,))
    bn = jnp.broadcast_to(bn[..., None], bn.shape + (HEAD_DIM,))
    return bc, bn


def _lru_sample_kernel(x_ref, z_ref, cs_ref, h0_ref, cw_ref, cb_ref, wr_ref, br_ref, wi_ref, bi_ref,
                       lam_ref, o_ref, conv_ref, hl_ref):
    S = DEC_SEQ
    xp = [cs_ref[t] for t in range(CONV_WIDTH - 1)] + [x_ref[t] for t in range(S)]
    xc = []
    for t in range(S):
        y = xp[t] * cw_ref[0:1, :]
        for tap in range(1, CONV_WIDTH):
            y = y + xp[t + tap] * cw_ref[tap:tap + 1, :]
        xc.append(y + cb_ref[...])
    xcat = jnp.concatenate(xc, axis=0)
    nsp = _softplus(-lam_ref[...])
    a_parts, u_parts = _lru_gates(xcat, wr_ref, br_ref[...], wi_ref, bi_ref[...], nsp, LRU_BLOCKS)
    a = jnp.concatenate(a_parts, axis=1)
    u = jnp.concatenate(u_parts, axis=1)
    nb = DEC_BATCH
    h = h0_ref[...]
    for t in range(S):
        h = a[t * nb:(t + 1) * nb] * h + u[t * nb:(t + 1) * nb]
        o_ref[t] = h * _silu(z_ref[t])
    for t in range(CONV_WIDTH - 1):
        conv_ref[t] = xp[S + t]
    hl_ref[...] = h


def _lru_sample(x_t, z_t, conv_t, h0, conv_w, conv_b, w_r, b_r, w_i, b_i, lam, layer):
    C = LRU_WIDTH

    def full(shape):
        return pl.BlockSpec(shape, lambda i: (0,) * len(shape))

    vec_spec = pl.BlockSpec((None, 1, C), lambda i: (layer, 0, 0))
    w_spec = pl.BlockSpec((None, LRU_BLOCKS, LRU_BLOCK, LRU_BLOCK), lambda i: (layer, 0, 0, 0))
    return pl.pallas_call(
        _lru_sample_kernel,
        grid=(1,),
        in_specs=[
            full((DEC_SEQ, DEC_BATCH, C)), full((DEC_SEQ, DEC_BATCH, C)),
            full((CONV_WIDTH - 1, DEC_BATCH, C)), full((DEC_BATCH, C)),
            pl.BlockSpec((None, CONV_WIDTH, C), lambda i: (layer, 0, 0)),
            vec_spec, w_spec, vec_spec, w_spec, vec_spec, vec_spec,
        ],
        out_specs=[full((DEC_SEQ, DEC_BATCH, C)), full((CONV_WIDTH - 1, DEC_BATCH, C)),
                   full((DEC_BATCH, C))],
        out_shape=[
            jax.ShapeDtypeStruct((DEC_SEQ, DEC_BATCH, C), F32),
            jax.ShapeDtypeStruct((CONV_WIDTH - 1, DEC_BATCH, C), F32),
            jax.ShapeDtypeStruct((DEC_BATCH, C), F32),
        ],
        compiler_params=_cparams(1),
        name="lru_sample",
    )(x_t, z_t, conv_t, h0, conv_w, conv_b, w_r, b_r, w_i, b_i, lam)


def kernel(x_prompt, x_sample, c_prompt, c_sample, cache_kv_g0, cache_kv_g1, cache_kv_g2, state_conv, state_h, rel_bias, w_ada, b_ada, norm_g, w_in, conv_w, conv_b, w_r, b_r, w_i, b_i, lam, w_pa, w_pb, w_out, final_g):
    L, B, T, D = DEPTH, BATCH, SEQ, D_MODEL
    Bd, S = DEC_BATCH, DEC_SEQ

    c_all = jnp.concatenate(
        [c_prompt, c_sample, jnp.zeros((MOD_ROWS - B - Bd, D), F32)], axis=0)
    mod = _modulation(c_all, w_ada, b_ada).reshape(L, MOD_ROWS, 3, D)
    mod_p = mod[:, :B]
    mod_s = jnp.repeat(mod[:, B:B + Bd], S, axis=1)

    wpa_bf = w_pa.astype(BF16)
    wpb_bf = w_pb.astype(BF16)
    wout_bf = w_out.astype(BF16)
    conv_b3 = conv_b.reshape(L, 1, LRU_WIDTH)
    b_r3 = b_r.reshape(L, 1, LRU_WIDTH)
    b_i3 = b_i.reshape(L, 1, LRU_WIDTH)
    lam3 = lam.reshape(L, 1, LRU_WIDTH)

    bias_p = _prompt_bias(rel_bias)
    bias_c, bias_n = _sample_bias(rel_bias)

    cache_views = (
        cache_kv_g0,
        cache_kv_g1.reshape(L, Bd, BLK, 4, 2, H_PER_GROUP, HEAD_DIM),
        cache_kv_g2.reshape(L, Bd, BLK, 16, 2, H_PER_GROUP, HEAD_DIM),
    )

    xp = x_prompt
    xs = x_sample.reshape(Bd * S, D)
    kv_p = [jnp.zeros((L, B, min(win, T), 2, H_PER_GROUP, HEAD_DIM), F32) for win, _ in ATT_GROUPS]
    kv_new = ([], [], [])
    conv_p, h_p, conv_s, h_s = [], [], [], []

    def mods(l):
        shift_p, scale_p, gate_p = (mod_p[l, :, i][:, None, :] for i in range(3))
        shift_s, scale_s, gate_s = (mod_s[l, :, i] for i in range(3))
        return (shift_p, scale_p, gate_p), (shift_s, scale_s, gate_s)

    (shift_p, scale_p, gate_p), (shift_s, scale_s, gate_s) = mods(0)
    hp_ = _norm_prompt(xp, norm_g[0], scale_p, shift_p)
    hs_ = _norm_sample(xs, norm_g[0], scale_s, shift_s)
    for l in range(L):
        proj_p, proj_s, *kv_p = _in_proj(hp_.reshape(B * T, D), hs_, w_in, kv_p, l)
        proj3 = proj_p.reshape(B, T, IN_COLS)

        att_p = _attn_prompt(proj3, bias_p)
        lru_p, cv, hl = _lru_prompt(proj3, conv_w, conv_b3, w_r, b_r3, w_i, b_i3, lam3, l)
        merged_p = _mix(att_p.reshape(B * T, ATT_WIDTH), lru_p.reshape(B * T, LRU_WIDTH),
                        proj_p, wpa_bf, wpb_bf, l, 512)
        last = l == L - 1
        if not last:
            (shift_pn, scale_pn, gate_pn), (shift_sn, scale_sn, gate_sn) = mods(l + 1)
            xp, hp_ = _residual_prompt(merged_p.reshape(B, T, D), wout_bf, xp, gate_p,
                                       norm_g[l + 1], scale_pn, shift_pn, l)
        else:
            y_prompt = _residual_prompt(merged_p.reshape(B, T, D), wout_bf, xp, gate_p,
                                        final_g, None, None, l)
        conv_p.append(cv)
        h_p.append(hl.reshape(B, LRU_WIDTH))

        proj_s4 = proj_s.reshape(Bd, S, IN_COLS // HEAD_DIM, HEAD_DIM)
        outs = _attn_sample(proj_s4, cache_views, bias_c, bias_n, l)
        att_s = outs[0]
        for g in range(N_GROUPS):
            kv_new[g].append(outs[1 + g])
        ps3 = proj_s.reshape(Bd, S, IN_COLS)
        x_t = jnp.transpose(ps3[:, :, COL_XLRU:COL_XLRU + LRU_WIDTH], (1, 0, 2))
        z_t = jnp.transpose(ps3[:, :, COL_ZLRU:COL_ZLRU + LRU_WIDTH], (1, 0, 2))
        conv_t = jnp.transpose(state_conv[l], (1, 0, 2))
        lru_t, cv_t, hl_s = _lru_sample(x_t, z_t, conv_t, state_h[l], conv_w, conv_b3,
                                        w_r, b_r3, w_i, b_i3, lam3, l)
        lru_s = jnp.transpose(lru_t, (1, 0, 2)).reshape(Bd * S, LRU_WIDTH).astype(BF16)
        merged_s = _mix(att_s.reshape(Bd * S, ATT_WIDTH).astype(BF16), lru_s, proj_s,
                        wpa_bf, wpb_bf, l, Bd * S)
        if not last:
            xs, hs_ = _residual_sample(merged_s, wout_bf, xs, gate_s,
                                       norm_g[l + 1], scale_sn, shift_sn, l)
            gate_p, gate_s = gate_pn, gate_sn
        else:
            y_sample = _residual_sample(merged_s, wout_bf, xs, gate_s,
                                        final_g, None, None, l).reshape(Bd, S, D)
        conv_s.append(jnp.transpose(cv_t, (1, 0, 2)))
        h_s.append(hl_s)

    kv_s = _cache_update((cache_kv_g0, cache_kv_g1, cache_kv_g2),
                         tuple(jnp.stack(rows) for rows in kv_new))
    return (y_prompt, y_sample,
            kv_p[0], kv_p[1], kv_p[2],
            jnp.stack(conv_p), jnp.stack(h_p),
            kv_s[0], kv_s[1], kv_s[2],
            jnp.stack(conv_s), jnp.stack(h_s))
```

```python
import functools
import math

import jax
import jax.numpy as jnp
import numpy as np
from jax import lax
from jax.experimental import pallas as pl
from jax.experimental.pallas import tpu as pltpu

D_MODEL = 2048
BATCH = 4
SEQ = 2048
DEPTH = 4
DEC_BATCH = 8
DEC_SEQ = 4
HEAD_DIM = 128
H_PER_GROUP = 8
ATT_GROUPS = ((128, 1), (512, 4), (2048, 16))
N_GROUPS = 3
N_ATT_HEADS = N_GROUPS * H_PER_GROUP
QKV_WIDTH = N_ATT_HEADS * HEAD_DIM
ATT_WIDTH = H_PER_GROUP * HEAD_DIM
N_BUCKETS = 32
REL_MAX_DIST = 2048
BLK = 128
LRU_WIDTH = D_MODEL
LRU_BLOCKS = 16
LRU_BLOCK = LRU_WIDTH // LRU_BLOCKS
CONV_WIDTH = 4
LRU_C = 8.0
IN_COLS = 3 * QKV_WIDTH + ATT_WIDTH + 2 * LRU_WIDTH + 2 * D_MODEL
RMS_EPS = 1e-6
NEG = -1e30
ATT_SCALE = HEAD_DIM ** -0.5

COL_Q = 0
COL_K = QKV_WIDTH
COL_V = 2 * QKV_WIDTH
COL_ZATT = 3 * QKV_WIDTH
COL_XLRU = COL_ZATT + ATT_WIDTH
COL_ZLRU = COL_XLRU + LRU_WIDTH
COL_GATT = COL_ZLRU + LRU_WIDTH
COL_GLRU = COL_GATT + D_MODEL

N_PROMPT_ROWS = BATCH * SEQ
N_SAMPLE_ROWS = DEC_BATCH * DEC_SEQ
MOD_ROWS = 16

VMEM_LIMIT = 52 * 1024 * 1024
ATT_ILP = 16
ATT_ILP_MERGE = 16

F32 = jnp.float32
BF16 = jnp.bfloat16


def _cparams(n_grid_dims, vmem_limit=VMEM_LIMIT):
    return pltpu.CompilerParams(
        dimension_semantics=("arbitrary",) * n_grid_dims,
        vmem_limit_bytes=vmem_limit)


def _sigmoid(x):
    return 0.5 * jnp.tanh(0.5 * x) + 0.5


def _silu(x):
    half = 0.5 * x
    return half * (jnp.tanh(half) + 1.0)


def _t5_bucket(dist):
    dist = np.asarray(dist).astype(np.int32)
    max_exact = N_BUCKETS // 2
    safe = np.maximum(dist, 1).astype(np.float32)
    large = max_exact + (np.log(safe / max_exact) / np.float32(math.log(REL_MAX_DIST / max_exact))
                         * (N_BUCKETS - max_exact)).astype(np.int32)
    large = np.minimum(large, N_BUCKETS - 1)
    return np.where(dist < max_exact, dist, large).astype(np.int32)


def _mod_kernel(c_ref, w_ref, b_ref, o_ref):
    c = _silu(c_ref[...])
    o_ref[...] = jnp.dot(c.astype(BF16), w_ref[...].astype(BF16),
                         preferred_element_type=F32) + b_ref[...]


def _modulation(c_all, w_ada, b_ada):
    tn = 1024
    n_cols = 3 * D_MODEL
    return pl.pallas_call(
        _mod_kernel,
        grid=(DEPTH, n_cols // tn),
        in_specs=[
            pl.BlockSpec((MOD_ROWS, D_MODEL), lambda l, j: (0, 0)),
            pl.BlockSpec((None, D_MODEL, tn), lambda l, j: (l, 0, j)),
            pl.BlockSpec((None, 1, tn), lambda l, j: (l, 0, j)),
        ],
        out_specs=pl.BlockSpec((None, MOD_ROWS, tn), lambda l, j: (l, 0, j)),
        out_shape=jax.ShapeDtypeStruct((DEPTH, MOD_ROWS, n_cols), F32),
        compiler_params=_cparams(2),
        name="adaln_mod",
    )(c_all, w_ada, b_ada.reshape(DEPTH, 1, n_cols))


def _norm_kernel(x_ref, g_ref, sc_ref, sh_ref, o_ref):
    x = x_ref[...]
    y = x * lax.rsqrt(jnp.mean(x * x, axis=-1, keepdims=True) + RMS_EPS)
    y = y * g_ref[...]
    o_ref[...] = (y * (1.0 + sc_ref[...]) + sh_ref[...]).astype(o_ref.dtype)


def _norm_prompt(x, g, scale, shift):
    tm = 512
    row_spec = pl.BlockSpec((None, tm, D_MODEL), lambda b, i: (b, i, 0))
    vec_spec = pl.BlockSpec((None, 1, D_MODEL), lambda b, i: (b, 0, 0))
    return pl.pallas_call(
        _norm_kernel,
        grid=(BATCH, SEQ // tm),
        in_specs=[row_spec, pl.BlockSpec((1, D_MODEL), lambda b, i: (0, 0)), vec_spec, vec_spec],
        out_specs=row_spec,
        out_shape=jax.ShapeDtypeStruct(x.shape, BF16),
        compiler_params=_cparams(2),
        name="rmsnorm_prompt",
    )(x, g.reshape(1, D_MODEL), scale, shift)


def _norm_sample(x, g, scale_rows, shift_rows):
    return pl.pallas_call(
        _norm_kernel,
        out_shape=jax.ShapeDtypeStruct(x.shape, BF16),
        name="rmsnorm_sample",
    )(x, g.reshape(1, D_MODEL), scale_rows, shift_rows)


IN_PROJ_VMEM_LIMIT = 58 * 1024 * 1024
IN_TM = 1024
IN_TN = ATT_WIDTH
IN_TILES_PER_SEQ = SEQ // IN_TM
N_ROW_TILES = N_PROMPT_ROWS // IN_TM
K_TILE0 = COL_K // IN_TN
V_TILE0 = COL_V // IN_TN


def _kv_rows(g):
    w = min(ATT_GROUPS[g][0], SEQ)
    assert w % IN_TM == 0 or IN_TM % w == 0
    return min(w, IN_TM)


def _kv_blocks_per_layer(g):
    n_wblk = min(ATT_GROUPS[g][0], SEQ) // _kv_rows(g)
    return BATCH * n_wblk * 2, n_wblk


def _in_proj_kernel(h_ref, hs_ref, w_ref, *rest, creates_kv):
    if creates_kv:
        o_ref, os_ref, kv0_ref, kv1_ref, kv2_ref, wbf_ref = rest
    else:
        o_ref, os_ref, kv0_ref, kv1_ref, kv2_ref, wbf_ref = rest[N_GROUPS:]
    j = pl.program_id(0)
    i = pl.program_id(1)

    @pl.when(i == 0)
    def _():
        wbf_ref[...] = w_ref[...].astype(BF16)
        os_ref[...] = jnp.dot(hs_ref[...], wbf_ref[...], preferred_element_type=F32)

    o_ref[...] = jnp.dot(h_ref[...], wbf_ref[...], preferred_element_type=F32)

    for g, kv_ref in enumerate((kv0_ref, kv1_ref, kv2_ref)):
        rows = _kv_rows(g)
        is_kv = (j == K_TILE0 + g) | (j == V_TILE0 + g)
        if min(ATT_GROUPS[g][0], SEQ) < SEQ:
            is_kv = is_kv & (i % IN_TILES_PER_SEQ == IN_TILES_PER_SEQ - 1)

        @pl.when(is_kv)
        def _(kv_ref=kv_ref, rows=rows):
            flat = kv_ref.reshape(rows * H_PER_GROUP, HEAD_DIM)
            for h in range(H_PER_GROUP):
                flat[pl.ds(h, rows, stride=H_PER_GROUP), :] = (
                    o_ref[IN_TM - rows:, h * HEAD_DIM:(h + 1) * HEAD_DIM])

        if creates_kv:
            per_layer, _ = _kv_blocks_per_layer(g)
            t = (j - (V_TILE0 + g) - 1) * N_ROW_TILES + i

            @pl.when((j > V_TILE0 + g) & (t < (DEPTH - 1) * per_layer))
            def _(kv_ref=kv_ref):
                kv_ref[...] = jnp.zeros(kv_ref.shape, kv_ref.dtype)


def _kv_index_map(g, layer, creates_kv):
    w = min(ATT_GROUPS[g][0], SEQ)
    full = w >= SEQ
    per_layer, n_wblk = _kv_blocks_per_layer(g)

    def index_map(j, i):
        k_col = K_TILE0 + g
        v_col = V_TILE0 + g
        c = jnp.where(j >= v_col, 1, 0)
        in_col = (j == k_col) | (j == v_col)
        if full:
            ii = jnp.where(in_col, i, jnp.where(j < k_col, 0, N_ROW_TILES - 1))
            b, wblk = ii // IN_TILES_PER_SEQ, ii % IN_TILES_PER_SEQ
        else:
            b_col = jnp.maximum(i - (IN_TILES_PER_SEQ - 1), 0) // IN_TILES_PER_SEQ
            b = jnp.where(in_col, b_col, jnp.where(j < k_col, 0, BATCH - 1))
            wblk = 0
        if not creates_kv:
            return (layer, b, wblk, c, 0, 0)
        assert layer == 0 and (IN_COLS // IN_TN - v_col - 1) * N_ROW_TILES >= (DEPTH - 1) * per_layer
        f = jnp.clip((j - v_col - 1) * N_ROW_TILES + i, 0, (DEPTH - 1) * per_layer - 1)
        rest = f % per_layer
        fill = j > v_col
        return (jnp.where(fill, 1 + f // per_layer, 0),
                jnp.where(fill, rest // (2 * n_wblk), b),
                jnp.where(fill, (rest // 2) % n_wblk, wblk),
                jnp.where(fill, rest % 2, c), 0, 0)

    return index_map


def _in_proj(h_p, h_s, w_in, kv_stacks, layer):
    tm, tn = IN_TM, IN_TN
    creates_kv = kv_stacks is None
    if creates_kv:
        kv_shapes = [jax.ShapeDtypeStruct((DEPTH, BATCH, min(win, SEQ), 2, H_PER_GROUP, HEAD_DIM), F32)
                     for win, _ in ATT_GROUPS]
        kv_stacks = ()
    else:
        kv_shapes = [jax.ShapeDtypeStruct(kv.shape, kv.dtype) for kv in kv_stacks]
    out_specs = [
        pl.BlockSpec((tm, tn), lambda j, i: (i, j)),
        pl.BlockSpec((N_SAMPLE_ROWS, tn), lambda j, i: (0, j)),
    ]
    out_shape = [
        jax.ShapeDtypeStruct((N_PROMPT_ROWS, IN_COLS), F32),
        jax.ShapeDtypeStruct((N_SAMPLE_ROWS, IN_COLS), F32),
    ]
    for g, shape in enumerate(kv_shapes):
        out_specs.append(pl.BlockSpec((None, None, _kv_rows(g), None, H_PER_GROUP, HEAD_DIM),
                                      _kv_index_map(g, layer, creates_kv)))
        out_shape.append(shape)
    n_in = 3
    return pl.pallas_call(
        functools.partial(_in_proj_kernel, creates_kv=creates_kv),
        grid=(IN_COLS // tn, N_ROW_TILES),
        in_specs=[
            pl.BlockSpec((tm, D_MODEL), lambda j, i: (i, 0)),
            pl.BlockSpec((N_SAMPLE_ROWS, D_MODEL), lambda j, i: (0, 0)),
            pl.BlockSpec((None, D_MODEL, tn), lambda j, i: (layer, 0, j)),
        ] + [pl.BlockSpec(memory_space=pl.ANY)] * len(kv_stacks),
        out_specs=out_specs,
        out_shape=out_shape,
        input_output_aliases={n_in + g: 2 + g for g in range(len(kv_stacks))},
        scratch_shapes=[pltpu.VMEM((D_MODEL, tn), BF16)],
        compiler_params=_cparams(2, IN_PROJ_VMEM_LIMIT),
        name="in_proj",
    )(h_p, h_s, w_in, *kv_stacks)


def _attn_prompt_kernel(q0, k0, v0, q1, k1, v1, q2, k2, v2, z_ref, bias_ref, o_ref,
                        acc_s, m_s, s_s):
    nt = (((1,), (1,)), ((), ()))

    def load(ref, base, d):
        if d == 1:
            return ref[pl.ds(base, BLK), :].astype(BF16)
        return ref[pl.ds(base, BLK, stride=d), :].astype(BF16)

    def store(g, base, d, acc, m, s):
        idx = pl.ds(base, BLK, stride=d)
        acc_s[g - 1, idx, :] = acc
        m_s[g - 1, idx, :] = jnp.broadcast_to(m, (BLK, HEAD_DIM))
        s_s[g - 1, idx, :] = jnp.broadcast_to(s, (BLK, HEAD_DIM))

    def finish(base, acc, m, s):
        sl = pl.ds(base, BLK)
        m1, m2 = m_s[0, sl, :], m_s[1, sl, :]
        mm = jnp.maximum(jnp.maximum(m, m1), m2)
        w0, w1, w2 = jnp.exp(m - mm), jnp.exp(m1 - mm), jnp.exp(m2 - mm)
        num = w0 * acc + w1 * acc_s[0, sl, :] + w2 * acc_s[1, sl, :]
        den = w0 * s + w1 * s_s[0, sl, :] + w2 * s_s[1, sl, :]
        o_ref[sl, :] = ((num / den) * _silu(z_ref[sl, :])).astype(o_ref.dtype)

    def block(g, refs, d, base, pbase, has_prev):
        q_ref, k_ref, v_ref = refs
        q = load(q_ref, base, d)
        lc = lax.dot_general(q, load(k_ref, base, d), nt, preferred_element_type=F32)
        lc = lc * ATT_SCALE + bias_ref[g, :, BLK:]
        if has_prev is None:
            return lc, None
        lp = lax.dot_general(q, load(k_ref, pbase, d), nt, preferred_element_type=F32)
        lp = lp * ATT_SCALE + bias_ref[g, :, :BLK]
        return lc, jnp.where(has_prev, lp, NEG)

    def softmax(lc, lp):
        if lp is None:
            m = jnp.max(lc, axis=-1, keepdims=True)
            pc = jnp.exp(lc - m)
            return pc, None, m, jnp.sum(pc, axis=-1, keepdims=True)
        m = jnp.max(jnp.maximum(lc, lp), axis=-1, keepdims=True)
        pc = jnp.exp(lc - m)
        pp = jnp.exp(lp - m)
        return pc, pp, m, jnp.sum(pc + pp, axis=-1, keepdims=True)

    def weighted(refs, d, base, pbase, pc, pp):
        v_ref = refs[2]
        acc = jnp.dot(pc.astype(BF16), load(v_ref, base, d), preferred_element_type=F32)
        if pp is not None:
            acc = acc + jnp.dot(pp.astype(BF16), load(v_ref, pbase, d), preferred_element_type=F32)
        return acc

    groups = ((q0, k0, v0), (q1, k1, v1), (q2, k2, v2))

    def run_blocks(g, d, specs):
        refs = groups[g]
        logits = [block(g, refs, d, base, pbase, has_prev) for base, pbase, has_prev in specs]
        probs = [softmax(lc, lp) for lc, lp in logits]
        accs = [weighted(refs, d, base, pbase, pc, pp)
                for (base, pbase, _), (pc, pp, _, _) in zip(specs, probs)]
        for (base, _, _), acc, (_, _, m, s) in zip(specs, accs, probs):
            if d == 1:
                finish(base, acc, m, s)
            else:
                store(g, base, d, acc, m, s)

    assert ATT_GROUPS[0][1] == 1 and all(d > 1 for _, d in ATT_GROUPS[1:])
    for g in reversed(range(N_GROUPS)):
        d = ATT_GROUPS[g][1]
        span = BLK * d
        n_blocks = SEQ // span
        if n_blocks == 1:
            def body_r(rr, carry, g=g, d=d):
                run_blocks(g, d, [(rr * ATT_ILP + u, None, None) for u in range(ATT_ILP)])
                return carry
            lax.fori_loop(0, d // ATT_ILP, body_r, 0)
        elif d == 1:
            def body_n(nn, carry, g=g, span=span):
                specs = []
                for u in range(ATT_ILP_MERGE):
                    n = nn * ATT_ILP_MERGE + u
                    specs.append((pl.multiple_of(n * span, BLK),
                                  pl.multiple_of(jnp.maximum(n - 1, 0) * span, BLK), n > 0))
                run_blocks(g, 1, specs)
                return carry
            lax.fori_loop(0, n_blocks // ATT_ILP_MERGE, body_n, 0)
        else:
            n_per_iter = ATT_ILP // d
            assert n_per_iter * d == ATT_ILP and n_blocks % n_per_iter == 0
            def body_n(nn, carry, g=g, d=d, span=span, n_per_iter=n_per_iter):
                specs = []
                for u in range(n_per_iter):
                    n = nn * n_per_iter + u
                    specs += [(n * span + r, jnp.maximum(n - 1, 0) * span + r, n > 0)
                              for r in range(d)]
                run_blocks(g, d, specs)
                return carry
            lax.fori_loop(0, n_blocks // n_per_iter, body_n, 0)


def _attn_prompt(proj3, bias_p):
    def head_spec(col0):
        return pl.BlockSpec((None, SEQ, HEAD_DIM),
                            lambda b, h, c=col0 // HEAD_DIM: (b, 0, c + h))

    in_specs = []
    for g in range(N_GROUPS):
        for col in (COL_Q, COL_K, COL_V):
            in_specs.append(head_spec(col + g * ATT_WIDTH))
    in_specs.append(head_spec(COL_ZATT))
    in_specs.append(pl.BlockSpec((N_GROUPS, None, BLK, 2 * BLK), lambda b, h: (0, h, 0, 0)))
    scratch = [pltpu.VMEM((N_GROUPS - 1, SEQ, HEAD_DIM), F32) for _ in range(3)]
    return pl.pallas_call(
        _attn_prompt_kernel,
        grid=(BATCH, H_PER_GROUP),
        in_specs=in_specs,
        out_specs=pl.BlockSpec((None, SEQ, HEAD_DIM), lambda b, h: (b, 0, h)),
        out_shape=jax.ShapeDtypeStruct((BATCH, SEQ, ATT_WIDTH), BF16),
        scratch_shapes=scratch,
        compiler_params=_cparams(2),
        name="attn_prompt",
    )(*([proj3] * 10), bias_p)


def _lookup(tbl, idx):
    idx = np.asarray(idx, np.int32)
    onehot = (jnp.asarray(idx.reshape(-1, 1)) == jnp.arange(tbl.shape[0], dtype=jnp.int32)[None, :])
    out = jnp.dot(onehot.astype(F32), tbl.astype(F32), precision=lax.Precision.HIGHEST)
    return out.reshape(idx.shape + tbl.shape[1:])


def _prompt_bias(rel_bias):
    qi = np.arange(BLK)[:, None]
    ki = np.arange(2 * BLK)[None, :]
    du = qi + BLK - ki
    band = (du >= 0) & (du <= BLK)
    out = []
    for g, (win, dil) in enumerate(ATT_GROUPS):
        assert win // dil == BLK
        tbl = rel_bias[:, g * H_PER_GROUP:(g + 1) * H_PER_GROUP]
        b = _lookup(tbl, _t5_bucket(np.clip(du, 0, None) * dil))
        b = jnp.where(band[:, :, None], b, NEG)
        out.append(jnp.transpose(b, (2, 0, 1)))
    return jnp.stack(out)


def _softplus(x):
    return jnp.maximum(x, 0.0) + jnp.log1p(jnp.exp(-jnp.abs(x)))


def _lru_gates(xc, wr_ref, br, wi_ref, bi, nsp, n_blocks):
    k = (-0.5 * LRU_C) * nsp
    half_br = 0.5 * br
    half_bi = 0.5 * bi
    a_parts, u_parts = [], []
    for n in range(n_blocks):
        sl = slice(n * LRU_BLOCK, (n + 1) * LRU_BLOCK)
        xn = xc[:, sl]
        xb = xn.astype(BF16)
        wr_half = (0.5 * wr_ref[n]).astype(BF16)
        wi_half = (0.5 * wi_ref[n]).astype(BF16)
        t_r = jnp.tanh(jnp.dot(xb, wr_half, preferred_element_type=F32) + half_br[:, sl])
        t_i = jnp.tanh(jnp.dot(xb, wi_half, preferred_element_type=F32) + half_bi[:, sl])
        log_a = k[:, sl] * t_r + k[:, sl]
        a = jnp.exp(log_a)
        one_minus_a2 = jnp.tanh(log_a) * (-1.0 - a * a)
        root = jnp.where(one_minus_a2 > 0.0, one_minus_a2 * lax.rsqrt(one_minus_a2), 0.0)
        a_parts.append(a)
        u_parts.append(root * ((0.5 * xn) * (t_i + 1.0)))
    return a_parts, u_parts


def _scan8(a, u, rows):
    for s in (1, 2, 4):
        keep = rows >= s
        a_sh = pltpu.roll(a, s, 0)
        u_sh = pltpu.roll(u, s, 0)
        u = jnp.where(keep, a * u_sh + u, u)
        a = jnp.where(keep, a * a_sh, a)
    return a, u


def _lru_prompt_kernel(x_ref, z_ref, cw_ref, cb_ref, wr_ref, br_ref, wi_ref, bi_ref, lam_ref,
                       o_ref, conv_ref, hl_ref, a_s, u_s):
    T, C = x_ref.shape
    n_blocks = C // LRU_BLOCK
    R = 256
    pad = 8

    conv_ref[...] = x_ref[T - (CONV_WIDTH - 1):T, :]

    nsp = _softplus(-lam_ref[...])
    br = br_ref[...]
    bi = bi_ref[...]
    cb = cb_ref[...]

    def gates(c, carry, first=False):
        if first:
            r0 = 0
            xw = jnp.concatenate([jnp.zeros((pad, C), F32), x_ref[0:R, :]], axis=0)
        else:
            r0 = pl.multiple_of(c * R, R)
            xw = x_ref[pl.ds(pl.multiple_of(r0 - pad, pad), R + pad), :]
        xc = xw[pad:] * cw_ref[CONV_WIDTH - 1:CONV_WIDTH, :]
        for tap in range(CONV_WIDTH - 1):
            shifted = pltpu.roll(xw, CONV_WIDTH - 1 - tap, 0)[pad:]
            xc = xc + shifted * cw_ref[tap:tap + 1, :]
        xc = xc + cb
        a_parts, u_parts = _lru_gates(xc, wr_ref, br, wi_ref, bi, nsp, n_blocks)
        for n in range(n_blocks):
            sl = slice(n * LRU_BLOCK, (n + 1) * LRU_BLOCK)
            a_s[pl.ds(r0, R), sl] = a_parts[n]
            u_s[pl.ds(r0, R), sl] = u_parts[n]
        return carry

    gates(0, 0, first=True)
    lax.fori_loop(1, T // R, gates, 0)

    rows = lax.broadcasted_iota(jnp.int32, (8, C), 0)

    def scan(c, h_prev):
        t0 = pl.multiple_of(c * 16, 16)
        hs = []
        for half in range(2):
            sl = pl.ds(t0 + 8 * half, 8)
            a, u = _scan8(a_s[sl, :], u_s[sl, :], rows)
            h = a * h_prev + u
            hs.append(h)
            h_prev = h[7:8, :]
        h16 = jnp.concatenate(hs, axis=0)
        z = z_ref[pl.ds(t0, 16), :]
        o_ref[pl.ds(t0, 16), :] = (h16 * _silu(z)).astype(o_ref.dtype)
        return h_prev

    h_last = lax.fori_loop(0, T // 16, scan, jnp.zeros((1, C), F32))
    hl_ref[...] = h_last


def _lru_prompt(proj3, conv_w, conv_b, w_r, b_r, w_i, b_i, lam, layer):
    tc = 512
    nb = tc // LRU_BLOCK
    n_ct = LRU_WIDTH // tc

    def col_spec(col0):
        return pl.BlockSpec((None, SEQ, tc), lambda b, c, c0=col0 // tc: (b, 0, c0 + c))

    vec_spec = pl.BlockSpec((None, 1, tc), lambda b, c: (layer, 0, c))
    w_spec = pl.BlockSpec((None, nb, LRU_BLOCK, LRU_BLOCK), lambda b, c: (layer, c, 0, 0))
    return pl.pallas_call(
        _lru_prompt_kernel,
        grid=(BATCH, n_ct),
        in_specs=[
            col_spec(COL_XLRU), col_spec(COL_ZLRU),
            pl.BlockSpec((None, CONV_WIDTH, tc), lambda b, c: (layer, 0, c)),
            vec_spec, w_spec, vec_spec, w_spec, vec_spec, vec_spec,
        ],
        out_specs=[
            pl.BlockSpec((None, SEQ, tc), lambda b, c: (b, 0, c)),
            pl.BlockSpec((None, CONV_WIDTH - 1, tc), lambda b, c: (b, 0, c)),
            pl.BlockSpec((None, 1, tc), lambda b, c: (b, 0, c)),
        ],
        out_shape=[
            jax.ShapeDtypeStruct((BATCH, SEQ, LRU_WIDTH), BF16),
            jax.ShapeDtypeStruct((BATCH, CONV_WIDTH - 1, LRU_WIDTH), F32),
            jax.ShapeDtypeStruct((BATCH, 1, LRU_WIDTH), F32),
        ],
        scratch_shapes=[
            pltpu.VMEM((SEQ, tc), F32),
            pltpu.VMEM((SEQ, tc), F32),
        ],
        compiler_params=_cparams(2),
        name="lru_prompt",
    )(proj3, proj3, conv_w, conv_b, w_r, b_r, w_i, b_i, lam)


def _mix_kernel(att_ref, lru_ref, ga_ref, gl_ref, wpa_ref, wpb_ref, o_ref):
    y_att = jnp.dot(att_ref[...], wpa_ref[...], preferred_element_type=F32)
    y_lru = jnp.dot(lru_ref[...], wpb_ref[...], preferred_element_type=F32)
    merged = _sigmoid(ga_ref[...]) * y_att + _sigmoid(gl_ref[...]) * y_lru
    o_ref[...] = merged.astype(o_ref.dtype)


def _mix(att, lru, proj, wpa, wpb, layer, tm):
    m_rows = att.shape[0]
    return pl.pallas_call(
        _mix_kernel,
        grid=(m_rows // tm,),
        in_specs=[
            pl.BlockSpec((tm, ATT_WIDTH), lambda i: (i, 0)),
            pl.BlockSpec((tm, LRU_WIDTH), lambda i: (i, 0)),
            pl.BlockSpec((tm, D_MODEL), lambda i: (i, COL_GATT // D_MODEL)),
            pl.BlockSpec((tm, D_MODEL), lambda i: (i, COL_GLRU // D_MODEL)),
            pl.BlockSpec((None, ATT_WIDTH, D_MODEL), lambda i: (layer, 0, 0),
                         pipeline_mode=pl.Buffered(1)),
            pl.BlockSpec((None, LRU_WIDTH, D_MODEL), lambda i: (layer, 0, 0),
                         pipeline_mode=pl.Buffered(1)),
        ],
        out_specs=pl.BlockSpec((tm, D_MODEL), lambda i: (i, 0)),
        out_shape=jax.ShapeDtypeStruct((m_rows, D_MODEL), BF16),
        compiler_params=_cparams(1),
        name="branch_mix",
    )(att, lru, proj, proj, wpa, wpb)


def _residual_kernel(m_ref, w_ref, x_ref, gate_ref, g_ref, *rest, last):
    out = jnp.dot(m_ref[...], w_ref[...].astype(BF16), preferred_element_type=F32)
    x_new = x_ref[...] + gate_ref[...] * out
    y = x_new * lax.rsqrt(jnp.mean(x_new * x_new, axis=-1, keepdims=True) + RMS_EPS)
    y = y * g_ref[...]
    if last:
        (y_ref,) = rest
        y_ref[...] = y
    else:
        sc_ref, sh_ref, xo_ref, ho_ref = rest
        xo_ref[...] = x_new
        ho_ref[...] = (y * (1.0 + sc_ref[...]) + sh_ref[...]).astype(ho_ref.dtype)


def _residual_prompt(merged, w_out, x, gate, g, scale, shift, layer):
    tm = 512
    last = scale is None
    row_spec = pl.BlockSpec((None, tm, D_MODEL), lambda b, i: (b, i, 0))
    vec_spec = pl.BlockSpec((None, 1, D_MODEL), lambda b, i: (b, 0, 0))
    in_specs = [
        row_spec,
        pl.BlockSpec((None, D_MODEL, D_MODEL), lambda b, i: (layer, 0, 0),
                     pipeline_mode=pl.Buffered(1)),
        row_spec,
        vec_spec,
        pl.BlockSpec((1, D_MODEL), lambda b, i: (0, 0)),
    ]
    args = [merged, w_out, x, gate, g.reshape(1, D_MODEL)]
    if last:
        out_specs = row_spec
        out_shape = jax.ShapeDtypeStruct(x.shape, F32)
    else:
        in_specs += [vec_spec, vec_spec]
        args += [scale, shift]
        out_specs = [row_spec, row_spec]
        out_shape = [jax.ShapeDtypeStruct(x.shape, F32), jax.ShapeDtypeStruct(x.shape, BF16)]
    return pl.pallas_call(
        functools.partial(_residual_kernel, last=last),
        grid=(BATCH, SEQ // tm),
        in_specs=in_specs,
        out_specs=out_specs,
        out_shape=out_shape,
        compiler_params=_cparams(2),
        name="residual_prompt",
    )(*args)


def _residual_sample(merged, w_out, x, gate_rows, g, scale_rows, shift_rows, layer):
    last = scale_rows is None
    full = pl.BlockSpec((N_SAMPLE_ROWS, D_MODEL), lambda i: (0, 0))
    in_specs = [full, pl.BlockSpec((None, D_MODEL, D_MODEL), lambda i: (layer, 0, 0)), full, full,
                pl.BlockSpec((1, D_MODEL), lambda i: (0, 0))]
    args = [merged, w_out, x, gate_rows, g.reshape(1, D_MODEL)]
    if last:
        out_specs = full
        out_shape = jax.ShapeDtypeStruct(x.shape, F32)
    else:
        in_specs += [full, full]
        args += [scale_rows, shift_rows]
        out_specs = [full, full]
        out_shape = [jax.ShapeDtypeStruct(x.shape, F32), jax.ShapeDtypeStruct(x.shape, BF16)]
    return pl.pallas_call(
        functools.partial(_residual_kernel, last=last),
        grid=(1,),
        in_specs=in_specs,
        out_specs=out_specs,
        out_shape=out_shape,
        compiler_params=_cparams(1),
        name="residual_sample",
    )(*args)


CACHE_CHUNK_ROWS = 512
CACHE_SLOTS = 6
CACHE_LOOKAHEAD = 3


def _cache_update_kernel(c0, c1, c2, n0, n1, n2, o0, o1, o2, buf, in_sem, out_sem, new_sem):
    chunks = []
    new_copies = []
    for gi, (c, n, o) in enumerate(((c0, n0, o0), (c1, n1, o1), (c2, n2, o2))):
        keep = c.shape[2] - DEC_SEQ
        for l in range(DEPTH):
            new_copies.append(pltpu.make_async_copy(
                n.at[l], o.at[l, :, pl.ds(keep, DEC_SEQ)], new_sem.at[gi, l]))
            for b in range(DEC_BATCH):
                for r in range(0, keep, CACHE_CHUNK_ROWS):
                    rows = min(CACHE_CHUNK_ROWS, keep - r)
                    chunks.append((c.at[l, b, pl.ds(DEC_SEQ + r, rows)],
                                   o.at[l, b, pl.ds(r, rows)], rows))

    def read(i):
        src, _, rows = chunks[i]
        slot = i % CACHE_SLOTS
        return pltpu.make_async_copy(src, buf.at[slot, pl.ds(0, rows)], in_sem.at[slot])

    def write(i):
        _, dst, rows = chunks[i]
        slot = i % CACHE_SLOTS
        return pltpu.make_async_copy(buf.at[slot, pl.ds(0, rows)], dst, out_sem.at[slot])

    for cp in new_copies:
        cp.start()
    n_chunks = len(chunks)
    for i in range(n_chunks + CACHE_LOOKAHEAD):
        if i < n_chunks:
            if i >= CACHE_SLOTS:
                write(i - CACHE_SLOTS).wait()
            read(i).start()
        j = i - CACHE_LOOKAHEAD
        if j >= 0:
            read(j).wait()
            write(j).start()
    for j in range(max(n_chunks - CACHE_SLOTS, 0), n_chunks):
        write(j).wait()
    for cp in new_copies:
        cp.wait()


def _cache_update(caches, new_rows):
    any_spec = pl.BlockSpec(memory_space=pl.ANY)
    vmem_spec = pl.BlockSpec(memory_space=pltpu.VMEM)
    return pl.pallas_call(
        _cache_update_kernel,
        in_specs=[any_spec] * 3 + [vmem_spec] * 3,
        out_specs=[any_spec] * 3,
        out_shape=[jax.ShapeDtypeStruct(c.shape, c.dtype) for c in caches],
        scratch_shapes=[
            pltpu.VMEM((CACHE_SLOTS, CACHE_CHUNK_ROWS, 2, H_PER_GROUP, HEAD_DIM), F32),
            pltpu.SemaphoreType.DMA((CACHE_SLOTS,)),
            pltpu.SemaphoreType.DMA((CACHE_SLOTS,)),
            pltpu.SemaphoreType.DMA((N_GROUPS, DEPTH)),
        ],
        compiler_params=pltpu.CompilerParams(vmem_limit_bytes=VMEM_LIMIT),
        name="cache_update",
    )(*caches, *new_rows)


def _attn_sample_kernel(q_ref, k_ref, v_ref, z_ref, c0_ref, c1_ref, c2_ref, bc_ref, bn_ref,
                        o_ref, kv0_ref, kv1_ref, kv2_ref):
    hp = H_PER_GROUP
    kv_refs = (kv0_ref, kv1_ref, kv2_ref)
    k_new, v_new = [], []
    for g in range(N_GROUPS):
        kn = k_ref[:, g * hp:(g + 1) * hp, :]
        vn = v_ref[:, g * hp:(g + 1) * hp, :]
        kv_refs[g][:, 0] = kn
        kv_refs[g][:, 1] = vn
        k_new.append(kn)
        v_new.append(vn)

    for s in range(DEC_SEQ):
        accs, ms, ss = [], [], []
        for g in range(N_GROUPS):
            qg = q_ref[s, g * hp:(g + 1) * hp, :]
            if g == 0:
                kc, vc = c0_ref[:, 0], c0_ref[:, 1]
            elif g == 1:
                kc, vc = c1_ref[:, s, 0], c1_ref[:, s, 1]
            else:
                kc, vc = c2_ref[:, s, 0], c2_ref[:, s, 1]
            lc = jnp.sum(kc * qg[None], axis=-1, keepdims=True) * ATT_SCALE + bc_ref[g, s]
            ln = jnp.sum(k_new[g] * qg[None], axis=-1, keepdims=True) * ATT_SCALE + bn_ref[g, s]
            m = jnp.maximum(jnp.max(lc, axis=0), jnp.max(ln, axis=0))
            pc = jnp.exp(lc - m[None])
            pn = jnp.exp(ln - m[None])
            ss.append(jnp.sum(pc, axis=0) + jnp.sum(pn, axis=0))
            accs.append(jnp.sum(pc * vc, axis=0) + jnp.sum(pn * v_new[g], axis=0))
            ms.append(m)
        mm = jnp.maximum(jnp.maximum(ms[0], ms[1]), ms[2])
        ws = [jnp.exp(m - mm) for m in ms]
        num = ws[0] * accs[0] + ws[1] * accs[1] + ws[2] * accs[2]
        den = ws[0] * ss[0] + ws[1] * ss[1] + ws[2] * ss[2]
        o_ref[s] = (num / den) * _silu(z_ref[s])


def _attn_sample(proj_s4, caches, bias_c, bias_n, layer):
    c0, c1, c2 = caches
    hp = H_PER_GROUP

    def head_spec(col0, n_heads):
        return pl.BlockSpec((None, DEC_SEQ, n_heads, HEAD_DIM),
                            lambda b, c=col0 // (HEAD_DIM * n_heads): (b, 0, c, 0))

    in_specs = [
        head_spec(COL_Q, N_ATT_HEADS), head_spec(COL_K, N_ATT_HEADS), head_spec(COL_V, N_ATT_HEADS),
        head_spec(COL_ZATT, hp),
        pl.BlockSpec((None, None, BLK, 2, hp, HEAD_DIM), lambda b: (layer, b, 0, 0, 0, 0)),
        pl.BlockSpec((None, None, BLK, DEC_SEQ, 2, hp, HEAD_DIM), lambda b: (layer, b, 0, 0, 0, 0, 0)),
        pl.BlockSpec((None, None, BLK, DEC_SEQ, 2, hp, HEAD_DIM), lambda b: (layer, b, 0, 0, 0, 0, 0)),
        pl.BlockSpec(bias_c.shape, lambda b: (0,) * 5),
        pl.BlockSpec(bias_n.shape, lambda b: (0,) * 5),
    ]
    out_specs = [pl.BlockSpec((None, DEC_SEQ, hp, HEAD_DIM), lambda b: (b, 0, 0, 0))]
    out_shape = [jax.ShapeDtypeStruct((DEC_BATCH, DEC_SEQ, hp, HEAD_DIM), F32)]
    for _ in range(N_GROUPS):
        out_specs.append(pl.BlockSpec((None, DEC_SEQ, 2, hp, HEAD_DIM), lambda b: (b, 0, 0, 0, 0)))
        out_shape.append(jax.ShapeDtypeStruct((DEC_BATCH, DEC_SEQ, 2, hp, HEAD_DIM), F32))
    return pl.pallas_call(
        _attn_sample_kernel,
        grid=(DEC_BATCH,),
        in_specs=in_specs,
        out_specs=out_specs,
        out_shape=out_shape,
        compiler_params=_cparams(1),
        name="attn_sample",
    )(proj_s4, proj_s4, proj_s4, proj_s4, c0, c1, c2, bias_c, bias_n)


def _sample_bias(rel_bias):
    s_idx = np.arange(DEC_SEQ)
    bc, bn = [], []
    for g, (win, dil) in enumerate(ATT_GROUPS):
        tbl = rel_bias[:, g * H_PER_GROUP:(g + 1) * H_PER_GROUP].astype(F32)
        key = np.arange(BLK)
        if dil == 1:
            j = BLK + s_idx[:, None] - key[None, :]
            valid = j <= BLK
        else:
            j = np.broadcast_to(BLK - key[None, :], (DEC_SEQ, BLK))
            valid = np.ones_like(j, dtype=bool)
        b = _lookup(tbl, _t5_bucket(np.clip(j, 0, None) * dil))
        bc.append(jnp.where(valid[:, :, None], b, NEG))
        jn = s_idx[:, None] - s_idx[None, :]
        valid_n = (jn >= 0) & (jn * dil <= win) & ((jn == 0) | (dil == 1))
        b = _lookup(tbl, _t5_bucket(np.clip(jn, 0, None) * dil))
        bn.append(jnp.where(valid_n[:, :, None], b, NEG))
    bc = jnp.stack(bc)
    bn = jnp.stack(bn)
    bc = jnp.broadcast_to(bc[..., None], bc.shape + (HEAD_DIM,))
    bn = jnp.broadcast_to(bn[..., None], bn.shape + (HEAD_DIM,))
    return bc, bn


def _lru_sample_kernel(x_ref, z_ref, cs_ref, h0_ref, cw_ref, cb_ref, wr_ref, br_ref, wi_ref, bi_ref,
                       lam_ref, o_ref, conv_ref, hl_ref):
    S = DEC_SEQ
    xp = [cs_ref[t] for t in range(CONV_WIDTH - 1)] + [x_ref[t] for t in range(S)]
    xc = []
    for t in range(S):
        y = xp[t] * cw_ref[0:1, :]
        for tap in range(1, CONV_WIDTH):
            y = y + xp[t + tap] * cw_ref[tap:tap + 1, :]
        xc.append(y + cb_ref[...])
    xcat = jnp.concatenate(xc, axis=0)
    nsp = _softplus(-lam_ref[...])
    a_parts, u_parts = _lru_gates(xcat, wr_ref, br_ref[...], wi_ref, bi_ref[...], nsp, LRU_BLOCKS)
    a = jnp.concatenate(a_parts, axis=1)
    u = jnp.concatenate(u_parts, axis=1)
    nb = DEC_BATCH
    h = h0_ref[...]
    for t in range(S):
        h = a[t * nb:(t + 1) * nb] * h + u[t * nb:(t + 1) * nb]
        o_ref[t] = h * _silu(z_ref[t])
    for t in range(CONV_WIDTH - 1):
        conv_ref[t] = xp[S + t]
    hl_ref[...] = h


def _lru_sample(x_t, z_t, conv_t, h0, conv_w, conv_b, w_r, b_r, w_i, b_i, lam, layer):
    C = LRU_WIDTH

    def full(shape):
        return pl.BlockSpec(shape, lambda i: (0,) * len(shape))

    vec_spec = pl.BlockSpec((None, 1, C), lambda i: (layer, 0, 0))
    w_spec = pl.BlockSpec((None, LRU_BLOCKS, LRU_BLOCK, LRU_BLOCK), lambda i: (layer, 0, 0, 0))
    return pl.pallas_call(
        _lru_sample_kernel,
        grid=(1,),
        in_specs=[
            full((DEC_SEQ, DEC_BATCH, C)), full((DEC_SEQ, DEC_BATCH, C)),
            full((CONV_WIDTH - 1, DEC_BATCH, C)), full((DEC_BATCH, C)),
            pl.BlockSpec((None, CONV_WIDTH, C), lambda i: (layer, 0, 0)),
            vec_spec, w_spec, vec_spec, w_spec, vec_spec, vec_spec,
        ],
        out_specs=[full((DEC_SEQ, DEC_BATCH, C)), full((CONV_WIDTH - 1, DEC_BATCH, C)),
                   full((DEC_BATCH, C))],
        out_shape=[
            jax.ShapeDtypeStruct((DEC_SEQ, DEC_BATCH, C), F32),
            jax.ShapeDtypeStruct((CONV_WIDTH - 1, DEC_BATCH, C), F32),
            jax.ShapeDtypeStruct((DEC_BATCH, C), F32),
        ],
        compiler_params=_cparams(1),
        name="lru_sample",
    )(x_t, z_t, conv_t, h0, conv_w, conv_b, w_r, b_r, w_i, b_i, lam)


def kernel(x_prompt, x_sample, c_prompt, c_sample, cache_kv_g0, cache_kv_g1, cache_kv_g2, state_conv, state_h, rel_bias, w_ada, b_ada, norm_g, w_in, conv_w, conv_b, w_r, b_r, w_i, b_i, lam, w_pa, w_pb, w_out, final_g):
    L, B, T, D = DEPTH, BATCH, SEQ, D_MODEL
    Bd, S = DEC_BATCH, DEC_SEQ

    c_all = jnp.concatenate(
        [c_prompt, c_sample, jnp.zeros((MOD_ROWS - B - Bd, D), F32)], axis=0)
    mod = _modulation(c_all, w_ada, b_ada).reshape(L, MOD_ROWS, 3, D)
    mod_p = mod[:, :B]
    mod_s = jnp.repeat(mod[:, B:B + Bd], S, axis=1)

    wpa_bf = w_pa.astype(BF16)
    wpb_bf = w_pb.astype(BF16)
    conv_b3 = conv_b.reshape(L, 1, LRU_WIDTH)
    b_r3 = b_r.reshape(L, 1, LRU_WIDTH)
    b_i3 = b_i.reshape(L, 1, LRU_WIDTH)
    lam3 = lam.reshape(L, 1, LRU_WIDTH)

    bias_p = _prompt_bias(rel_bias)
    bias_c, bias_n = _sample_bias(rel_bias)

    cache_views = (
        cache_kv_g0,
        cache_kv_g1.reshape(L, Bd, BLK, 4, 2, H_PER_GROUP, HEAD_DIM),
        cache_kv_g2.reshape(L, Bd, BLK, 16, 2, H_PER_GROUP, HEAD_DIM),
    )

    xp = x_prompt
    xs = x_sample.reshape(Bd * S, D)
    kv_p = None
    kv_new = ([], [], [])
    conv_p, h_p, conv_s, h_s = [], [], [], []

    def mods(l):
        shift_p, scale_p, gate_p = (mod_p[l, :, i][:, None, :] for i in range(3))
        shift_s, scale_s, gate_s = (mod_s[l, :, i] for i in range(3))
        return (shift_p, scale_p, gate_p), (shift_s, scale_s, gate_s)

    (shift_p, scale_p, gate_p), (shift_s, scale_s, gate_s) = mods(0)
    hp_ = _norm_prompt(xp, norm_g[0], scale_p, shift_p)
    hs_ = _norm_sample(xs, norm_g[0], scale_s, shift_s)
    for l in range(L):
        proj_p, proj_s, *kv_p = _in_proj(hp_.reshape(B * T, D), hs_, w_in, kv_p, l)
        proj3 = proj_p.reshape(B, T, IN_COLS)

        att_p = _attn_prompt(proj3, bias_p)
        lru_p, cv, hl = _lru_prompt(proj3, conv_w, conv_b3, w_r, b_r3, w_i, b_i3, lam3, l)
        merged_p = _mix(att_p.reshape(B * T, ATT_WIDTH), lru_p.reshape(B * T, LRU_WIDTH),
                        proj_p, wpa_bf, wpb_bf, l, 512)
        last = l == L - 1
        if not last:
            (shift_pn, scale_pn, gate_pn), (shift_sn, scale_sn, gate_sn) = mods(l + 1)
            xp, hp_ = _residual_prompt(merged_p.reshape(B, T, D), w_out, xp, gate_p,
                                       norm_g[l + 1], scale_pn, shift_pn, l)
        else:
            y_prompt = _residual_prompt(merged_p.reshape(B, T, D), w_out, xp, gate_p,
                                        final_g, None, None, l)
        conv_p.append(cv)
        h_p.append(hl.reshape(B, LRU_WIDTH))

        proj_s4 = proj_s.reshape(Bd, S, IN_COLS // HEAD_DIM, HEAD_DIM)
        outs = _attn_sample(proj_s4, cache_views, bias_c, bias_n, l)
        att_s = outs[0]
        for g in range(N_GROUPS):
            kv_new[g].append(outs[1 + g])
        ps3 = proj_s.reshape(Bd, S, IN_COLS)
        x_t = jnp.transpose(ps3[:, :, COL_XLRU:COL_XLRU + LRU_WIDTH], (1, 0, 2))
        z_t = jnp.transpose(ps3[:, :, COL_ZLRU:COL_ZLRU + LRU_WIDTH], (1, 0, 2))
        conv_t = jnp.transpose(state_conv[l], (1, 0, 2))
        lru_t, cv_t, hl_s = _lru_sample(x_t, z_t, conv_t, state_h[l], conv_w, conv_b3,
                                        w_r, b_r3, w_i, b_i3, lam3, l)
        lru_s = jnp.transpose(lru_t, (1, 0, 2)).reshape(Bd * S, LRU_WIDTH).astype(BF16)
        merged_s = _mix(att_s.reshape(Bd * S, ATT_WIDTH).astype(BF16), lru_s, proj_s,
                        wpa_bf, wpb_bf, l, Bd * S)
        if not last:
            xs, hs_ = _residual_sample(merged_s, w_out, xs, gate_s,
                                       norm_g[l + 1], scale_sn, shift_sn, l)
            gate_p, gate_s = gate_pn, gate_sn
        else:
            y_sample = _residual_sample(merged_s, w_out, xs, gate_s,
                                        final_g, None, None, l).reshape(Bd, S, D)
        conv_s.append(jnp.transpose(cv_t, (1, 0, 2)))
        h_s.append(hl_s)

    kv_s = _cache_update((cache_kv_g0, cache_kv_g1, cache_kv_g2),
                         tuple(jnp.stack(rows) for rows in kv_new))
    return (y_prompt, y_sample,
            kv_p[0], kv_p[1], kv_p[2],
            jnp.stack(conv_p), jnp.stack(h_p),
            kv_s[0], kv_s[1], kv_s[2],
            jnp.stack(conv_s), jnp.stack(h_s))
```

```python
import functools
import math

import jax
import jax.numpy as jnp
import numpy as np
from jax import lax
from jax.experimental import pallas as pl
from jax.experimental.pallas import tpu as pltpu

D_MODEL = 2048
BATCH = 4
SEQ = 2048
DEPTH = 4
DEC_BATCH = 8
DEC_SEQ = 4
HEAD_DIM = 128
H_PER_GROUP = 8
ATT_GROUPS = ((128, 1), (512, 4), (2048, 16))
N_GROUPS = 3
N_ATT_HEADS = N_GROUPS * H_PER_GROUP
QKV_WIDTH = N_ATT_HEADS * HEAD_DIM
ATT_WIDTH = H_PER_GROUP * HEAD_DIM
N_BUCKETS = 32
REL_MAX_DIST = 2048
BLK = 128
LRU_WIDTH = D_MODEL
LRU_BLOCKS = 16
LRU_BLOCK = LRU_WIDTH // LRU_BLOCKS
CONV_WIDTH = 4
LRU_C = 8.0
IN_COLS = 3 * QKV_WIDTH + ATT_WIDTH + 2 * LRU_WIDTH + 2 * D_MODEL
RMS_EPS = 1e-6
NEG = -1e30
ATT_SCALE = HEAD_DIM ** -0.5

COL_Q = 0
COL_K = QKV_WIDTH
COL_V = 2 * QKV_WIDTH
COL_ZATT = 3 * QKV_WIDTH
COL_XLRU = COL_ZATT + ATT_WIDTH
COL_ZLRU = COL_XLRU + LRU_WIDTH
COL_GATT = COL_ZLRU + LRU_WIDTH
COL_GLRU = COL_GATT + D_MODEL

N_PROMPT_ROWS = BATCH * SEQ
N_SAMPLE_ROWS = DEC_BATCH * DEC_SEQ
MOD_ROWS = 16

VMEM_LIMIT = 52 * 1024 * 1024
ATT_ILP = 16
ATT_ILP_MERGE = 16

F32 = jnp.float32
BF16 = jnp.bfloat16


def _cparams(n_grid_dims, vmem_limit=VMEM_LIMIT):
    return pltpu.CompilerParams(
        dimension_semantics=("arbitrary",) * n_grid_dims,
        vmem_limit_bytes=vmem_limit)


def _sigmoid(x):
    return 0.5 * jnp.tanh(0.5 * x) + 0.5


def _silu(x):
    half = 0.5 * x
    return half * (jnp.tanh(half) + 1.0)


def _t5_bucket(dist):
    dist = np.asarray(dist).astype(np.int32)
    max_exact = N_BUCKETS // 2
    safe = np.maximum(dist, 1).astype(np.float32)
    large = max_exact + (np.log(safe / max_exact) / np.float32(math.log(REL_MAX_DIST / max_exact))
                         * (N_BUCKETS - max_exact)).astype(np.int32)
    large = np.minimum(large, N_BUCKETS - 1)
    return np.where(dist < max_exact, dist, large).astype(np.int32)


def _mod_kernel(c_ref, w_ref, b_ref, o_ref):
    c = _silu(c_ref[...])
    o_ref[...] = jnp.dot(c.astype(BF16), w_ref[...].astype(BF16),
                         preferred_element_type=F32) + b_ref[...]


def _modulation(c_all, w_ada, b_ada):
    tn = 1024
    n_cols = 3 * D_MODEL
    return pl.pallas_call(
        _mod_kernel,
        grid=(DEPTH, n_cols // tn),
        in_specs=[
            pl.BlockSpec((MOD_ROWS, D_MODEL), lambda l, j: (0, 0)),
            pl.BlockSpec((None, D_MODEL, tn), lambda l, j: (l, 0, j)),
            pl.BlockSpec((None, 1, tn), lambda l, j: (l, 0, j)),
        ],
        out_specs=pl.BlockSpec((None, MOD_ROWS, tn), lambda l, j: (l, 0, j)),
        out_shape=jax.ShapeDtypeStruct((DEPTH, MOD_ROWS, n_cols), F32),
        compiler_params=_cparams(2),
        name="adaln_mod",
    )(c_all, w_ada, b_ada.reshape(DEPTH, 1, n_cols))


def _norm_kernel(x_ref, g_ref, sc_ref, sh_ref, o_ref):
    x = x_ref[...]
    y = x * lax.rsqrt(jnp.mean(x * x, axis=-1, keepdims=True) + RMS_EPS)
    y = y * g_ref[...]
    o_ref[...] = (y * (1.0 + sc_ref[...]) + sh_ref[...]).astype(o_ref.dtype)


def _norm_prompt(x, g, scale, shift):
    tm = 512
    row_spec = pl.BlockSpec((None, tm, D_MODEL), lambda b, i: (b, i, 0))
    vec_spec = pl.BlockSpec((None, 1, D_MODEL), lambda b, i: (b, 0, 0))
    return pl.pallas_call(
        _norm_kernel,
        grid=(BATCH, SEQ // tm),
        in_specs=[row_spec, pl.BlockSpec((1, D_MODEL), lambda b, i: (0, 0)), vec_spec, vec_spec],
        out_specs=row_spec,
        out_shape=jax.ShapeDtypeStruct(x.shape, BF16),
        compiler_params=_cparams(2),
        name="rmsnorm_prompt",
    )(x, g.reshape(1, D_MODEL), scale, shift)


def _norm_sample(x, g, scale_rows, shift_rows):
    return pl.pallas_call(
        _norm_kernel,
        out_shape=jax.ShapeDtypeStruct(x.shape, BF16),
        name="rmsnorm_sample",
    )(x, g.reshape(1, D_MODEL), scale_rows, shift_rows)


IN_PROJ_VMEM_LIMIT = 58 * 1024 * 1024
IN_TM = 1024
IN_TN = ATT_WIDTH
IN_TILES_PER_SEQ = SEQ // IN_TM
N_ROW_TILES = N_PROMPT_ROWS // IN_TM
K_TILE0 = COL_K // IN_TN
V_TILE0 = COL_V // IN_TN


def _kv_rows(g):
    w = min(ATT_GROUPS[g][0], SEQ)
    assert w % IN_TM == 0 or IN_TM % w == 0
    return min(w, IN_TM)


def _kv_blocks_per_layer(g):
    n_wblk = min(ATT_GROUPS[g][0], SEQ) // _kv_rows(g)
    return BATCH * n_wblk * 2, n_wblk


def _in_proj_kernel(h_ref, hs_ref, w_ref, *rest, creates_kv):
    if creates_kv:
        o_ref, os_ref, kv0_ref, kv1_ref, kv2_ref, wbf_ref = rest
    else:
        o_ref, os_ref, kv0_ref, kv1_ref, kv2_ref, wbf_ref = rest[N_GROUPS:]
    j = pl.program_id(0)
    i = pl.program_id(1)

    @pl.when(i == 0)
    def _():
        wb = w_ref[...].astype(BF16)
        wbf_ref[...] = wb
        o_ref[...] = jnp.dot(h_ref[...], wb, preferred_element_type=F32)
        os_ref[...] = jnp.dot(hs_ref[...], wb, preferred_element_type=F32)

    @pl.when(i != 0)
    def _():
        o_ref[...] = jnp.dot(h_ref[...], wbf_ref[...], preferred_element_type=F32)

    for g, kv_ref in enumerate((kv0_ref, kv1_ref, kv2_ref)):
        rows = _kv_rows(g)
        is_kv = (j == K_TILE0 + g) | (j == V_TILE0 + g)
        if min(ATT_GROUPS[g][0], SEQ) < SEQ:
            is_kv = is_kv & (i % IN_TILES_PER_SEQ == IN_TILES_PER_SEQ - 1)

        @pl.when(is_kv)
        def _(kv_ref=kv_ref, rows=rows):
            flat = kv_ref.reshape(rows * H_PER_GROUP, HEAD_DIM)
            for h in range(H_PER_GROUP):
                flat[pl.ds(h, rows, stride=H_PER_GROUP), :] = (
                    o_ref[IN_TM - rows:, h * HEAD_DIM:(h + 1) * HEAD_DIM])

        if creates_kv:
            per_layer, _ = _kv_blocks_per_layer(g)
            t = (j - (V_TILE0 + g) - 1) * N_ROW_TILES + i

            @pl.when((j > V_TILE0 + g) & (t < (DEPTH - 1) * per_layer))
            def _(kv_ref=kv_ref):
                kv_ref[...] = jnp.zeros(kv_ref.shape, kv_ref.dtype)


def _kv_index_map(g, layer, creates_kv):
    w = min(ATT_GROUPS[g][0], SEQ)
    full = w >= SEQ
    per_layer, n_wblk = _kv_blocks_per_layer(g)

    def index_map(j, i):
        k_col = K_TILE0 + g
        v_col = V_TILE0 + g
        c = jnp.where(j >= v_col, 1, 0)
        in_col = (j == k_col) | (j == v_col)
        if full:
            ii = jnp.where(in_col, i, jnp.where(j < k_col, 0, N_ROW_TILES - 1))
            b, wblk = ii // IN_TILES_PER_SEQ, ii % IN_TILES_PER_SEQ
        else:
            b_col = jnp.maximum(i - (IN_TILES_PER_SEQ - 1), 0) // IN_TILES_PER_SEQ
            b = jnp.where(in_col, b_col, jnp.where(j < k_col, 0, BATCH - 1))
            wblk = 0
        if not creates_kv:
            return (layer, b, wblk, c, 0, 0)
        assert layer == 0 and (IN_COLS // IN_TN - v_col - 1) * N_ROW_TILES >= (DEPTH - 1) * per_layer
        f = jnp.clip((j - v_col - 1) * N_ROW_TILES + i, 0, (DEPTH - 1) * per_layer - 1)
        rest = f % per_layer
        fill = j > v_col
        return (jnp.where(fill, 1 + f // per_layer, 0),
                jnp.where(fill, rest // (2 * n_wblk), b),
                jnp.where(fill, (rest // 2) % n_wblk, wblk),
                jnp.where(fill, rest % 2, c), 0, 0)

    return index_map


def _in_proj(h_p, h_s, w_in, kv_stacks, layer):
    tm, tn = IN_TM, IN_TN
    creates_kv = kv_stacks is None
    if creates_kv:
        kv_shapes = [jax.ShapeDtypeStruct((DEPTH, BATCH, min(win, SEQ), 2, H_PER_GROUP, HEAD_DIM), F32)
                     for win, _ in ATT_GROUPS]
        kv_stacks = ()
    else:
        kv_shapes = [jax.ShapeDtypeStruct(kv.shape, kv.dtype) for kv in kv_stacks]
    out_specs = [
        pl.BlockSpec((tm, tn), lambda j, i: (i, j)),
        pl.BlockSpec((N_SAMPLE_ROWS, tn), lambda j, i: (0, j)),
    ]
    out_shape = [
        jax.ShapeDtypeStruct((N_PROMPT_ROWS, IN_COLS), F32),
        jax.ShapeDtypeStruct((N_SAMPLE_ROWS, IN_COLS), F32),
    ]
    for g, shape in enumerate(kv_shapes):
        out_specs.append(pl.BlockSpec((None, None, _kv_rows(g), None, H_PER_GROUP, HEAD_DIM),
                                      _kv_index_map(g, layer, creates_kv)))
        out_shape.append(shape)
    n_in = 3
    return pl.pallas_call(
        functools.partial(_in_proj_kernel, creates_kv=creates_kv),
        grid=(IN_COLS // tn, N_ROW_TILES),
        in_specs=[
            pl.BlockSpec((tm, D_MODEL), lambda j, i: (i, 0)),
            pl.BlockSpec((N_SAMPLE_ROWS, D_MODEL), lambda j, i: (0, 0)),
            pl.BlockSpec((None, D_MODEL, tn), lambda j, i: (layer, 0, j)),
        ] + [pl.BlockSpec(memory_space=pl.ANY)] * len(kv_stacks),
        out_specs=out_specs,
        out_shape=out_shape,
        input_output_aliases={n_in + g: 2 + g for g in range(len(kv_stacks))},
        scratch_shapes=[pltpu.VMEM((D_MODEL, tn), BF16)],
        compiler_params=_cparams(2, IN_PROJ_VMEM_LIMIT),
        name="in_proj",
    )(h_p, h_s, w_in, *kv_stacks)


def _attn_prompt_kernel(q0, k0, v0, q1, k1, v1, q2, k2, v2, z_ref, bias_ref, o_ref,
                        acc_s, m_s, s_s):
    nt = (((1,), (1,)), ((), ()))

    def load(ref, base, d):
        if d == 1:
            return ref[pl.ds(base, BLK), :].astype(BF16)
        return ref[pl.ds(base, BLK, stride=d), :].astype(BF16)

    def store(g, base, d, acc, m, s):
        idx = pl.ds(base, BLK, stride=d)
        acc_s[g - 1, idx, :] = acc
        m_s[g - 1, idx, :] = jnp.broadcast_to(m, (BLK, HEAD_DIM))
        s_s[g - 1, idx, :] = jnp.broadcast_to(s, (BLK, HEAD_DIM))

    def finish(base, acc, m, s):
        sl = pl.ds(base, BLK)
        m1, m2 = m_s[0, sl, :], m_s[1, sl, :]
        mm = jnp.maximum(jnp.maximum(m, m1), m2)
        w0, w1, w2 = jnp.exp(m - mm), jnp.exp(m1 - mm), jnp.exp(m2 - mm)
        num = w0 * acc + w1 * acc_s[0, sl, :] + w2 * acc_s[1, sl, :]
        den = w0 * s + w1 * s_s[0, sl, :] + w2 * s_s[1, sl, :]
        o_ref[sl, :] = ((num / den) * _silu(z_ref[sl, :])).astype(o_ref.dtype)

    def block(g, refs, d, base, pbase, has_prev):
        q_ref, k_ref, v_ref = refs
        q = load(q_ref, base, d)
        lc = lax.dot_general(q, load(k_ref, base, d), nt, preferred_element_type=F32)
        lc = lc * ATT_SCALE + bias_ref[g, :, BLK:]
        if has_prev is None:
            return lc, None
        lp = lax.dot_general(q, load(k_ref, pbase, d), nt, preferred_element_type=F32)
        lp = lp * ATT_SCALE + bias_ref[g, :, :BLK]
        return lc, jnp.where(has_prev, lp, NEG)

    def softmax(lc, lp):
        if lp is None:
            m = jnp.max(lc, axis=-1, keepdims=True)
            pc = jnp.exp(lc - m)
            return pc, None, m, jnp.sum(pc, axis=-1, keepdims=True)
        m = jnp.max(jnp.maximum(lc, lp), axis=-1, keepdims=True)
        pc = jnp.exp(lc - m)
        pp = jnp.exp(lp - m)
        return pc, pp, m, jnp.sum(pc + pp, axis=-1, keepdims=True)

    def weighted(refs, d, base, pbase, pc, pp):
        v_ref = refs[2]
        acc = jnp.dot(pc.astype(BF16), load(v_ref, base, d), preferred_element_type=F32)
        if pp is not None:
            acc = acc + jnp.dot(pp.astype(BF16), load(v_ref, pbase, d), preferred_element_type=F32)
        return acc

    groups = ((q0, k0, v0), (q1, k1, v1), (q2, k2, v2))

    def run_blocks(g, d, specs):
        refs = groups[g]
        logits = [block(g, refs, d, base, pbase, has_prev) for base, pbase, has_prev in specs]
        probs = [softmax(lc, lp) for lc, lp in logits]
        accs = [weighted(refs, d, base, pbase, pc, pp)
                for (base, pbase, _), (pc, pp, _, _) in zip(specs, probs)]
        for (base, _, _), acc, (_, _, m, s) in zip(specs, accs, probs):
            if d == 1:
                finish(base, acc, m, s)
            else:
                store(g, base, d, acc, m, s)

    assert ATT_GROUPS[0][1] == 1 and all(d > 1 for _, d in ATT_GROUPS[1:])
    for g in reversed(range(N_GROUPS)):
        d = ATT_GROUPS[g][1]
        span = BLK * d
        n_blocks = SEQ // span
        if n_blocks == 1:
            def body_r(rr, carry, g=g, d=d):
                run_blocks(g, d, [(rr * ATT_ILP + u, None, None) for u in range(ATT_ILP)])
                return carry
            lax.fori_loop(0, d // ATT_ILP, body_r, 0)
        elif d == 1:
            def body_n(nn, carry, g=g, span=span):
                specs = []
                for u in range(ATT_ILP_MERGE):
                    n = nn * ATT_ILP_MERGE + u
                    specs.append((pl.multiple_of(n * span, BLK),
                                  pl.multiple_of(jnp.maximum(n - 1, 0) * span, BLK), n > 0))
                run_blocks(g, 1, specs)
                return carry
            lax.fori_loop(0, n_blocks // ATT_ILP_MERGE, body_n, 0)
        else:
            n_per_iter = ATT_ILP // d
            assert n_per_iter * d == ATT_ILP and n_blocks % n_per_iter == 0
            def body_n(nn, carry, g=g, d=d, span=span, n_per_iter=n_per_iter):
                specs = []
                for u in range(n_per_iter):
                    n = nn * n_per_iter + u
                    specs += [(n * span + r, jnp.maximum(n - 1, 0) * span + r, n > 0)
                              for r in range(d)]
                run_blocks(g, d, specs)
                return carry
            lax.fori_loop(0, n_blocks // n_per_iter, body_n, 0)


def _attn_prompt(proj3, bias_p):
    def head_spec(col0):
        return pl.BlockSpec((None, SEQ, HEAD_DIM),
                            lambda b, h, c=col0 // HEAD_DIM: (b, 0, c + h))

    in_specs = []
    for g in range(N_GROUPS):
        for col in (COL_Q, COL_K, COL_V):
            in_specs.append(head_spec(col + g * ATT_WIDTH))
    in_specs.append(head_spec(COL_ZATT))
    in_specs.append(pl.BlockSpec((N_GROUPS, None, BLK, 2 * BLK), lambda b, h: (0, h, 0, 0)))
    scratch = [pltpu.VMEM((N_GROUPS - 1, SEQ, HEAD_DIM), F32) for _ in range(3)]
    return pl.pallas_call(
        _attn_prompt_kernel,
        grid=(BATCH, H_PER_GROUP),
        in_specs=in_specs,
        out_specs=pl.BlockSpec((None, SEQ, HEAD_DIM), lambda b, h: (b, 0, h)),
        out_shape=jax.ShapeDtypeStruct((BATCH, SEQ, ATT_WIDTH), BF16),
        scratch_shapes=scratch,
        compiler_params=_cparams(2),
        name="attn_prompt",
    )(*([proj3] * 10), bias_p)


def _lookup(tbl, idx):
    idx = np.asarray(idx, np.int32)
    onehot = (jnp.asarray(idx.reshape(-1, 1)) == jnp.arange(tbl.shape[0], dtype=jnp.int32)[None, :])
    out = jnp.dot(onehot.astype(F32), tbl.astype(F32), precision=lax.Precision.HIGHEST)
    return out.reshape(idx.shape + tbl.shape[1:])


def _prompt_bias(rel_bias):
    qi = np.arange(BLK)[:, None]
    ki = np.arange(2 * BLK)[None, :]
    du = qi + BLK - ki
    band = (du >= 0) & (du <= BLK)
    out = []
    for g, (win, dil) in enumerate(ATT_GROUPS):
        assert win // dil == BLK
        tbl = rel_bias[:, g * H_PER_GROUP:(g + 1) * H_PER_GROUP]
        b = _lookup(tbl, _t5_bucket(np.clip(du, 0, None) * dil))
        b = jnp.where(band[:, :, None], b, NEG)
        out.append(jnp.transpose(b, (2, 0, 1)))
    return jnp.stack(out)


def _softplus(x):
    return jnp.maximum(x, 0.0) + jnp.log1p(jnp.exp(-jnp.abs(x)))


def _lru_gates(xc, wr_ref, br, wi_ref, bi, nsp, n_blocks):
    k = (-0.5 * LRU_C) * nsp
    half_br = 0.5 * br
    half_bi = 0.5 * bi
    a_parts, u_parts = [], []
    for n in range(n_blocks):
        sl = slice(n * LRU_BLOCK, (n + 1) * LRU_BLOCK)
        xn = xc[:, sl]
        xb = xn.astype(BF16)
        wr_half = (0.5 * wr_ref[n]).astype(BF16)
        wi_half = (0.5 * wi_ref[n]).astype(BF16)
        t_r = jnp.tanh(jnp.dot(xb, wr_half, preferred_element_type=F32) + half_br[:, sl])
        t_i = jnp.tanh(jnp.dot(xb, wi_half, preferred_element_type=F32) + half_bi[:, sl])
        log_a = k[:, sl] * t_r + k[:, sl]
        a = jnp.exp(log_a)
        one_minus_a2 = jnp.tanh(log_a) * (-1.0 - a * a)
        root = jnp.where(one_minus_a2 > 0.0, one_minus_a2 * lax.rsqrt(one_minus_a2), 0.0)
        a_parts.append(a)
        u_parts.append(root * ((0.5 * xn) * (t_i + 1.0)))
    return a_parts, u_parts


def _scan8(a, u, rows):
    for s in (1, 2, 4):
        keep = rows >= s
        a_sh = pltpu.roll(a, s, 0)
        u_sh = pltpu.roll(u, s, 0)
        u = jnp.where(keep, a * u_sh + u, u)
        a = jnp.where(keep, a * a_sh, a)
    return a, u


def _lru_prompt_kernel(x_ref, z_ref, cw_ref, cb_ref, wr_ref, br_ref, wi_ref, bi_ref, lam_ref,
                       o_ref, conv_ref, hl_ref, a_s, u_s):
    T, C = x_ref.shape
    n_blocks = C // LRU_BLOCK
    R = 256
    pad = 8

    conv_ref[...] = x_ref[T - (CONV_WIDTH - 1):T, :]

    nsp = _softplus(-lam_ref[...])
    br = br_ref[...]
    bi = bi_ref[...]
    cb = cb_ref[...]

    def gates(c, carry, first=False):
        if first:
            r0 = 0
            xw = jnp.concatenate([jnp.zeros((pad, C), F32), x_ref[0:R, :]], axis=0)
        else:
            r0 = pl.multiple_of(c * R, R)
            xw = x_ref[pl.ds(pl.multiple_of(r0 - pad, pad), R + pad), :]
        xc = xw[pad:] * cw_ref[CONV_WIDTH - 1:CONV_WIDTH, :]
        for tap in range(CONV_WIDTH - 1):
            shifted = pltpu.roll(xw, CONV_WIDTH - 1 - tap, 0)[pad:]
            xc = xc + shifted * cw_ref[tap:tap + 1, :]
        xc = xc + cb
        a_parts, u_parts = _lru_gates(xc, wr_ref, br, wi_ref, bi, nsp, n_blocks)
        for n in range(n_blocks):
            sl = slice(n * LRU_BLOCK, (n + 1) * LRU_BLOCK)
            a_s[pl.ds(r0, R), sl] = a_parts[n]
            u_s[pl.ds(r0, R), sl] = u_parts[n]
        return carry

    gates(0, 0, first=True)
    lax.fori_loop(1, T // R, gates, 0)

    rows = lax.broadcasted_iota(jnp.int32, (8, C), 0)

    def scan(c, h_prev):
        t0 = pl.multiple_of(c * 16, 16)
        hs = []
        for half in range(2):
            sl = pl.ds(t0 + 8 * half, 8)
            a, u = _scan8(a_s[sl, :], u_s[sl, :], rows)
            h = a * h_prev + u
            hs.append(h)
            h_prev = h[7:8, :]
        h16 = jnp.concatenate(hs, axis=0)
        z = z_ref[pl.ds(t0, 16), :]
        o_ref[pl.ds(t0, 16), :] = (h16 * _silu(z)).astype(o_ref.dtype)
        return h_prev

    h_last = lax.fori_loop(0, T // 16, scan, jnp.zeros((1, C), F32))
    hl_ref[...] = h_last


def _lru_prompt(proj3, conv_w, conv_b, w_r, b_r, w_i, b_i, lam, layer):
    tc = 512
    nb = tc // LRU_BLOCK
    n_ct = LRU_WIDTH // tc

    def col_spec(col0):
        return pl.BlockSpec((None, SEQ, tc), lambda b, c, c0=col0 // tc: (b, 0, c0 + c))

    vec_spec = pl.BlockSpec((None, 1, tc), lambda b, c: (layer, 0, c))
    w_spec = pl.BlockSpec((None, nb, LRU_BLOCK, LRU_BLOCK), lambda b, c: (layer, c, 0, 0))
    return pl.pallas_call(
        _lru_prompt_kernel,
        grid=(BATCH, n_ct),
        in_specs=[
            col_spec(COL_XLRU), col_spec(COL_ZLRU),
            pl.BlockSpec((None, CONV_WIDTH, tc), lambda b, c: (layer, 0, c)),
            vec_spec, w_spec, vec_spec, w_spec, vec_spec, vec_spec,
        ],
        out_specs=[
            pl.BlockSpec((None, SEQ, tc), lambda b, c: (b, 0, c)),
            pl.BlockSpec((None, CONV_WIDTH - 1, tc), lambda b, c: (b, 0, c)),
            pl.BlockSpec((None, 1, tc), lambda b, c: (b, 0, c)),
        ],
        out_shape=[
            jax.ShapeDtypeStruct((BATCH, SEQ, LRU_WIDTH), BF16),
            jax.ShapeDtypeStruct((BATCH, CONV_WIDTH - 1, LRU_WIDTH), F32),
            jax.ShapeDtypeStruct((BATCH, 1, LRU_WIDTH), F32),
        ],
        scratch_shapes=[
            pltpu.VMEM((SEQ, tc), F32),
            pltpu.VMEM((SEQ, tc), F32),
        ],
        compiler_params=_cparams(2),
        name="lru_prompt",
    )(proj3, proj3, conv_w, conv_b, w_r, b_r, w_i, b_i, lam)


def _mix_kernel(att_ref, lru_ref, ga_ref, gl_ref, wpa_ref, wpb_ref, o_ref):
    y_att = jnp.dot(att_ref[...], wpa_ref[...], preferred_element_type=F32)
    y_lru = jnp.dot(lru_ref[...], wpb_ref[...], preferred_element_type=F32)
    merged = _sigmoid(ga_ref[...]) * y_att + _sigmoid(gl_ref[...]) * y_lru
    o_ref[...] = merged.astype(o_ref.dtype)


def _mix(att, lru, proj, wpa, wpb, layer, tm):
    m_rows = att.shape[0]
    return pl.pallas_call(
        _mix_kernel,
        grid=(m_rows // tm,),
        in_specs=[
            pl.BlockSpec((tm, ATT_WIDTH), lambda i: (i, 0)),
            pl.BlockSpec((tm, LRU_WIDTH), lambda i: (i, 0)),
            pl.BlockSpec((tm, D_MODEL), lambda i: (i, COL_GATT // D_MODEL)),
            pl.BlockSpec((tm, D_MODEL), lambda i: (i, COL_GLRU // D_MODEL)),
            pl.BlockSpec((None, ATT_WIDTH, D_MODEL), lambda i: (layer, 0, 0),
                         pipeline_mode=pl.Buffered(1)),
            pl.BlockSpec((None, LRU_WIDTH, D_MODEL), lambda i: (layer, 0, 0),
                         pipeline_mode=pl.Buffered(1)),
        ],
        out_specs=pl.BlockSpec((tm, D_MODEL), lambda i: (i, 0)),
        out_shape=jax.ShapeDtypeStruct((m_rows, D_MODEL), BF16),
        compiler_params=_cparams(1),
        name="branch_mix",
    )(att, lru, proj, proj, wpa, wpb)


def _residual_kernel(m_ref, w_ref, x_ref, gate_ref, g_ref, *rest, last):
    out = jnp.dot(m_ref[...], w_ref[...].astype(BF16), preferred_element_type=F32)
    x_new = x_ref[...] + gate_ref[...] * out
    y = x_new * lax.rsqrt(jnp.mean(x_new * x_new, axis=-1, keepdims=True) + RMS_EPS)
    y = y * g_ref[...]
    if last:
        (y_ref,) = rest
        y_ref[...] = y
    else:
        sc_ref, sh_ref, xo_ref, ho_ref = rest
        xo_ref[...] = x_new
        ho_ref[...] = (y * (1.0 + sc_ref[...]) + sh_ref[...]).astype(ho_ref.dtype)


def _residual_prompt(merged, w_out, x, gate, g, scale, shift, layer):
    tm = 512
    last = scale is None
    row_spec = pl.BlockSpec((None, tm, D_MODEL), lambda b, i: (b, i, 0))
    vec_spec = pl.BlockSpec((None, 1, D_MODEL), lambda b, i: (b, 0, 0))
    in_specs = [
        row_spec,
        pl.BlockSpec((None, D_MODEL, D_MODEL), lambda b, i: (layer, 0, 0),
                     pipeline_mode=pl.Buffered(1)),
        row_spec,
        vec_spec,
        pl.BlockSpec((1, D_MODEL), lambda b, i: (0, 0)),
    ]
    args = [merged, w_out, x, gate, g.reshape(1, D_MODEL)]
    if last:
        out_specs = row_spec
        out_shape = jax.ShapeDtypeStruct(x.shape, F32)
    else:
        in_specs += [vec_spec, vec_spec]
        args += [scale, shift]
        out_specs = [row_spec, row_spec]
        out_shape = [jax.ShapeDtypeStruct(x.shape, F32), jax.ShapeDtypeStruct(x.shape, BF16)]
    return pl.pallas_call(
        functools.partial(_residual_kernel, last=last),
        grid=(BATCH, SEQ // tm),
        in_specs=in_specs,
        out_specs=out_specs,
        out_shape=out_shape,
        compiler_params=_cparams(2),
        name="residual_prompt",
    )(*args)


def _residual_sample(merged, w_out, x, gate_rows, g, scale_rows, shift_rows, layer):
    last = scale_rows is None
    full = pl.BlockSpec((N_SAMPLE_ROWS, D_MODEL), lambda i: (0, 0))
    in_specs = [full, pl.BlockSpec((None, D_MODEL, D_MODEL), lambda i: (layer, 0, 0)), full, full,
                pl.BlockSpec((1, D_MODEL), lambda i: (0, 0))]
    args = [merged, w_out, x, gate_rows, g.reshape(1, D_MODEL)]
    if last:
        out_specs = full
        out_shape = jax.ShapeDtypeStruct(x.shape, F32)
    else:
        in_specs += [full, full]
        args += [scale_rows, shift_rows]
        out_specs = [full, full]
        out_shape = [jax.ShapeDtypeStruct(x.shape, F32), jax.ShapeDtypeStruct(x.shape, BF16)]
    return pl.pallas_call(
        functools.partial(_residual_kernel, last=last),
        grid=(1,),
        in_specs=in_specs,
        out_specs=out_specs,
        out_shape=out_shape,
        compiler_params=_cparams(1),
        name="residual_sample",
    )(*args)


CACHE_CHUNK_ROWS = 512
CACHE_SLOTS = 6
CACHE_LOOKAHEAD = 3


def _cache_update_kernel(c0, c1, c2, n0, n1, n2, o0, o1, o2, buf, in_sem, out_sem, new_sem):
    chunks = []
    new_copies = []
    for gi, (c, n, o) in enumerate(((c0, n0, o0), (c1, n1, o1), (c2, n2, o2))):
        keep = c.shape[2] - DEC_SEQ
        for l in range(DEPTH):
            new_copies.append(pltpu.make_async_copy(
                n.at[l], o.at[l, :, pl.ds(keep, DEC_SEQ)], new_sem.at[gi, l]))
            for b in range(DEC_BATCH):
                for r in range(0, keep, CACHE_CHUNK_ROWS):
                    rows = min(CACHE_CHUNK_ROWS, keep - r)
                    chunks.append((c.at[l, b, pl.ds(DEC_SEQ + r, rows)],
                                   o.at[l, b, pl.ds(r, rows)], rows))

    def read(i):
        src, _, rows = chunks[i]
        slot = i % CACHE_SLOTS
        return pltpu.make_async_copy(src, buf.at[slot, pl.ds(0, rows)], in_sem.at[slot])

    def write(i):
        _, dst, rows = chunks[i]
        slot = i % CACHE_SLOTS
        return pltpu.make_async_copy(buf.at[slot, pl.ds(0, rows)], dst, out_sem.at[slot])

    for cp in new_copies:
        cp.start()
    n_chunks = len(chunks)
    for i in range(n_chunks + CACHE_LOOKAHEAD):
        if i < n_chunks:
            if i >= CACHE_SLOTS:
                write(i - CACHE_SLOTS).wait()
            read(i).start()
        j = i - CACHE_LOOKAHEAD
        if j >= 0:
            read(j).wait()
            write(j).start()
    for j in range(max(n_chunks - CACHE_SLOTS, 0), n_chunks):
        write(j).wait()
    for cp in new_copies:
        cp.wait()


def _cache_update(caches, new_rows):
    any_spec = pl.BlockSpec(memory_space=pl.ANY)
    vmem_spec = pl.BlockSpec(memory_space=pltpu.VMEM)
    return pl.pallas_call(
        _cache_update_kernel,
        in_specs=[any_spec] * 3 + [vmem_spec] * 3,
        out_specs=[any_spec] * 3,
        out_shape=[jax.ShapeDtypeStruct(c.shape, c.dtype) for c in caches],
        scratch_shapes=[
            pltpu.VMEM((CACHE_SLOTS, CACHE_CHUNK_ROWS, 2, H_PER_GROUP, HEAD_DIM), F32),
            pltpu.SemaphoreType.DMA((CACHE_SLOTS,)),
            pltpu.SemaphoreType.DMA((CACHE_SLOTS,)),
            pltpu.SemaphoreType.DMA((N_GROUPS, DEPTH)),
        ],
        compiler_params=pltpu.CompilerParams(vmem_limit_bytes=VMEM_LIMIT),
        name="cache_update",
    )(*caches, *new_rows)


def _attn_sample_kernel(q_ref, k_ref, v_ref, z_ref, c0_ref, c1_ref, c2_ref, bc_ref, bn_ref,
                        o_ref, kv0_ref, kv1_ref, kv2_ref):
    hp = H_PER_GROUP
    kv_refs = (kv0_ref, kv1_ref, kv2_ref)
    k_new, v_new = [], []
    for g in range(N_GROUPS):
        kn = k_ref[:, g * hp:(g + 1) * hp, :]
        vn = v_ref[:, g * hp:(g + 1) * hp, :]
        kv_refs[g][:, 0] = kn
        kv_refs[g][:, 1] = vn
        k_new.append(kn)
        v_new.append(vn)

    for s in range(DEC_SEQ):
        accs, ms, ss = [], [], []
        for g in range(N_GROUPS):
            qg = q_ref[s, g * hp:(g + 1) * hp, :]
            if g == 0:
                kc, vc = c0_ref[:, 0], c0_ref[:, 1]
            elif g == 1:
                kc, vc = c1_ref[:, s, 0], c1_ref[:, s, 1]
            else:
                kc, vc = c2_ref[:, s, 0], c2_ref[:, s, 1]
            lc = jnp.sum(kc * qg[None], axis=-1, keepdims=True) * ATT_SCALE + bc_ref[g, s]
            ln = jnp.sum(k_new[g] * qg[None], axis=-1, keepdims=True) * ATT_SCALE + bn_ref[g, s]
            m = jnp.maximum(jnp.max(lc, axis=0), jnp.max(ln, axis=0))
            pc = jnp.exp(lc - m[None])
            pn = jnp.exp(ln - m[None])
            ss.append(jnp.sum(pc, axis=0) + jnp.sum(pn, axis=0))
            accs.append(jnp.sum(pc * vc, axis=0) + jnp.sum(pn * v_new[g], axis=0))
            ms.append(m)
        mm = jnp.maximum(jnp.maximum(ms[0], ms[1]), ms[2])
        ws = [jnp.exp(m - mm) for m in ms]
        num = ws[0] * accs[0] + ws[1] * accs[1] + ws[2] * accs[2]
        den = ws[0] * ss[0] + ws[1] * ss[1] + ws[2] * ss[2]
        o_ref[s] = (num / den) * _silu(z_ref[s])


def _attn_sample(proj_s4, caches, bias_c, bias_n, layer):
    c0, c1, c2 = caches
    hp = H_PER_GROUP

    def head_spec(col0, n_heads):
        return pl.BlockSpec((None, DEC_SEQ, n_heads, HEAD_DIM),
                            lambda b, c=col0 // (HEAD_DIM * n_heads): (b, 0, c, 0))

    in_specs = [
        head_spec(COL_Q, N_ATT_HEADS), head_spec(COL_K, N_ATT_HEADS), head_spec(COL_V, N_ATT_HEADS),
        head_spec(COL_ZATT, hp),
        pl.BlockSpec((None, None, BLK, 2, hp, HEAD_DIM), lambda b: (layer, b, 0, 0, 0, 0)),
        pl.BlockSpec((None, None, BLK, DEC_SEQ, 2, hp, HEAD_DIM), lambda b: (layer, b, 0, 0, 0, 0, 0)),
        pl.BlockSpec((None, None, BLK, DEC_SEQ, 2, hp, HEAD_DIM), lambda b: (layer, b, 0, 0, 0, 0, 0)),
        pl.BlockSpec(bias_c.shape, lambda b: (0,) * 5),
        pl.BlockSpec(bias_n.shape, lambda b: (0,) * 5),
    ]
    out_specs = [pl.BlockSpec((None, DEC_SEQ, hp, HEAD_DIM), lambda b: (b, 0, 0, 0))]
    out_shape = [jax.ShapeDtypeStruct((DEC_BATCH, DEC_SEQ, hp, HEAD_DIM), F32)]
    for _ in range(N_GROUPS):
        out_specs.append(pl.BlockSpec((None, DEC_SEQ, 2, hp, HEAD_DIM), lambda b: (b, 0, 0, 0, 0)))
        out_shape.append(jax.ShapeDtypeStruct((DEC_BATCH, DEC_SEQ, 2, hp, HEAD_DIM), F32))
    return pl.pallas_call(
        _attn_sample_kernel,
        grid=(DEC_BATCH,),
        in_specs=in_specs,
        out_specs=out_specs,
        out_shape=out_shape,
        compiler_params=_cparams(1),
        name="attn_sample",
    )(proj_s4, proj_s4, proj_s4, proj_s4, c0, c1, c2, bias_c, bias_n)


def _sample_bias(rel_bias):
    s_idx = np.arange(DEC_SEQ)
    bc, bn = [], []
    for g, (win, dil) in enumerate(ATT_GROUPS):
        tbl = rel_bias[:, g * H_PER_GROUP:(g + 1) * H_PER_GROUP].astype(F32)
        key = np.arange(BLK)
        if dil == 1:
            j = BLK + s_idx[:, None] - key[None, :]
            valid = j <= BLK
        else:
            j = np.broadcast_to(BLK - key[None, :], (DEC_SEQ, BLK))
            valid = np.ones_like(j, dtype=bool)
        b = _lookup(tbl, _t5_bucket(np.clip(j, 0, None) * dil))
        bc.append(jnp.where(valid[:, :, None], b, NEG))
        jn = s_idx[:, None] - s_idx[None, :]
        valid_n = (jn >= 0) & (jn * dil <= win) & ((jn == 0) | (dil == 1))
        b = _lookup(tbl, _t5_bucket(np.clip(jn, 0, None) * dil))
        bn.append(jnp.where(valid_n[:, :, None], b, NEG))
    bc = jnp.stack(bc)
    bn = jnp.stack(bn)
    bc = jnp.broadcast_to(bc[..., None], bc.shape + (HEAD_DIM,))
    bn = jnp.broadcast_to(bn[..., None], bn.shape + (HEAD_DIM,))
    return bc, bn


def _lru_sample_kernel(x_ref, z_ref, cs_ref, h0_ref, cw_ref, cb_ref, wr_ref, br_ref, wi_ref, bi_ref,
                       lam_ref, o_ref, conv_ref, hl_ref):
    S = DEC_SEQ
    xp = [cs_ref[t] for t in range(CONV_WIDTH - 1)] + [x_ref[t] for t in range(S)]
    xc = []
    for t in range(S):
        y = xp[t] * cw_ref[0:1, :]
        for tap in range(1, CONV_WIDTH):
            y = y + xp[t + tap] * cw_ref[tap:tap + 1, :]
        xc.append(y + cb_ref[...])
    xcat = jnp.concatenate(xc, axis=0)
    nsp = _softplus(-lam_ref[...])
    a_parts, u_parts = _lru_gates(xcat, wr_ref, br_ref[...], wi_ref, bi_ref[...], nsp, LRU_BLOCKS)
    a = jnp.concatenate(a_parts, axis=1)
    u = jnp.concatenate(u_parts, axis=1)
    nb = DEC_BATCH
    h = h0_ref[...]
    for t in range(S):
        h = a[t * nb:(t + 1) * nb] * h + u[t * nb:(t + 1) * nb]
        o_ref[t] = h * _silu(z_ref[t])
    for t in range(CONV_WIDTH - 1):
        conv_ref[t] = xp[S + t]
    hl_ref[...] = h


def _lru_sample(x_t, z_t, conv_t, h0, conv_w, conv_b, w_r, b_r, w_i, b_i, lam, layer):
    C = LRU_WIDTH

    def full(shape):
        return pl.BlockSpec(shape, lambda i: (0,) * len(shape))

    vec_spec = pl.BlockSpec((None, 1, C), lambda i: (layer, 0, 0))
    w_spec = pl.BlockSpec((None, LRU_BLOCKS, LRU_BLOCK, LRU_BLOCK), lambda i: (layer, 0, 0, 0))
    return pl.pallas_call(
        _lru_sample_kernel,
        grid=(1,),
        in_specs=[
            full((DEC_SEQ, DEC_BATCH, C)), full((DEC_SEQ, DEC_BATCH, C)),
            full((CONV_WIDTH - 1, DEC_BATCH, C)), full((DEC_BATCH, C)),
            pl.BlockSpec((None, CONV_WIDTH, C), lambda i: (layer, 0, 0)),
            vec_spec, w_spec, vec_spec, w_spec, vec_spec, vec_spec,
        ],
        out_specs=[full((DEC_SEQ, DEC_BATCH, C)), full((CONV_WIDTH - 1, DEC_BATCH, C)),
                   full((DEC_BATCH, C))],
        out_shape=[
            jax.ShapeDtypeStruct((DEC_SEQ, DEC_BATCH, C), F32),
            jax.ShapeDtypeStruct((CONV_WIDTH - 1, DEC_BATCH, C), F32),
            jax.ShapeDtypeStruct((DEC_BATCH, C), F32),
        ],
        compiler_params=_cparams(1),
        name="lru_sample",
    )(x_t, z_t, conv_t, h0, conv_w, conv_b, w_r, b_r, w_i, b_i, lam)


def kernel(x_prompt, x_sample, c_prompt, c_sample, cache_kv_g0, cache_kv_g1, cache_kv_g2, state_conv, state_h, rel_bias, w_ada, b_ada, norm_g, w_in, conv_w, conv_b, w_r, b_r, w_i, b_i, lam, w_pa, w_pb, w_out, final_g):
    L, B, T, D = DEPTH, BATCH, SEQ, D_MODEL
    Bd, S = DEC_BATCH, DEC_SEQ

    c_all = jnp.concatenate(
        [c_prompt, c_sample, jnp.zeros((MOD_ROWS - B - Bd, D), F32)], axis=0)
    mod = _modulation(c_all, w_ada, b_ada).reshape(L, MOD_ROWS, 3, D)
    mod_p = mod[:, :B]
    mod_s = jnp.repeat(mod[:, B:B + Bd], S, axis=1)

    wpa_bf = w_pa.astype(BF16)
    wpb_bf = w_pb.astype(BF16)
    conv_b3 = conv_b.reshape(L, 1, LRU_WIDTH)
    b_r3 = b_r.reshape(L, 1, LRU_WIDTH)
    b_i3 = b_i.reshape(L, 1, LRU_WIDTH)
    lam3 = lam.reshape(L, 1, LRU_WIDTH)

    bias_p = _prompt_bias(rel_bias)
    bias_c, bias_n = _sample_bias(rel_bias)

    cache_views = (
        cache_kv_g0,
        cache_kv_g1.reshape(L, Bd, BLK, 4, 2, H_PER_GROUP, HEAD_DIM),
        cache_kv_g2.reshape(L, Bd, BLK, 16, 2, H_PER_GROUP, HEAD_DIM),
    )

    xp = x_prompt
    xs = x_sample.reshape(Bd * S, D)
    kv_p = None
    kv_new = ([], [], [])
    conv_p, h_p, conv_s, h_s = [], [], [], []

    def mods(l):
        shift_p, scale_p, gate_p = (mod_p[l, :, i][:, None, :] for i in range(3))
        shift_s, scale_s, gate_s = (mod_s[l, :, i] for i in range(3))
        return (shift_p, scale_p, gate_p), (shift_s, scale_s, gate_s)

    (shift_p, scale_p, gate_p), (shift_s, scale_s, gate_s) = mods(0)
    hp_ = _norm_prompt(xp, norm_g[0], scale_p, shift_p)
    hs_ = _norm_sample(xs, norm_g[0], scale_s, shift_s)
    for l in range(L):
        proj_p, proj_s, *kv_p = _in_proj(hp_.reshape(B * T, D), hs_, w_in, kv_p, l)
        proj3 = proj_p.reshape(B, T, IN_COLS)

        att_p = _attn_prompt(proj3, bias_p)
        lru_p, cv, hl = _lru_prompt(proj3, conv_w, conv_b3, w_r, b_r3, w_i, b_i3, lam3, l)
        merged_p = _mix(att_p.reshape(B * T, ATT_WIDTH), lru_p.reshape(B * T, LRU_WIDTH),
                        proj_p, wpa_bf, wpb_bf, l, 512)
        last = l == L - 1
        if not last:
            (shift_pn, scale_pn, gate_pn), (shift_sn, scale_sn, gate_sn) = mods(l + 1)
            xp, hp_ = _residual_prompt(merged_p.reshape(B, T, D), w_out, xp, gate_p,
                                       norm_g[l + 1], scale_pn, shift_pn, l)
        else:
            y_prompt = _residual_prompt(merged_p.reshape(B, T, D), w_out, xp, gate_p,
                                        final_g, None, None, l)
        conv_p.append(cv)
        h_p.append(hl.reshape(B, LRU_WIDTH))

        proj_s4 = proj_s.reshape(Bd, S, IN_COLS // HEAD_DIM, HEAD_DIM)
        outs = _attn_sample(proj_s4, cache_views, bias_c, bias_n, l)
        att_s = outs[0]
        for g in range(N_GROUPS):
            kv_new[g].append(outs[1 + g])
        ps3 = proj_s.reshape(Bd, S, IN_COLS)
        x_t = jnp.transpose(ps3[:, :, COL_XLRU:COL_XLRU + LRU_WIDTH], (1, 0, 2))
        z_t = jnp.transpose(ps3[:, :, COL_ZLRU:COL_ZLRU + LRU_WIDTH], (1, 0, 2))
        conv_t = jnp.transpose(state_conv[l], (1, 0, 2))
        lru_t, cv_t, hl_s = _lru_sample(x_t, z_t, conv_t, state_h[l], conv_w, conv_b3,
                                        w_r, b_r3, w_i, b_i3, lam3, l)
        lru_s = jnp.transpose(lru_t, (1, 0, 2)).reshape(Bd * S, LRU_WIDTH).astype(BF16)
        merged_s = _mix(att_s.reshape(Bd * S, ATT_WIDTH).astype(BF16), lru_s, proj_s,
                        wpa_bf, wpb_bf, l, Bd * S)
        if not last:
            xs, hs_ = _residual_sample(merged_s, w_out, xs, gate_s,
                                       norm_g[l + 1], scale_sn, shift_sn, l)
            gate_p, gate_s = gate_pn, gate_sn
        else:
            y_sample = _residual_sample(merged_s, w_out, xs, gate_s,
                                        final_g, None, None, l).reshape(Bd, S, D)
        conv_s.append(jnp.transpose(cv_t, (1, 0, 2)))
        h_s.append(hl_s)

    kv_s = _cache_update((cache_kv_g0, cache_kv_g1, cache_kv_g2),
                         tuple(jnp.stack(rows) for rows in kv_new))
    return (y_prompt, y_sample,
            kv_p[0], kv_p[1], kv_p[2],
            jnp.stack(conv_p), jnp.stack(h_p),
            kv_s[0], kv_s[1], kv_s[2],
            jnp.stack(conv_s), jnp.stack(h_s))
```

```python
import functools
import math

import jax
import jax.numpy as jnp
import numpy as np
from jax import lax
from jax.experimental import pallas as pl
from jax.experimental.pallas import tpu as pltpu

D_MODEL = 2048
BATCH = 4
SEQ = 2048
DEPTH = 4
DEC_BATCH = 8
DEC_SEQ = 4
HEAD_DIM = 128
H_PER_GROUP = 8
ATT_GROUPS = ((128, 1), (512, 4), (2048, 16))
N_GROUPS = 3
N_ATT_HEADS = N_GROUPS * H_PER_GROUP
QKV_WIDTH = N_ATT_HEADS * HEAD_DIM
ATT_WIDTH = H_PER_GROUP * HEAD_DIM
N_BUCKETS = 32
REL_MAX_DIST = 2048
BLK = 128
LRU_WIDTH = D_MODEL
LRU_BLOCKS = 16
LRU_BLOCK = LRU_WIDTH // LRU_BLOCKS
CONV_WIDTH = 4
LRU_C = 8.0
IN_COLS = 3 * QKV_WIDTH + ATT_WIDTH + 2 * LRU_WIDTH + 2 * D_MODEL
RMS_EPS = 1e-6
NEG = -1e30
ATT_SCALE = HEAD_DIM ** -0.5

COL_Q = 0
COL_K = QKV_WIDTH
COL_V = 2 * QKV_WIDTH
COL_ZATT = 3 * QKV_WIDTH
COL_XLRU = COL_ZATT + ATT_WIDTH
COL_ZLRU = COL_XLRU + LRU_WIDTH
COL_GATT = COL_ZLRU + LRU_WIDTH
COL_GLRU = COL_GATT + D_MODEL

N_PROMPT_ROWS = BATCH * SEQ
N_SAMPLE_ROWS = DEC_BATCH * DEC_SEQ
MOD_ROWS = 16

VMEM_LIMIT = 52 * 1024 * 1024
ATT_ILP = 16
ATT_ILP_MERGE = 16

F32 = jnp.float32
BF16 = jnp.bfloat16


def _cparams(n_grid_dims, vmem_limit=VMEM_LIMIT):
    return pltpu.CompilerParams(
        dimension_semantics=("arbitrary",) * n_grid_dims,
        vmem_limit_bytes=vmem_limit)


def _sigmoid(x):
    return 0.5 * jnp.tanh(0.5 * x) + 0.5


def _silu(x):
    half = 0.5 * x
    return half * (jnp.tanh(half) + 1.0)


def _t5_bucket(dist):
    dist = np.asarray(dist).astype(np.int32)
    max_exact = N_BUCKETS // 2
    safe = np.maximum(dist, 1).astype(np.float32)
    large = max_exact + (np.log(safe / max_exact) / np.float32(math.log(REL_MAX_DIST / max_exact))
                         * (N_BUCKETS - max_exact)).astype(np.int32)
    large = np.minimum(large, N_BUCKETS - 1)
    return np.where(dist < max_exact, dist, large).astype(np.int32)


def _mod_kernel(c_ref, w_ref, b_ref, o_ref):
    c = _silu(c_ref[...])
    o_ref[...] = jnp.dot(c.astype(BF16), w_ref[...].astype(BF16),
                         preferred_element_type=F32) + b_ref[...]


def _modulation(c_all, w_ada, b_ada):
    tn = 1024
    n_cols = 3 * D_MODEL
    return pl.pallas_call(
        _mod_kernel,
        grid=(DEPTH, n_cols // tn),
        in_specs=[
            pl.BlockSpec((MOD_ROWS, D_MODEL), lambda l, j: (0, 0)),
            pl.BlockSpec((None, D_MODEL, tn), lambda l, j: (l, 0, j)),
            pl.BlockSpec((None, 1, tn), lambda l, j: (l, 0, j)),
        ],
        out_specs=pl.BlockSpec((None, MOD_ROWS, tn), lambda l, j: (l, 0, j)),
        out_shape=jax.ShapeDtypeStruct((DEPTH, MOD_ROWS, n_cols), F32),
        compiler_params=_cparams(2),
        name="adaln_mod",
    )(c_all, w_ada, b_ada.reshape(DEPTH, 1, n_cols))


def _norm_kernel(x_ref, g_ref, sc_ref, sh_ref, o_ref):
    x = x_ref[...]
    y = x * lax.rsqrt(jnp.mean(x * x, axis=-1, keepdims=True) + RMS_EPS)
    y = y * g_ref[...]
    o_ref[...] = (y * (1.0 + sc_ref[...]) + sh_ref[...]).astype(o_ref.dtype)


def _norm_prompt(x, g, scale, shift):
    tm = 512
    row_spec = pl.BlockSpec((None, tm, D_MODEL), lambda b, i: (b, i, 0))
    vec_spec = pl.BlockSpec((None, 1, D_MODEL), lambda b, i: (b, 0, 0))
    return pl.pallas_call(
        _norm_kernel,
        grid=(BATCH, SEQ // tm),
        in_specs=[row_spec, pl.BlockSpec((1, D_MODEL), lambda b, i: (0, 0)), vec_spec, vec_spec],
        out_specs=row_spec,
        out_shape=jax.ShapeDtypeStruct(x.shape, BF16),
        compiler_params=_cparams(2),
        name="rmsnorm_prompt",
    )(x, g.reshape(1, D_MODEL), scale, shift)


def _norm_sample(x, g, scale_rows, shift_rows):
    return pl.pallas_call(
        _norm_kernel,
        out_shape=jax.ShapeDtypeStruct(x.shape, BF16),
        name="rmsnorm_sample",
    )(x, g.reshape(1, D_MODEL), scale_rows, shift_rows)


IN_PROJ_VMEM_LIMIT = 58 * 1024 * 1024
IN_TM = 1024
IN_TN = ATT_WIDTH
IN_TILES_PER_SEQ = SEQ // IN_TM
N_ROW_TILES = N_PROMPT_ROWS // IN_TM
K_TILE0 = COL_K // IN_TN
V_TILE0 = COL_V // IN_TN


def _kv_rows(g):
    w = min(ATT_GROUPS[g][0], SEQ)
    assert w % IN_TM == 0 or IN_TM % w == 0
    return min(w, IN_TM)


def _kv_blocks_per_layer(g):
    n_wblk = min(ATT_GROUPS[g][0], SEQ) // _kv_rows(g)
    return BATCH * n_wblk * 2, n_wblk


WCAST_ROWS = 32
N_WPA_CHUNKS = ATT_WIDTH // WCAST_ROWS
N_WPB_CHUNKS = LRU_WIDTH // WCAST_ROWS


def _wcast_chunk(first, count):
    def index(j, i):
        return jnp.clip(j * N_ROW_TILES + i - first, 0, count - 1)
    return index


def _in_proj_kernel(h_ref, hs_ref, w_ref, wpa_ref, wpb_ref, *rest, creates_kv):
    if not creates_kv:
        rest = rest[N_GROUPS:]
    o_ref, os_ref, kv0_ref, kv1_ref, kv2_ref, wpa_o, wpb_o, wbf_ref = rest
    j = pl.program_id(0)
    i = pl.program_id(1)

    t_step = j * N_ROW_TILES + i

    @pl.when(t_step < N_WPA_CHUNKS)
    def _():
        wpa_o[...] = wpa_ref[...].astype(BF16)

    @pl.when((t_step >= N_WPA_CHUNKS) & (t_step < N_WPA_CHUNKS + N_WPB_CHUNKS))
    def _():
        wpb_o[...] = wpb_ref[...].astype(BF16)

    @pl.when(i == 0)
    def _():
        wb = w_ref[...].astype(BF16)
        wbf_ref[...] = wb
        o_ref[...] = jnp.dot(h_ref[...], wb, preferred_element_type=F32)
        os_ref[...] = jnp.dot(hs_ref[...], wb, preferred_element_type=F32)

    @pl.when(i != 0)
    def _():
        o_ref[...] = jnp.dot(h_ref[...], wbf_ref[...], preferred_element_type=F32)

    for g, kv_ref in enumerate((kv0_ref, kv1_ref, kv2_ref)):
        rows = _kv_rows(g)
        is_kv = (j == K_TILE0 + g) | (j == V_TILE0 + g)
        if min(ATT_GROUPS[g][0], SEQ) < SEQ:
            is_kv = is_kv & (i % IN_TILES_PER_SEQ == IN_TILES_PER_SEQ - 1)

        @pl.when(is_kv)
        def _(kv_ref=kv_ref, rows=rows):
            flat = kv_ref.reshape(rows * H_PER_GROUP, HEAD_DIM)
            for h in range(H_PER_GROUP):
                flat[pl.ds(h, rows, stride=H_PER_GROUP), :] = (
                    o_ref[IN_TM - rows:, h * HEAD_DIM:(h + 1) * HEAD_DIM])

        if creates_kv:
            per_layer, _ = _kv_blocks_per_layer(g)
            t = (j - (V_TILE0 + g) - 1) * N_ROW_TILES + i

            @pl.when((j > V_TILE0 + g) & (t < (DEPTH - 1) * per_layer))
            def _(kv_ref=kv_ref):
                kv_ref[...] = jnp.zeros(kv_ref.shape, kv_ref.dtype)


def _kv_index_map(g, layer, creates_kv):
    w = min(ATT_GROUPS[g][0], SEQ)
    full = w >= SEQ
    per_layer, n_wblk = _kv_blocks_per_layer(g)

    def index_map(j, i):
        k_col = K_TILE0 + g
        v_col = V_TILE0 + g
        c = jnp.where(j >= v_col, 1, 0)
        in_col = (j == k_col) | (j == v_col)
        if full:
            ii = jnp.where(in_col, i, jnp.where(j < k_col, 0, N_ROW_TILES - 1))
            b, wblk = ii // IN_TILES_PER_SEQ, ii % IN_TILES_PER_SEQ
        else:
            b_col = jnp.maximum(i - (IN_TILES_PER_SEQ - 1), 0) // IN_TILES_PER_SEQ
            b = jnp.where(in_col, b_col, jnp.where(j < k_col, 0, BATCH - 1))
            wblk = 0
        if not creates_kv:
            return (layer, b, wblk, c, 0, 0)
        assert layer == 0 and (IN_COLS // IN_TN - v_col - 1) * N_ROW_TILES >= (DEPTH - 1) * per_layer
        f = jnp.clip((j - v_col - 1) * N_ROW_TILES + i, 0, (DEPTH - 1) * per_layer - 1)
        rest = f % per_layer
        fill = j > v_col
        return (jnp.where(fill, 1 + f // per_layer, 0),
                jnp.where(fill, rest // (2 * n_wblk), b),
                jnp.where(fill, (rest // 2) % n_wblk, wblk),
                jnp.where(fill, rest % 2, c), 0, 0)

    return index_map


def _in_proj(h_p, h_s, w_in, w_pa, w_pb, kv_stacks, layer):
    tm, tn = IN_TM, IN_TN
    creates_kv = kv_stacks is None
    if creates_kv:
        kv_shapes = [jax.ShapeDtypeStruct((DEPTH, BATCH, min(win, SEQ), 2, H_PER_GROUP, HEAD_DIM), F32)
                     for win, _ in ATT_GROUPS]
        kv_stacks = ()
    else:
        kv_shapes = [jax.ShapeDtypeStruct(kv.shape, kv.dtype) for kv in kv_stacks]
    out_specs = [
        pl.BlockSpec((tm, tn), lambda j, i: (i, j)),
        pl.BlockSpec((N_SAMPLE_ROWS, tn), lambda j, i: (0, j)),
    ]
    out_shape = [
        jax.ShapeDtypeStruct((N_PROMPT_ROWS, IN_COLS), F32),
        jax.ShapeDtypeStruct((N_SAMPLE_ROWS, IN_COLS), F32),
    ]
    for g, shape in enumerate(kv_shapes):
        out_specs.append(pl.BlockSpec((None, None, _kv_rows(g), None, H_PER_GROUP, HEAD_DIM),
                                      _kv_index_map(g, layer, creates_kv)))
        out_shape.append(shape)
    assert N_WPA_CHUNKS + N_WPB_CHUNKS <= (IN_COLS // tn) * N_ROW_TILES
    pa_chunk = _wcast_chunk(0, N_WPA_CHUNKS)
    pb_chunk = _wcast_chunk(N_WPA_CHUNKS, N_WPB_CHUNKS)
    out_specs += [
        pl.BlockSpec((WCAST_ROWS, D_MODEL), lambda j, i: (pa_chunk(j, i), 0)),
        pl.BlockSpec((WCAST_ROWS, D_MODEL), lambda j, i: (pb_chunk(j, i), 0)),
    ]
    out_shape += [
        jax.ShapeDtypeStruct((ATT_WIDTH, D_MODEL), BF16),
        jax.ShapeDtypeStruct((LRU_WIDTH, D_MODEL), BF16),
    ]
    n_in = 5
    return pl.pallas_call(
        functools.partial(_in_proj_kernel, creates_kv=creates_kv),
        grid=(IN_COLS // tn, N_ROW_TILES),
        in_specs=[
            pl.BlockSpec((tm, D_MODEL), lambda j, i: (i, 0)),
            pl.BlockSpec((N_SAMPLE_ROWS, D_MODEL), lambda j, i: (0, 0)),
            pl.BlockSpec((None, D_MODEL, tn), lambda j, i: (layer, 0, j)),
            pl.BlockSpec((None, WCAST_ROWS, D_MODEL), lambda j, i: (layer, pa_chunk(j, i), 0)),
            pl.BlockSpec((None, WCAST_ROWS, D_MODEL), lambda j, i: (layer, pb_chunk(j, i), 0)),
        ] + [pl.BlockSpec(memory_space=pl.ANY)] * len(kv_stacks),
        out_specs=out_specs,
        out_shape=out_shape,
        input_output_aliases={n_in + g: 2 + g for g in range(len(kv_stacks))},
        scratch_shapes=[pltpu.VMEM((D_MODEL, tn), BF16)],
        compiler_params=_cparams(2, IN_PROJ_VMEM_LIMIT),
        name="in_proj",
    )(h_p, h_s, w_in, w_pa, w_pb, *kv_stacks)


def _attn_prompt_kernel(q0, k0, v0, q1, k1, v1, q2, k2, v2, z_ref, bias_ref, o_ref,
                        acc_s, m_s, s_s):
    nt = (((1,), (1,)), ((), ()))

    def load(ref, base, d):
        if d == 1:
            return ref[pl.ds(base, BLK), :].astype(BF16)
        return ref[pl.ds(base, BLK, stride=d), :].astype(BF16)

    def store(g, base, d, acc, m, s):
        idx = pl.ds(base, BLK, stride=d)
        acc_s[g - 1, idx, :] = acc
        m_s[g - 1, idx, :] = jnp.broadcast_to(m, (BLK, HEAD_DIM))
        s_s[g - 1, idx, :] = jnp.broadcast_to(s, (BLK, HEAD_DIM))

    def finish(base, acc, m, s):
        sl = pl.ds(base, BLK)
        m1, m2 = m_s[0, sl, :], m_s[1, sl, :]
        mm = jnp.maximum(jnp.maximum(m, m1), m2)
        w0, w1, w2 = jnp.exp(m - mm), jnp.exp(m1 - mm), jnp.exp(m2 - mm)
        num = w0 * acc + w1 * acc_s[0, sl, :] + w2 * acc_s[1, sl, :]
        den = w0 * s + w1 * s_s[0, sl, :] + w2 * s_s[1, sl, :]
        o_ref[sl, :] = ((num / den) * _silu(z_ref[sl, :])).astype(o_ref.dtype)

    def block(g, refs, d, base, pbase, has_prev):
        q_ref, k_ref, v_ref = refs
        q = load(q_ref, base, d)
        lc = lax.dot_general(q, load(k_ref, base, d), nt, preferred_element_type=F32)
        lc = lc * ATT_SCALE + bias_ref[g, :, BLK:]
        if has_prev is None:
            return lc, None
        lp = lax.dot_general(q, load(k_ref, pbase, d), nt, preferred_element_type=F32)
        lp = lp * ATT_SCALE + bias_ref[g, :, :BLK]
        return lc, jnp.where(has_prev, lp, NEG)

    def softmax(lc, lp):
        if lp is None:
            m = jnp.max(lc, axis=-1, keepdims=True)
            pc = jnp.exp(lc - m)
            return pc, None, m, jnp.sum(pc, axis=-1, keepdims=True)
        m = jnp.max(jnp.maximum(lc, lp), axis=-1, keepdims=True)
        pc = jnp.exp(lc - m)
        pp = jnp.exp(lp - m)
        return pc, pp, m, jnp.sum(pc + pp, axis=-1, keepdims=True)

    def weighted(refs, d, base, pbase, pc, pp):
        v_ref = refs[2]
        acc = jnp.dot(pc.astype(BF16), load(v_ref, base, d), preferred_element_type=F32)
        if pp is not None:
            acc = acc + jnp.dot(pp.astype(BF16), load(v_ref, pbase, d), preferred_element_type=F32)
        return acc

    groups = ((q0, k0, v0), (q1, k1, v1), (q2, k2, v2))

    def run_blocks(g, d, specs):
        refs = groups[g]
        logits = [block(g, refs, d, base, pbase, has_prev) for base, pbase, has_prev in specs]
        probs = [softmax(lc, lp) for lc, lp in logits]
        accs = [weighted(refs, d, base, pbase, pc, pp)
                for (base, pbase, _), (pc, pp, _, _) in zip(specs, probs)]
        for (base, _, _), acc, (_, _, m, s) in zip(specs, accs, probs):
            if d == 1:
                finish(base, acc, m, s)
            else:
                store(g, base, d, acc, m, s)

    assert ATT_GROUPS[0][1] == 1 and all(d > 1 for _, d in ATT_GROUPS[1:])
    for g in reversed(range(N_GROUPS)):
        d = ATT_GROUPS[g][1]
        span = BLK * d
        n_blocks = SEQ // span
        if n_blocks == 1:
            def body_r(rr, carry, g=g, d=d):
                run_blocks(g, d, [(rr * ATT_ILP + u, None, None) for u in range(ATT_ILP)])
                return carry
            lax.fori_loop(0, d // ATT_ILP, body_r, 0)
        elif d == 1:
            def body_n(nn, carry, g=g, span=span):
                specs = []
                for u in range(ATT_ILP_MERGE):
                    n = nn * ATT_ILP_MERGE + u
                    specs.append((pl.multiple_of(n * span, BLK),
                                  pl.multiple_of(jnp.maximum(n - 1, 0) * span, BLK), n > 0))
                run_blocks(g, 1, specs)
                return carry
            lax.fori_loop(0, n_blocks // ATT_ILP_MERGE, body_n, 0)
        else:
            n_per_iter = ATT_ILP // d
            assert n_per_iter * d == ATT_ILP and n_blocks % n_per_iter == 0
            def body_n(nn, carry, g=g, d=d, span=span, n_per_iter=n_per_iter):
                specs = []
                for u in range(n_per_iter):
                    n = nn * n_per_iter + u
                    specs += [(n * span + r, jnp.maximum(n - 1, 0) * span + r, n > 0)
                              for r in range(d)]
                run_blocks(g, d, specs)
                return carry
            lax.fori_loop(0, n_blocks // n_per_iter, body_n, 0)


def _attn_prompt(proj3, bias_p):
    def head_spec(col0):
        return pl.BlockSpec((None, SEQ, HEAD_DIM),
                            lambda b, h, c=col0 // HEAD_DIM: (b, 0, c + h))

    in_specs = []
    for g in range(N_GROUPS):
        for col in (COL_Q, COL_K, COL_V):
            in_specs.append(head_spec(col + g * ATT_WIDTH))
    in_specs.append(head_spec(COL_ZATT))
    in_specs.append(pl.BlockSpec((N_GROUPS, None, BLK, 2 * BLK), lambda b, h: (0, h, 0, 0)))
    scratch = [pltpu.VMEM((N_GROUPS - 1, SEQ, HEAD_DIM), F32) for _ in range(3)]
    return pl.pallas_call(
        _attn_prompt_kernel,
        grid=(BATCH, H_PER_GROUP),
        in_specs=in_specs,
        out_specs=pl.BlockSpec((None, SEQ, HEAD_DIM), lambda b, h: (b, 0, h)),
        out_shape=jax.ShapeDtypeStruct((BATCH, SEQ, ATT_WIDTH), BF16),
        scratch_shapes=scratch,
        compiler_params=_cparams(2),
        name="attn_prompt",
    )(*([proj3] * 10), bias_p)


def _lookup(tbl, idx):
    idx = np.asarray(idx, np.int32)
    onehot = (jnp.asarray(idx.reshape(-1, 1)) == jnp.arange(tbl.shape[0], dtype=jnp.int32)[None, :])
    out = jnp.dot(onehot.astype(F32), tbl.astype(F32), precision=lax.Precision.HIGHEST)
    return out.reshape(idx.shape + tbl.shape[1:])


def _prompt_bias(rel_bias):
    qi = np.arange(BLK)[:, None]
    ki = np.arange(2 * BLK)[None, :]
    du = qi + BLK - ki
    band = (du >= 0) & (du <= BLK)
    out = []
    for g, (win, dil) in enumerate(ATT_GROUPS):
        assert win // dil == BLK
        tbl = rel_bias[:, g * H_PER_GROUP:(g + 1) * H_PER_GROUP]
        b = _lookup(tbl, _t5_bucket(np.clip(du, 0, None) * dil))
        b = jnp.where(band[:, :, None], b, NEG)
        out.append(jnp.transpose(b, (2, 0, 1)))
    return jnp.stack(out)


def _softplus(x):
    return jnp.maximum(x, 0.0) + jnp.log1p(jnp.exp(-jnp.abs(x)))


def _lru_gates(xc, wr_ref, br, wi_ref, bi, nsp, n_blocks):
    k = (-0.5 * LRU_C) * nsp
    half_br = 0.5 * br
    half_bi = 0.5 * bi
    a_parts, u_parts = [], []
    for n in range(n_blocks):
        sl = slice(n * LRU_BLOCK, (n + 1) * LRU_BLOCK)
        xn = xc[:, sl]
        xb = xn.astype(BF16)
        wr_half = (0.5 * wr_ref[n]).astype(BF16)
        wi_half = (0.5 * wi_ref[n]).astype(BF16)
        t_r = jnp.tanh(jnp.dot(xb, wr_half, preferred_element_type=F32) + half_br[:, sl])
        t_i = jnp.tanh(jnp.dot(xb, wi_half, preferred_element_type=F32) + half_bi[:, sl])
        log_a = k[:, sl] * t_r + k[:, sl]
        a = jnp.exp(log_a)
        one_minus_a2 = jnp.tanh(log_a) * (-1.0 - a * a)
        root = jnp.where(one_minus_a2 > 0.0, one_minus_a2 * lax.rsqrt(one_minus_a2), 0.0)
        a_parts.append(a)
        u_parts.append(root * ((0.5 * xn) * (t_i + 1.0)))
    return a_parts, u_parts


def _scan8(a, u, rows):
    for s in (1, 2, 4):
        keep = rows >= s
        a_sh = pltpu.roll(a, s, 0)
        u_sh = pltpu.roll(u, s, 0)
        u = jnp.where(keep, a * u_sh + u, u)
        a = jnp.where(keep, a * a_sh, a)
    return a, u


def _lru_prompt_kernel(x_ref, z_ref, cw_ref, cb_ref, wr_ref, br_ref, wi_ref, bi_ref, lam_ref,
                       o_ref, conv_ref, hl_ref, a_s, u_s):
    T, C = x_ref.shape
    n_blocks = C // LRU_BLOCK
    R = 256
    pad = 8

    conv_ref[...] = x_ref[T - (CONV_WIDTH - 1):T, :]

    nsp = _softplus(-lam_ref[...])
    br = br_ref[...]
    bi = bi_ref[...]
    cb = cb_ref[...]

    def gates(c, carry, first=False):
        if first:
            r0 = 0
            xw = jnp.concatenate([jnp.zeros((pad, C), F32), x_ref[0:R, :]], axis=0)
        else:
            r0 = pl.multiple_of(c * R, R)
            xw = x_ref[pl.ds(pl.multiple_of(r0 - pad, pad), R + pad), :]
        xc = xw[pad:] * cw_ref[CONV_WIDTH - 1:CONV_WIDTH, :]
        for tap in range(CONV_WIDTH - 1):
            shifted = pltpu.roll(xw, CONV_WIDTH - 1 - tap, 0)[pad:]
            xc = xc + shifted * cw_ref[tap:tap + 1, :]
        xc = xc + cb
        a_parts, u_parts = _lru_gates(xc, wr_ref, br, wi_ref, bi, nsp, n_blocks)
        for n in range(n_blocks):
            sl = slice(n * LRU_BLOCK, (n + 1) * LRU_BLOCK)
            a_s[pl.ds(r0, R), sl] = a_parts[n]
            u_s[pl.ds(r0, R), sl] = u_parts[n]
        return carry

    gates(0, 0, first=True)
    lax.fori_loop(1, T // R, gates, 0)

    rows = lax.broadcasted_iota(jnp.int32, (8, C), 0)

    def scan(c, h_prev):
        t0 = pl.multiple_of(c * 16, 16)
        hs = []
        for half in range(2):
            sl = pl.ds(t0 + 8 * half, 8)
            a, u = _scan8(a_s[sl, :], u_s[sl, :], rows)
            h = a * h_prev + u
            hs.append(h)
            h_prev = h[7:8, :]
        h16 = jnp.concatenate(hs, axis=0)
        z = z_ref[pl.ds(t0, 16), :]
        o_ref[pl.ds(t0, 16), :] = (h16 * _silu(z)).astype(o_ref.dtype)
        return h_prev

    h_last = lax.fori_loop(0, T // 16, scan, jnp.zeros((1, C), F32))
    hl_ref[...] = h_last


def _lru_prompt(proj3, conv_w, conv_b, w_r, b_r, w_i, b_i, lam, layer):
    tc = 512
    nb = tc // LRU_BLOCK
    n_ct = LRU_WIDTH // tc

    def col_spec(col0):
        return pl.BlockSpec((None, SEQ, tc), lambda b, c, c0=col0 // tc: (b, 0, c0 + c))

    vec_spec = pl.BlockSpec((None, 1, tc), lambda b, c: (layer, 0, c))
    w_spec = pl.BlockSpec((None, nb, LRU_BLOCK, LRU_BLOCK), lambda b, c: (layer, c, 0, 0))
    return pl.pallas_call(
        _lru_prompt_kernel,
        grid=(BATCH, n_ct),
        in_specs=[
            col_spec(COL_XLRU), col_spec(COL_ZLRU),
            pl.BlockSpec((None, CONV_WIDTH, tc), lambda b, c: (layer, 0, c)),
            vec_spec, w_spec, vec_spec, w_spec, vec_spec, vec_spec,
        ],
        out_specs=[
            pl.BlockSpec((None, SEQ, tc), lambda b, c: (b, 0, c)),
            pl.BlockSpec((None, CONV_WIDTH - 1, tc), lambda b, c: (b, 0, c)),
            pl.BlockSpec((None, 1, tc), lambda b, c: (b, 0, c)),
        ],
        out_shape=[
            jax.ShapeDtypeStruct((BATCH, SEQ, LRU_WIDTH), BF16),
            jax.ShapeDtypeStruct((BATCH, CONV_WIDTH - 1, LRU_WIDTH), F32),
            jax.ShapeDtypeStruct((BATCH, 1, LRU_WIDTH), F32),
        ],
        scratch_shapes=[
            pltpu.VMEM((SEQ, tc), F32),
            pltpu.VMEM((SEQ, tc), F32),
        ],
        compiler_params=_cparams(2),
        name="lru_prompt",
    )(proj3, proj3, conv_w, conv_b, w_r, b_r, w_i, b_i, lam)


def _mix_kernel(att_ref, lru_ref, ga_ref, gl_ref, wpa_ref, wpb_ref, o_ref):
    y_att = jnp.dot(att_ref[...], wpa_ref[...], preferred_element_type=F32)
    y_lru = jnp.dot(lru_ref[...], wpb_ref[...], preferred_element_type=F32)
    merged = _sigmoid(ga_ref[...]) * y_att + _sigmoid(gl_ref[...]) * y_lru
    o_ref[...] = merged.astype(o_ref.dtype)


def _mix(att, lru, proj, wpa, wpb, tm):
    m_rows = att.shape[0]
    return pl.pallas_call(
        _mix_kernel,
        grid=(m_rows // tm,),
        in_specs=[
            pl.BlockSpec((tm, ATT_WIDTH), lambda i: (i, 0)),
            pl.BlockSpec((tm, LRU_WIDTH), lambda i: (i, 0)),
            pl.BlockSpec((tm, D_MODEL), lambda i: (i, COL_GATT // D_MODEL)),
            pl.BlockSpec((tm, D_MODEL), lambda i: (i, COL_GLRU // D_MODEL)),
            pl.BlockSpec((ATT_WIDTH, D_MODEL), lambda i: (0, 0),
                         pipeline_mode=pl.Buffered(1)),
            pl.BlockSpec((LRU_WIDTH, D_MODEL), lambda i: (0, 0),
                         pipeline_mode=pl.Buffered(1)),
        ],
        out_specs=pl.BlockSpec((tm, D_MODEL), lambda i: (i, 0)),
        out_shape=jax.ShapeDtypeStruct((m_rows, D_MODEL), BF16),
        compiler_params=_cparams(1),
        name="branch_mix",
    )(att, lru, proj, proj, wpa, wpb)


def _residual_kernel(m_ref, w_ref, x_ref, gate_ref, g_ref, *rest, last):
    out = jnp.dot(m_ref[...], w_ref[...].astype(BF16), preferred_element_type=F32)
    x_new = x_ref[...] + gate_ref[...] * out
    y = x_new * lax.rsqrt(jnp.mean(x_new * x_new, axis=-1, keepdims=True) + RMS_EPS)
    y = y * g_ref[...]
    if last:
        (y_ref,) = rest
        y_ref[...] = y
    else:
        sc_ref, sh_ref, xo_ref, ho_ref = rest
        xo_ref[...] = x_new
        ho_ref[...] = (y * (1.0 + sc_ref[...]) + sh_ref[...]).astype(ho_ref.dtype)


def _residual_prompt(merged, w_out, x, gate, g, scale, shift, layer):
    tm = 512
    last = scale is None
    row_spec = pl.BlockSpec((None, tm, D_MODEL), lambda b, i: (b, i, 0))
    vec_spec = pl.BlockSpec((None, 1, D_MODEL), lambda b, i: (b, 0, 0))
    in_specs = [
        row_spec,
        pl.BlockSpec((None, D_MODEL, D_MODEL), lambda b, i: (layer, 0, 0),
                     pipeline_mode=pl.Buffered(1)),
        row_spec,
        vec_spec,
        pl.BlockSpec((1, D_MODEL), lambda b, i: (0, 0)),
    ]
    args = [merged, w_out, x, gate, g.reshape(1, D_MODEL)]
    if last:
        out_specs = row_spec
        out_shape = jax.ShapeDtypeStruct(x.shape, F32)
    else:
        in_specs += [vec_spec, vec_spec]
        args += [scale, shift]
        out_specs = [row_spec, row_spec]
        out_shape = [jax.ShapeDtypeStruct(x.shape, F32), jax.ShapeDtypeStruct(x.shape, BF16)]
    return pl.pallas_call(
        functools.partial(_residual_kernel, last=last),
        grid=(BATCH, SEQ // tm),
        in_specs=in_specs,
        out_specs=out_specs,
        out_shape=out_shape,
        compiler_params=_cparams(2),
        name="residual_prompt",
    )(*args)


def _residual_sample(merged, w_out, x, gate_rows, g, scale_rows, shift_rows, layer):
    last = scale_rows is None
    full = pl.BlockSpec((N_SAMPLE_ROWS, D_MODEL), lambda i: (0, 0))
    in_specs = [full, pl.BlockSpec((None, D_MODEL, D_MODEL), lambda i: (layer, 0, 0)), full, full,
                pl.BlockSpec((1, D_MODEL), lambda i: (0, 0))]
    args = [merged, w_out, x, gate_rows, g.reshape(1, D_MODEL)]
    if last:
        out_specs = full
        out_shape = jax.ShapeDtypeStruct(x.shape, F32)
    else:
        in_specs += [full, full]
        args += [scale_rows, shift_rows]
        out_specs = [full, full]
        out_shape = [jax.ShapeDtypeStruct(x.shape, F32), jax.ShapeDtypeStruct(x.shape, BF16)]
    return pl.pallas_call(
        functools.partial(_residual_kernel, last=last),
        grid=(1,),
        in_specs=in_specs,
        out_specs=out_specs,
        out_shape=out_shape,
        compiler_params=_cparams(1),
        name="residual_sample",
    )(*args)


CACHE_CHUNK_ROWS = 512
CACHE_SLOTS = 6
CACHE_LOOKAHEAD = 3


def _cache_update_kernel(c0, c1, c2, n0, n1, n2, o0, o1, o2, buf, in_sem, out_sem, new_sem):
    chunks = []
    new_copies = []
    for gi, (c, n, o) in enumerate(((c0, n0, o0), (c1, n1, o1), (c2, n2, o2))):
        keep = c.shape[2] - DEC_SEQ
        for l in range(DEPTH):
            new_copies.append(pltpu.make_async_copy(
                n.at[l], o.at[l, :, pl.ds(keep, DEC_SEQ)], new_sem.at[gi, l]))
            for b in range(DEC_BATCH):
                for r in range(0, keep, CACHE_CHUNK_ROWS):
                    rows = min(CACHE_CHUNK_ROWS, keep - r)
                    chunks.append((c.at[l, b, pl.ds(DEC_SEQ + r, rows)],
                                   o.at[l, b, pl.ds(r, rows)], rows))

    def read(i):
        src, _, rows = chunks[i]
        slot = i % CACHE_SLOTS
        return pltpu.make_async_copy(src, buf.at[slot, pl.ds(0, rows)], in_sem.at[slot])

    def write(i):
        _, dst, rows = chunks[i]
        slot = i % CACHE_SLOTS
        return pltpu.make_async_copy(buf.at[slot, pl.ds(0, rows)], dst, out_sem.at[slot])

    for cp in new_copies:
        cp.start()
    n_chunks = len(chunks)
    for i in range(n_chunks + CACHE_LOOKAHEAD):
        if i < n_chunks:
            if i >= CACHE_SLOTS:
                write(i - CACHE_SLOTS).wait()
            read(i).start()
        j = i - CACHE_LOOKAHEAD
        if j >= 0:
            read(j).wait()
            write(j).start()
    for j in range(max(n_chunks - CACHE_SLOTS, 0), n_chunks):
        write(j).wait()
    for cp in new_copies:
        cp.wait()


def _cache_update(caches, new_rows):
    any_spec = pl.BlockSpec(memory_space=pl.ANY)
    vmem_spec = pl.BlockSpec(memory_space=pltpu.VMEM)
    return pl.pallas_call(
        _cache_update_kernel,
        in_specs=[any_spec] * 3 + [vmem_spec] * 3,
        out_specs=[any_spec] * 3,
        out_shape=[jax.ShapeDtypeStruct(c.shape, c.dtype) for c in caches],
        scratch_shapes=[
            pltpu.VMEM((CACHE_SLOTS, CACHE_CHUNK_ROWS, 2, H_PER_GROUP, HEAD_DIM), F32),
            pltpu.SemaphoreType.DMA((CACHE_SLOTS,)),
            pltpu.SemaphoreType.DMA((CACHE_SLOTS,)),
            pltpu.SemaphoreType.DMA((N_GROUPS, DEPTH)),
        ],
        compiler_params=pltpu.CompilerParams(vmem_limit_bytes=VMEM_LIMIT),
        name="cache_update",
    )(*caches, *new_rows)


def _attn_sample_kernel(q_ref, k_ref, v_ref, z_ref, c0_ref, c1_ref, c2_ref, bc_ref, bn_ref,
                        o_ref, kv0_ref, kv1_ref, kv2_ref):
    hp = H_PER_GROUP
    kv_refs = (kv0_ref, kv1_ref, kv2_ref)
    k_new, v_new = [], []
    for g in range(N_GROUPS):
        kn = k_ref[:, g * hp:(g + 1) * hp, :]
        vn = v_ref[:, g * hp:(g + 1) * hp, :]
        kv_refs[g][:, 0] = kn
        kv_refs[g][:, 1] = vn
        k_new.append(kn)
        v_new.append(vn)

    for s in range(DEC_SEQ):
        accs, ms, ss = [], [], []
        for g in range(N_GROUPS):
            qg = q_ref[s, g * hp:(g + 1) * hp, :]
            if g == 0:
                kc, vc = c0_ref[:, 0], c0_ref[:, 1]
            elif g == 1:
                kc, vc = c1_ref[:, s, 0], c1_ref[:, s, 1]
            else:
                kc, vc = c2_ref[:, s, 0], c2_ref[:, s, 1]
            lc = jnp.sum(kc * qg[None], axis=-1, keepdims=True) * ATT_SCALE + bc_ref[g, s]
            ln = jnp.sum(k_new[g] * qg[None], axis=-1, keepdims=True) * ATT_SCALE + bn_ref[g, s]
            m = jnp.maximum(jnp.max(lc, axis=0), jnp.max(ln, axis=0))
            pc = jnp.exp(lc - m[None])
            pn = jnp.exp(ln - m[None])
            ss.append(jnp.sum(pc, axis=0) + jnp.sum(pn, axis=0))
            accs.append(jnp.sum(pc * vc, axis=0) + jnp.sum(pn * v_new[g], axis=0))
            ms.append(m)
        mm = jnp.maximum(jnp.maximum(ms[0], ms[1]), ms[2])
        ws = [jnp.exp(m - mm) for m in ms]
        num = ws[0] * accs[0] + ws[1] * accs[1] + ws[2] * accs[2]
        den = ws[0] * ss[0] + ws[1] * ss[1] + ws[2] * ss[2]
        o_ref[s] = (num / den) * _silu(z_ref[s])


def _attn_sample(proj_s4, caches, bias_c, bias_n, layer):
    c0, c1, c2 = caches
    hp = H_PER_GROUP

    def head_spec(col0, n_heads):
        return pl.BlockSpec((None, DEC_SEQ, n_heads, HEAD_DIM),
                            lambda b, c=col0 // (HEAD_DIM * n_heads): (b, 0, c, 0))

    in_specs = [
        head_spec(COL_Q, N_ATT_HEADS), head_spec(COL_K, N_ATT_HEADS), head_spec(COL_V, N_ATT_HEADS),
        head_spec(COL_ZATT, hp),
        pl.BlockSpec((None, None, BLK, 2, hp, HEAD_DIM), lambda b: (layer, b, 0, 0, 0, 0)),
        pl.BlockSpec((None, None, BLK, DEC_SEQ, 2, hp, HEAD_DIM), lambda b: (layer, b, 0, 0, 0, 0, 0)),
        pl.BlockSpec((None, None, BLK, DEC_SEQ, 2, hp, HEAD_DIM), lambda b: (layer, b, 0, 0, 0, 0, 0)),
        pl.BlockSpec(bias_c.shape, lambda b: (0,) * 5),
        pl.BlockSpec(bias_n.shape, lambda b: (0,) * 5),
    ]
    out_specs = [pl.BlockSpec((None, DEC_SEQ, hp, HEAD_DIM), lambda b: (b, 0, 0, 0))]
    out_shape = [jax.ShapeDtypeStruct((DEC_BATCH, DEC_SEQ, hp, HEAD_DIM), F32)]
    for _ in range(N_GROUPS):
        out_specs.append(pl.BlockSpec((None, DEC_SEQ, 2, hp, HEAD_DIM), lambda b: (b, 0, 0, 0, 0)))
        out_shape.append(jax.ShapeDtypeStruct((DEC_BATCH, DEC_SEQ, 2, hp, HEAD_DIM), F32))
    return pl.pallas_call(
        _attn_sample_kernel,
        grid=(DEC_BATCH,),
        in_specs=in_specs,
        out_specs=out_specs,
        out_shape=out_shape,
        compiler_params=_cparams(1),
        name="attn_sample",
    )(proj_s4, proj_s4, proj_s4, proj_s4, c0, c1, c2, bias_c, bias_n)


def _sample_bias(rel_bias):
    s_idx = np.arange(DEC_SEQ)
    bc, bn = [], []
    for g, (win, dil) in enumerate(ATT_GROUPS):
        tbl = rel_bias[:, g * H_PER_GROUP:(g + 1) * H_PER_GROUP].astype(F32)
        key = np.arange(BLK)
        if dil == 1:
            j = BLK + s_idx[:, None] - key[None, :]
            valid = j <= BLK
        else:
            j = np.broadcast_to(BLK - key[None, :], (DEC_SEQ, BLK))
            valid = np.ones_like(j, dtype=bool)
        b = _lookup(tbl, _t5_bucket(np.clip(j, 0, None) * dil))
        bc.append(jnp.where(valid[:, :, None], b, NEG))
        jn = s_idx[:, None] - s_idx[None, :]
        valid_n = (jn >= 0) & (jn * dil <= win) & ((jn == 0) | (dil == 1))
        b = _lookup(tbl, _t5_bucket(np.clip(jn, 0, None) * dil))
        bn.append(jnp.where(valid_n[:, :, None], b, NEG))
    bc = jnp.stack(bc)
    bn = jnp.stack(bn)
    bc = jnp.broadcast_to(bc[..., None], bc.shape + (HEAD_DIM,))
    bn = jnp.broadcast_to(bn[..., None], bn.shape + (HEAD_DIM,))
    return bc, bn


def _lru_sample_kernel(x_ref, z_ref, cs_ref, h0_ref, cw_ref, cb_ref, wr_ref, br_ref, wi_ref, bi_ref,
                       lam_ref, o_ref, conv_ref, hl_ref):
    S = DEC_SEQ
    xp = [cs_ref[t] for t in range(CONV_WIDTH - 1)] + [x_ref[t] for t in range(S)]
    xc = []
    for t in range(S):
        y = xp[t] * cw_ref[0:1, :]
        for tap in range(1, CONV_WIDTH):
            y = y + xp[t + tap] * cw_ref[tap:tap + 1, :]
        xc.append(y + cb_ref[...])
    xcat = jnp.concatenate(xc, axis=0)
    nsp = _softplus(-lam_ref[...])
    a_parts, u_parts = _lru_gates(xcat, wr_ref, br_ref[...], wi_ref, bi_ref[...], nsp, LRU_BLOCKS)
    a = jnp.concatenate(a_parts, axis=1)
    u = jnp.concatenate(u_parts, axis=1)
    nb = DEC_BATCH
    h = h0_ref[...]
    for t in range(S):
        h = a[t * nb:(t + 1) * nb] * h + u[t * nb:(t + 1) * nb]
        o_ref[t] = h * _silu(z_ref[t])
    for t in range(CONV_WIDTH - 1):
        conv_ref[t] = xp[S + t]
    hl_ref[...] = h


def _lru_sample(x_t, z_t, conv_t, h0, conv_w, conv_b, w_r, b_r, w_i, b_i, lam, layer):
    C = LRU_WIDTH

    def full(shape):
        return pl.BlockSpec(shape, lambda i: (0,) * len(shape))

    vec_spec = pl.BlockSpec((None, 1, C), lambda i: (layer, 0, 0))
    w_spec = pl.BlockSpec((None, LRU_BLOCKS, LRU_BLOCK, LRU_BLOCK), lambda i: (layer, 0, 0, 0))
    return pl.pallas_call(
        _lru_sample_kernel,
        grid=(1,),
        in_specs=[
            full((DEC_SEQ, DEC_BATCH, C)), full((DEC_SEQ, DEC_BATCH, C)),
            full((CONV_WIDTH - 1, DEC_BATCH, C)), full((DEC_BATCH, C)),
            pl.BlockSpec((None, CONV_WIDTH, C), lambda i: (layer, 0, 0)),
            vec_spec, w_spec, vec_spec, w_spec, vec_spec, vec_spec,
        ],
        out_specs=[full((DEC_SEQ, DEC_BATCH, C)), full((CONV_WIDTH - 1, DEC_BATCH, C)),
                   full((DEC_BATCH, C))],
        out_shape=[
            jax.ShapeDtypeStruct((DEC_SEQ, DEC_BATCH, C), F32),
            jax.ShapeDtypeStruct((CONV_WIDTH - 1, DEC_BATCH, C), F32),
            jax.ShapeDtypeStruct((DEC_BATCH, C), F32),
        ],
        compiler_params=_cparams(1),
        name="lru_sample",
    )(x_t, z_t, conv_t, h0, conv_w, conv_b, w_r, b_r, w_i, b_i, lam)


def kernel(x_prompt, x_sample, c_prompt, c_sample, cache_kv_g0, cache_kv_g1, cache_kv_g2, state_conv, state_h, rel_bias, w_ada, b_ada, norm_g, w_in, conv_w, conv_b, w_r, b_r, w_i, b_i, lam, w_pa, w_pb, w_out, final_g):
    L, B, T, D = DEPTH, BATCH, SEQ, D_MODEL
    Bd, S = DEC_BATCH, DEC_SEQ

    c_all = jnp.concatenate(
        [c_prompt, c_sample, jnp.zeros((MOD_ROWS - B - Bd, D), F32)], axis=0)
    mod = _modulation(c_all, w_ada, b_ada).reshape(L, MOD_ROWS, 3, D)
    mod_p = mod[:, :B]
    mod_s = jnp.repeat(mod[:, B:B + Bd], S, axis=1)

    conv_b3 = conv_b.reshape(L, 1, LRU_WIDTH)
    b_r3 = b_r.reshape(L, 1, LRU_WIDTH)
    b_i3 = b_i.reshape(L, 1, LRU_WIDTH)
    lam3 = lam.reshape(L, 1, LRU_WIDTH)

    bias_p = _prompt_bias(rel_bias)
    bias_c, bias_n = _sample_bias(rel_bias)

    cache_views = (
        cache_kv_g0,
        cache_kv_g1.reshape(L, Bd, BLK, 4, 2, H_PER_GROUP, HEAD_DIM),
        cache_kv_g2.reshape(L, Bd, BLK, 16, 2, H_PER_GROUP, HEAD_DIM),
    )

    xp = x_prompt
    xs = x_sample.reshape(Bd * S, D)
    kv_p = None
    kv_new = ([], [], [])
    conv_p, h_p, conv_s, h_s = [], [], [], []

    def mods(l):
        shift_p, scale_p, gate_p = (mod_p[l, :, i][:, None, :] for i in range(3))
        shift_s, scale_s, gate_s = (mod_s[l, :, i] for i in range(3))
        return (shift_p, scale_p, gate_p), (shift_s, scale_s, gate_s)

    (shift_p, scale_p, gate_p), (shift_s, scale_s, gate_s) = mods(0)
    hp_ = _norm_prompt(xp, norm_g[0], scale_p, shift_p)
    hs_ = _norm_sample(xs, norm_g[0], scale_s, shift_s)
    for l in range(L):
        proj_p, proj_s, *kv_p, wpa_bf, wpb_bf = _in_proj(hp_.reshape(B * T, D), hs_, w_in,
                                                         w_pa, w_pb, kv_p, l)
        proj3 = proj_p.reshape(B, T, IN_COLS)

        att_p = _attn_prompt(proj3, bias_p)
        lru_p, cv, hl = _lru_prompt(proj3, conv_w, conv_b3, w_r, b_r3, w_i, b_i3, lam3, l)
        merged_p = _mix(att_p.reshape(B * T, ATT_WIDTH), lru_p.reshape(B * T, LRU_WIDTH),
                        proj_p, wpa_bf, wpb_bf, 512)
        last = l == L - 1
        if not last:
            (shift_pn, scale_pn, gate_pn), (shift_sn, scale_sn, gate_sn) = mods(l + 1)
            xp, hp_ = _residual_prompt(merged_p.reshape(B, T, D), w_out, xp, gate_p,
                                       norm_g[l + 1], scale_pn, shift_pn, l)
        else:
            y_prompt = _residual_prompt(merged_p.reshape(B, T, D), w_out, xp, gate_p,
                                        final_g, None, None, l)
        conv_p.append(cv)
        h_p.append(hl.reshape(B, LRU_WIDTH))

        proj_s4 = proj_s.reshape(Bd, S, IN_COLS // HEAD_DIM, HEAD_DIM)
        outs = _attn_sample(proj_s4, cache_views, bias_c, bias_n, l)
        att_s = outs[0]
        for g in range(N_GROUPS):
            kv_new[g].append(outs[1 + g])
        ps3 = proj_s.reshape(Bd, S, IN_COLS)
        x_t = jnp.transpose(ps3[:, :, COL_XLRU:COL_XLRU + LRU_WIDTH], (1, 0, 2))
        z_t = jnp.transpose(ps3[:, :, COL_ZLRU:COL_ZLRU + LRU_WIDTH], (1, 0, 2))
        conv_t = jnp.transpose(state_conv[l], (1, 0, 2))
        lru_t, cv_t, hl_s = _lru_sample(x_t, z_t, conv_t, state_h[l], conv_w, conv_b3,
                                        w_r, b_r3, w_i, b_i3, lam3, l)
        lru_s = jnp.transpose(lru_t, (1, 0, 2)).reshape(Bd * S, LRU_WIDTH).astype(BF16)
        merged_s = _mix(att_s.reshape(Bd * S, ATT_WIDTH).astype(BF16), lru_s, proj_s,
                        wpa_bf, wpb_bf, Bd * S)
        if not last:
            xs, hs_ = _residual_sample(merged_s, w_out, xs, gate_s,
                                       norm_g[l + 1], scale_sn, shift_sn, l)
            gate_p, gate_s = gate_pn, gate_sn
        else:
            y_sample = _residual_sample(merged_s, w_out, xs, gate_s,
                                        final_g, None, None, l).reshape(Bd, S, D)
        conv_s.append(jnp.transpose(cv_t, (1, 0, 2)))
        h_s.append(hl_s)

    kv_s = _cache_update((cache_kv_g0, cache_kv_g1, cache_kv_g2),
                         tuple(jnp.stack(rows) for rows in kv_new))
    return (y_prompt, y_sample,
            kv_p[0], kv_p[1], kv_p[2],
            jnp.stack(conv_p), jnp.stack(h_p),
            kv_s[0], kv_s[1], kv_s[2],
            jnp.stack(conv_s), jnp.stack(h_s))
```

```python
import functools
import math

import jax
import jax.numpy as jnp
import numpy as np
from jax import lax
from jax.experimental import pallas as pl
from jax.experimental.pallas import tpu as pltpu

D_MODEL = 2048
BATCH = 4
SEQ = 2048
DEPTH = 4
DEC_BATCH = 8
DEC_SEQ = 4
HEAD_DIM = 128
H_PER_GROUP = 8
ATT_GROUPS = ((128, 1), (512, 4), (2048, 16))
N_GROUPS = 3
N_ATT_HEADS = N_GROUPS * H_PER_GROUP
QKV_WIDTH = N_ATT_HEADS * HEAD_DIM
ATT_WIDTH = H_PER_GROUP * HEAD_DIM
N_BUCKETS = 32
REL_MAX_DIST = 2048
BLK = 128
LRU_WIDTH = D_MODEL
LRU_BLOCKS = 16
LRU_BLOCK = LRU_WIDTH // LRU_BLOCKS
CONV_WIDTH = 4
LRU_C = 8.0
IN_COLS = 3 * QKV_WIDTH + ATT_WIDTH + 2 * LRU_WIDTH + 2 * D_MODEL
RMS_EPS = 1e-6
NEG = -1e30
ATT_SCALE = HEAD_DIM ** -0.5

COL_Q = 0
COL_K = QKV_WIDTH
COL_V = 2 * QKV_WIDTH
COL_ZATT = 3 * QKV_WIDTH
COL_XLRU = COL_ZATT + ATT_WIDTH
COL_ZLRU = COL_XLRU + LRU_WIDTH
COL_GATT = COL_ZLRU + LRU_WIDTH
COL_GLRU = COL_GATT + D_MODEL

N_PROMPT_ROWS = BATCH * SEQ
N_SAMPLE_ROWS = DEC_BATCH * DEC_SEQ
MOD_ROWS = 16

VMEM_LIMIT = 52 * 1024 * 1024
ATT_ILP = 16
ATT_ILP_MERGE = 16

F32 = jnp.float32
BF16 = jnp.bfloat16


def _cparams(n_grid_dims, vmem_limit=VMEM_LIMIT):
    return pltpu.CompilerParams(
        dimension_semantics=("arbitrary",) * n_grid_dims,
        vmem_limit_bytes=vmem_limit)


def _sigmoid(x):
    return 0.5 * jnp.tanh(0.5 * x) + 0.5


def _silu(x):
    half = 0.5 * x
    return half * (jnp.tanh(half) + 1.0)


def _t5_bucket(dist):
    dist = np.asarray(dist).astype(np.int32)
    max_exact = N_BUCKETS // 2
    safe = np.maximum(dist, 1).astype(np.float32)
    large = max_exact + (np.log(safe / max_exact) / np.float32(math.log(REL_MAX_DIST / max_exact))
                         * (N_BUCKETS - max_exact)).astype(np.int32)
    large = np.minimum(large, N_BUCKETS - 1)
    return np.where(dist < max_exact, dist, large).astype(np.int32)


def _mod_kernel(c_ref, w_ref, b_ref, o_ref):
    c = _silu(c_ref[...])
    o_ref[...] = jnp.dot(c.astype(BF16), w_ref[...].astype(BF16),
                         preferred_element_type=F32) + b_ref[...]


def _modulation(c_all, w_ada, b_ada):
    tn = 1024
    n_cols = 3 * D_MODEL
    return pl.pallas_call(
        _mod_kernel,
        grid=(DEPTH, n_cols // tn),
        in_specs=[
            pl.BlockSpec((MOD_ROWS, D_MODEL), lambda l, j: (0, 0)),
            pl.BlockSpec((None, D_MODEL, tn), lambda l, j: (l, 0, j)),
            pl.BlockSpec((None, 1, tn), lambda l, j: (l, 0, j)),
        ],
        out_specs=pl.BlockSpec((None, MOD_ROWS, tn), lambda l, j: (l, 0, j)),
        out_shape=jax.ShapeDtypeStruct((DEPTH, MOD_ROWS, n_cols), F32),
        compiler_params=_cparams(2),
        name="adaln_mod",
    )(c_all, w_ada, b_ada.reshape(DEPTH, 1, n_cols))


def _norm_kernel(x_ref, g_ref, sc_ref, sh_ref, o_ref):
    x = x_ref[...]
    y = x * lax.rsqrt(jnp.mean(x * x, axis=-1, keepdims=True) + RMS_EPS)
    y = y * g_ref[...]
    o_ref[...] = (y * (1.0 + sc_ref[...]) + sh_ref[...]).astype(o_ref.dtype)


def _norm_prompt(x, g, scale, shift):
    tm = 512
    row_spec = pl.BlockSpec((None, tm, D_MODEL), lambda b, i: (b, i, 0))
    vec_spec = pl.BlockSpec((None, 1, D_MODEL), lambda b, i: (b, 0, 0))
    return pl.pallas_call(
        _norm_kernel,
        grid=(BATCH, SEQ // tm),
        in_specs=[row_spec, pl.BlockSpec((1, D_MODEL), lambda b, i: (0, 0)), vec_spec, vec_spec],
        out_specs=row_spec,
        out_shape=jax.ShapeDtypeStruct(x.shape, BF16),
        compiler_params=_cparams(2),
        name="rmsnorm_prompt",
    )(x, g.reshape(1, D_MODEL), scale, shift)


def _norm_sample(x, g, scale_rows, shift_rows):
    return pl.pallas_call(
        _norm_kernel,
        out_shape=jax.ShapeDtypeStruct(x.shape, BF16),
        name="rmsnorm_sample",
    )(x, g.reshape(1, D_MODEL), scale_rows, shift_rows)


IN_PROJ_VMEM_LIMIT = 58 * 1024 * 1024
IN_TM = 1024
IN_TN = ATT_WIDTH
IN_TILES_PER_SEQ = SEQ // IN_TM
N_ROW_TILES = N_PROMPT_ROWS // IN_TM
K_TILE0 = COL_K // IN_TN
V_TILE0 = COL_V // IN_TN


def _kv_rows(g):
    w = min(ATT_GROUPS[g][0], SEQ)
    assert w % IN_TM == 0 or IN_TM % w == 0
    return min(w, IN_TM)


def _kv_blocks_per_layer(g):
    n_wblk = min(ATT_GROUPS[g][0], SEQ) // _kv_rows(g)
    return BATCH * n_wblk * 2, n_wblk


WCAST_ROWS = 32
N_WPA_CHUNKS = ATT_WIDTH // WCAST_ROWS
N_WPB_CHUNKS = LRU_WIDTH // WCAST_ROWS


def _wcast_chunk(first, count):
    def index(j, i):
        return jnp.clip(j * N_ROW_TILES + i - first, 0, count - 1)
    return index


def _in_proj_kernel(h_ref, hs_ref, w_ref, wpa_ref, wpb_ref, *rest, creates_kv):
    if not creates_kv:
        rest = rest[N_GROUPS:]
    o_ref, os_ref, kv0_ref, kv1_ref, kv2_ref, wpa_o, wpb_o, wbf_ref = rest
    j = pl.program_id(0)
    i = pl.program_id(1)

    t_step = j * N_ROW_TILES + i

    @pl.when(t_step < N_WPA_CHUNKS)
    def _():
        wpa_o[...] = wpa_ref[...].astype(BF16)

    @pl.when((t_step >= N_WPA_CHUNKS) & (t_step < N_WPA_CHUNKS + N_WPB_CHUNKS))
    def _():
        wpb_o[...] = wpb_ref[...].astype(BF16)

    def kv_layout(kv_ref, rows, src):
        flat = kv_ref.reshape(rows * H_PER_GROUP, HEAD_DIM)
        for h in range(H_PER_GROUP):
            flat[pl.ds(h, rows, stride=H_PER_GROUP), :] = src[IN_TM - rows:, h * HEAD_DIM:(h + 1) * HEAD_DIM]

    full_groups = [g for g in range(N_GROUPS) if min(ATT_GROUPS[g][0], SEQ) >= SEQ]
    assert len(full_groups) == 1
    gf = full_groups[0]
    kv_refs = (kv0_ref, kv1_ref, kv2_ref)
    is_full_kv = (j == K_TILE0 + gf) | (j == V_TILE0 + gf)

    def project(weights, with_kv):
        res = jnp.dot(h_ref[...], weights, preferred_element_type=F32)
        o_ref[...] = res
        if with_kv:
            kv_layout(kv_refs[gf], _kv_rows(gf), res)

    for with_kv in (False, True):
        kv_cond = is_full_kv if with_kv else jnp.logical_not(is_full_kv)

        @pl.when((i == 0) & kv_cond)
        def _(with_kv=with_kv):
            wb = w_ref[...].astype(BF16)
            wbf_ref[...] = wb
            project(wb, with_kv)
            os_ref[...] = jnp.dot(hs_ref[...], wb, preferred_element_type=F32)

        @pl.when((i != 0) & kv_cond)
        def _(with_kv=with_kv):
            project(wbf_ref[...], with_kv)

    for g, kv_ref in enumerate(kv_refs):
        rows = _kv_rows(g)
        if g != gf:
            is_kv = (j == K_TILE0 + g) | (j == V_TILE0 + g)
            is_kv = is_kv & (i % IN_TILES_PER_SEQ == IN_TILES_PER_SEQ - 1)

            @pl.when(is_kv)
            def _(kv_ref=kv_ref, rows=rows):
                kv_layout(kv_ref, rows, o_ref)

        if creates_kv:
            per_layer, _ = _kv_blocks_per_layer(g)
            t = (j - (V_TILE0 + g) - 1) * N_ROW_TILES + i

            @pl.when((j > V_TILE0 + g) & (t < (DEPTH - 1) * per_layer))
            def _(kv_ref=kv_ref):
                kv_ref[...] = jnp.zeros(kv_ref.shape, kv_ref.dtype)


def _kv_index_map(g, layer, creates_kv):
    w = min(ATT_GROUPS[g][0], SEQ)
    full = w >= SEQ
    per_layer, n_wblk = _kv_blocks_per_layer(g)

    def index_map(j, i):
        k_col = K_TILE0 + g
        v_col = V_TILE0 + g
        c = jnp.where(j >= v_col, 1, 0)
        in_col = (j == k_col) | (j == v_col)
        if full:
            ii = jnp.where(in_col, i, jnp.where(j < k_col, 0, N_ROW_TILES - 1))
            b, wblk = ii // IN_TILES_PER_SEQ, ii % IN_TILES_PER_SEQ
        else:
            b_col = jnp.maximum(i - (IN_TILES_PER_SEQ - 1), 0) // IN_TILES_PER_SEQ
            b = jnp.where(in_col, b_col, jnp.where(j < k_col, 0, BATCH - 1))
            wblk = 0
        if not creates_kv:
            return (layer, b, wblk, c, 0, 0)
        assert layer == 0 and (IN_COLS // IN_TN - v_col - 1) * N_ROW_TILES >= (DEPTH - 1) * per_layer
        f = jnp.clip((j - v_col - 1) * N_ROW_TILES + i, 0, (DEPTH - 1) * per_layer - 1)
        rest = f % per_layer
        fill = j > v_col
        return (jnp.where(fill, 1 + f // per_layer, 0),
                jnp.where(fill, rest // (2 * n_wblk), b),
                jnp.where(fill, (rest // 2) % n_wblk, wblk),
                jnp.where(fill, rest % 2, c), 0, 0)

    return index_map


def _in_proj(h_p, h_s, w_in, w_pa, w_pb, kv_stacks, layer):
    tm, tn = IN_TM, IN_TN
    creates_kv = kv_stacks is None
    if creates_kv:
        kv_shapes = [jax.ShapeDtypeStruct((DEPTH, BATCH, min(win, SEQ), 2, H_PER_GROUP, HEAD_DIM), F32)
                     for win, _ in ATT_GROUPS]
        kv_stacks = ()
    else:
        kv_shapes = [jax.ShapeDtypeStruct(kv.shape, kv.dtype) for kv in kv_stacks]
    out_specs = [
        pl.BlockSpec((tm, tn), lambda j, i: (i, j)),
        pl.BlockSpec((N_SAMPLE_ROWS, tn), lambda j, i: (0, j)),
    ]
    out_shape = [
        jax.ShapeDtypeStruct((N_PROMPT_ROWS, IN_COLS), F32),
        jax.ShapeDtypeStruct((N_SAMPLE_ROWS, IN_COLS), F32),
    ]
    for g, shape in enumerate(kv_shapes):
        out_specs.append(pl.BlockSpec((None, None, _kv_rows(g), None, H_PER_GROUP, HEAD_DIM),
                                      _kv_index_map(g, layer, creates_kv)))
        out_shape.append(shape)
    assert N_WPA_CHUNKS + N_WPB_CHUNKS <= (IN_COLS // tn) * N_ROW_TILES
    pa_chunk = _wcast_chunk(0, N_WPA_CHUNKS)
    pb_chunk = _wcast_chunk(N_WPA_CHUNKS, N_WPB_CHUNKS)
    out_specs += [
        pl.BlockSpec((WCAST_ROWS, D_MODEL), lambda j, i: (pa_chunk(j, i), 0)),
        pl.BlockSpec((WCAST_ROWS, D_MODEL), lambda j, i: (pb_chunk(j, i), 0)),
    ]
    out_shape += [
        jax.ShapeDtypeStruct((ATT_WIDTH, D_MODEL), BF16),
        jax.ShapeDtypeStruct((LRU_WIDTH, D_MODEL), BF16),
    ]
    n_in = 5
    return pl.pallas_call(
        functools.partial(_in_proj_kernel, creates_kv=creates_kv),
        grid=(IN_COLS // tn, N_ROW_TILES),
        in_specs=[
            pl.BlockSpec((tm, D_MODEL), lambda j, i: (i, 0)),
            pl.BlockSpec((N_SAMPLE_ROWS, D_MODEL), lambda j, i: (0, 0)),
            pl.BlockSpec((None, D_MODEL, tn), lambda j, i: (layer, 0, j)),
            pl.BlockSpec((None, WCAST_ROWS, D_MODEL), lambda j, i: (layer, pa_chunk(j, i), 0)),
            pl.BlockSpec((None, WCAST_ROWS, D_MODEL), lambda j, i: (layer, pb_chunk(j, i), 0)),
        ] + [pl.BlockSpec(memory_space=pl.ANY)] * len(kv_stacks),
        out_specs=out_specs,
        out_shape=out_shape,
        input_output_aliases={n_in + g: 2 + g for g in range(len(kv_stacks))},
        scratch_shapes=[pltpu.VMEM((D_MODEL, tn), BF16)],
        compiler_params=_cparams(2, IN_PROJ_VMEM_LIMIT),
        name="in_proj",
    )(h_p, h_s, w_in, w_pa, w_pb, *kv_stacks)


def _attn_prompt_kernel(q0, k0, v0, q1, k1, v1, q2, k2, v2, z_ref, bias_ref, o_ref,
                        acc_s, m_s, s_s):
    nt = (((1,), (1,)), ((), ()))

    def load(ref, base, d):
        if d == 1:
            return ref[pl.ds(base, BLK), :].astype(BF16)
        return ref[pl.ds(base, BLK, stride=d), :].astype(BF16)

    def store(g, base, d, acc, m, s):
        idx = pl.ds(base, BLK, stride=d)
        acc_s[g - 1, idx, :] = acc
        m_s[g - 1, idx, :] = jnp.broadcast_to(m, (BLK, HEAD_DIM))
        s_s[g - 1, idx, :] = jnp.broadcast_to(s, (BLK, HEAD_DIM))

    def finish(base, acc, m, s):
        sl = pl.ds(base, BLK)
        m1, m2 = m_s[0, sl, :], m_s[1, sl, :]
        mm = jnp.maximum(jnp.maximum(m, m1), m2)
        w0, w1, w2 = jnp.exp(m - mm), jnp.exp(m1 - mm), jnp.exp(m2 - mm)
        num = w0 * acc + w1 * acc_s[0, sl, :] + w2 * acc_s[1, sl, :]
        den = w0 * s + w1 * s_s[0, sl, :] + w2 * s_s[1, sl, :]
        o_ref[sl, :] = ((num / den) * _silu(z_ref[sl, :])).astype(o_ref.dtype)

    def block(g, refs, d, base, pbase, has_prev):
        q_ref, k_ref, v_ref = refs
        q = load(q_ref, base, d)
        lc = lax.dot_general(q, load(k_ref, base, d), nt, preferred_element_type=F32)
        lc = lc * ATT_SCALE + bias_ref[g, :, BLK:]
        if has_prev is None:
            return lc, None
        lp = lax.dot_general(q, load(k_ref, pbase, d), nt, preferred_element_type=F32)
        lp = lp * ATT_SCALE + bias_ref[g, :, :BLK]
        return lc, jnp.where(has_prev, lp, NEG)

    def softmax(lc, lp):
        if lp is None:
            m = jnp.max(lc, axis=-1, keepdims=True)
            pc = jnp.exp(lc - m)
            return pc, None, m, jnp.sum(pc, axis=-1, keepdims=True)
        m = jnp.max(jnp.maximum(lc, lp), axis=-1, keepdims=True)
        pc = jnp.exp(lc - m)
        pp = jnp.exp(lp - m)
        return pc, pp, m, jnp.sum(pc + pp, axis=-1, keepdims=True)

    def weighted(refs, d, base, pbase, pc, pp):
        v_ref = refs[2]
        acc = jnp.dot(pc.astype(BF16), load(v_ref, base, d), preferred_element_type=F32)
        if pp is not None:
            acc = acc + jnp.dot(pp.astype(BF16), load(v_ref, pbase, d), preferred_element_type=F32)
        return acc

    groups = ((q0, k0, v0), (q1, k1, v1), (q2, k2, v2))

    def run_blocks(g, d, specs):
        refs = groups[g]
        logits = [block(g, refs, d, base, pbase, has_prev) for base, pbase, has_prev in specs]
        probs = [softmax(lc, lp) for lc, lp in logits]
        accs = [weighted(refs, d, base, pbase, pc, pp)
                for (base, pbase, _), (pc, pp, _, _) in zip(specs, probs)]
        for (base, _, _), acc, (_, _, m, s) in zip(specs, accs, probs):
            if d == 1:
                finish(base, acc, m, s)
            else:
                store(g, base, d, acc, m, s)

    assert ATT_GROUPS[0][1] == 1 and all(d > 1 for _, d in ATT_GROUPS[1:])
    for g in reversed(range(N_GROUPS)):
        d = ATT_GROUPS[g][1]
        span = BLK * d
        n_blocks = SEQ // span
        if n_blocks == 1:
            def body_r(rr, carry, g=g, d=d):
                run_blocks(g, d, [(rr * ATT_ILP + u, None, None) for u in range(ATT_ILP)])
                return carry
            lax.fori_loop(0, d // ATT_ILP, body_r, 0)
        elif d == 1:
            def body_n(nn, carry, g=g, span=span):
                specs = []
                for u in range(ATT_ILP_MERGE):
                    n = nn * ATT_ILP_MERGE + u
                    specs.append((pl.multiple_of(n * span, BLK),
                                  pl.multiple_of(jnp.maximum(n - 1, 0) * span, BLK), n > 0))
                run_blocks(g, 1, specs)
                return carry
            lax.fori_loop(0, n_blocks // ATT_ILP_MERGE, body_n, 0)
        else:
            n_per_iter = ATT_ILP // d
            assert n_per_iter * d == ATT_ILP and n_blocks % n_per_iter == 0
            def body_n(nn, carry, g=g, d=d, span=span, n_per_iter=n_per_iter):
                specs = []
                for u in range(n_per_iter):
                    n = nn * n_per_iter + u
                    specs += [(n * span + r, jnp.maximum(n - 1, 0) * span + r, n > 0)
                              for r in range(d)]
                run_blocks(g, d, specs)
                return carry
            lax.fori_loop(0, n_blocks // n_per_iter, body_n, 0)


def _attn_prompt(proj3, bias_p):
    def head_spec(col0):
        return pl.BlockSpec((None, SEQ, HEAD_DIM),
                            lambda b, h, c=col0 // HEAD_DIM: (b, 0, c + h))

    in_specs = []
    for g in range(N_GROUPS):
        for col in (COL_Q, COL_K, COL_V):
            in_specs.append(head_spec(col + g * ATT_WIDTH))
    in_specs.append(head_spec(COL_ZATT))
    in_specs.append(pl.BlockSpec((N_GROUPS, None, BLK, 2 * BLK), lambda b, h: (0, h, 0, 0)))
    scratch = [pltpu.VMEM((N_GROUPS - 1, SEQ, HEAD_DIM), F32) for _ in range(3)]
    return pl.pallas_call(
        _attn_prompt_kernel,
        grid=(BATCH, H_PER_GROUP),
        in_specs=in_specs,
        out_specs=pl.BlockSpec((None, SEQ, HEAD_DIM), lambda b, h: (b, 0, h)),
        out_shape=jax.ShapeDtypeStruct((BATCH, SEQ, ATT_WIDTH), BF16),
        scratch_shapes=scratch,
        compiler_params=_cparams(2),
        name="attn_prompt",
    )(*([proj3] * 10), bias_p)


def _lookup(tbl, idx):
    idx = np.asarray(idx, np.int32)
    onehot = (jnp.asarray(idx.reshape(-1, 1)) == jnp.arange(tbl.shape[0], dtype=jnp.int32)[None, :])
    out = jnp.dot(onehot.astype(F32), tbl.astype(F32), precision=lax.Precision.HIGHEST)
    return out.reshape(idx.shape + tbl.shape[1:])


def _prompt_bias(rel_bias):
    qi = np.arange(BLK)[:, None]
    ki = np.arange(2 * BLK)[None, :]
    du = qi + BLK - ki
    band = (du >= 0) & (du <= BLK)
    out = []
    for g, (win, dil) in enumerate(ATT_GROUPS):
        assert win // dil == BLK
        tbl = rel_bias[:, g * H_PER_GROUP:(g + 1) * H_PER_GROUP]
        b = _lookup(tbl, _t5_bucket(np.clip(du, 0, None) * dil))
        b = jnp.where(band[:, :, None], b, NEG)
        out.append(jnp.transpose(b, (2, 0, 1)))
    return jnp.stack(out)


def _softplus(x):
    return jnp.maximum(x, 0.0) + jnp.log1p(jnp.exp(-jnp.abs(x)))


def _lru_gates(xc, wr_ref, br, wi_ref, bi, nsp, n_blocks):
    k = (-0.5 * LRU_C) * nsp
    half_br = 0.5 * br
    half_bi = 0.5 * bi
    a_parts, u_parts = [], []
    for n in range(n_blocks):
        sl = slice(n * LRU_BLOCK, (n + 1) * LRU_BLOCK)
        xn = xc[:, sl]
        xb = xn.astype(BF16)
        wr_half = (0.5 * wr_ref[n]).astype(BF16)
        wi_half = (0.5 * wi_ref[n]).astype(BF16)
        t_r = jnp.tanh(jnp.dot(xb, wr_half, preferred_element_type=F32) + half_br[:, sl])
        t_i = jnp.tanh(jnp.dot(xb, wi_half, preferred_element_type=F32) + half_bi[:, sl])
        log_a = k[:, sl] * t_r + k[:, sl]
        a = jnp.exp(log_a)
        one_minus_a2 = jnp.tanh(log_a) * (-1.0 - a * a)
        root = jnp.where(one_minus_a2 > 0.0, one_minus_a2 * lax.rsqrt(one_minus_a2), 0.0)
        a_parts.append(a)
        u_parts.append(root * ((0.5 * xn) * (t_i + 1.0)))
    return a_parts, u_parts


def _scan8(a, u, rows):
    for s in (1, 2, 4):
        keep = rows >= s
        a_sh = pltpu.roll(a, s, 0)
        u_sh = pltpu.roll(u, s, 0)
        u = jnp.where(keep, a * u_sh + u, u)
        a = jnp.where(keep, a * a_sh, a)
    return a, u


def _lru_prompt_kernel(x_ref, z_ref, cw_ref, cb_ref, wr_ref, br_ref, wi_ref, bi_ref, lam_ref,
                       o_ref, conv_ref, hl_ref, a_s, u_s):
    T, C = x_ref.shape
    n_blocks = C // LRU_BLOCK
    R = 256
    pad = 8

    conv_ref[...] = x_ref[T - (CONV_WIDTH - 1):T, :]

    nsp = _softplus(-lam_ref[...])
    br = br_ref[...]
    bi = bi_ref[...]
    cb = cb_ref[...]

    def gates(c, carry, first=False):
        if first:
            r0 = 0
            xw = jnp.concatenate([jnp.zeros((pad, C), F32), x_ref[0:R, :]], axis=0)
        else:
            r0 = pl.multiple_of(c * R, R)
            xw = x_ref[pl.ds(pl.multiple_of(r0 - pad, pad), R + pad), :]
        xc = xw[pad:] * cw_ref[CONV_WIDTH - 1:CONV_WIDTH, :]
        for tap in range(CONV_WIDTH - 1):
            shifted = pltpu.roll(xw, CONV_WIDTH - 1 - tap, 0)[pad:]
            xc = xc + shifted * cw_ref[tap:tap + 1, :]
        xc = xc + cb
        a_parts, u_parts = _lru_gates(xc, wr_ref, br, wi_ref, bi, nsp, n_blocks)
        for n in range(n_blocks):
            sl = slice(n * LRU_BLOCK, (n + 1) * LRU_BLOCK)
            a_s[pl.ds(r0, R), sl] = a_parts[n]
            u_s[pl.ds(r0, R), sl] = u_parts[n]
        return carry

    gates(0, 0, first=True)
    lax.fori_loop(1, T // R, gates, 0)

    rows = lax.broadcasted_iota(jnp.int32, (8, C), 0)

    def scan(c, h_prev):
        t0 = pl.multiple_of(c * 16, 16)
        hs = []
        for half in range(2):
            sl = pl.ds(t0 + 8 * half, 8)
            a, u = _scan8(a_s[sl, :], u_s[sl, :], rows)
            h = a * h_prev + u
            hs.append(h)
            h_prev = h[7:8, :]
        h16 = jnp.concatenate(hs, axis=0)
        z = z_ref[pl.ds(t0, 16), :]
        o_ref[pl.ds(t0, 16), :] = (h16 * _silu(z)).astype(o_ref.dtype)
        return h_prev

    h_last = lax.fori_loop(0, T // 16, scan, jnp.zeros((1, C), F32))
    hl_ref[...] = h_last


def _lru_prompt(proj3, conv_w, conv_b, w_r, b_r, w_i, b_i, lam, layer):
    tc = 512
    nb = tc // LRU_BLOCK
    n_ct = LRU_WIDTH // tc

    def col_spec(col0):
        return pl.BlockSpec((None, SEQ, tc), lambda b, c, c0=col0 // tc: (b, 0, c0 + c))

    vec_spec = pl.BlockSpec((None, 1, tc), lambda b, c: (layer, 0, c))
    w_spec = pl.BlockSpec((None, nb, LRU_BLOCK, LRU_BLOCK), lambda b, c: (layer, c, 0, 0))
    return pl.pallas_call(
        _lru_prompt_kernel,
        grid=(BATCH, n_ct),
        in_specs=[
            col_spec(COL_XLRU), col_spec(COL_ZLRU),
            pl.BlockSpec((None, CONV_WIDTH, tc), lambda b, c: (layer, 0, c)),
            vec_spec, w_spec, vec_spec, w_spec, vec_spec, vec_spec,
        ],
        out_specs=[
            pl.BlockSpec((None, SEQ, tc), lambda b, c: (b, 0, c)),
            pl.BlockSpec((None, CONV_WIDTH - 1, tc), lambda b, c: (b, 0, c)),
            pl.BlockSpec((None, 1, tc), lambda b, c: (b, 0, c)),
        ],
        out_shape=[
            jax.ShapeDtypeStruct((BATCH, SEQ, LRU_WIDTH), BF16),
            jax.ShapeDtypeStruct((BATCH, CONV_WIDTH - 1, LRU_WIDTH), F32),
            jax.ShapeDtypeStruct((BATCH, 1, LRU_WIDTH), F32),
        ],
        scratch_shapes=[
            pltpu.VMEM((SEQ, tc), F32),
            pltpu.VMEM((SEQ, tc), F32),
        ],
        compiler_params=_cparams(2),
        name="lru_prompt",
    )(proj3, proj3, conv_w, conv_b, w_r, b_r, w_i, b_i, lam)


def _mix_kernel(att_ref, lru_ref, ga_ref, gl_ref, wpa_ref, wpb_ref, o_ref):
    y_att = jnp.dot(att_ref[...], wpa_ref[...], preferred_element_type=F32)
    y_lru = jnp.dot(lru_ref[...], wpb_ref[...], preferred_element_type=F32)
    merged = _sigmoid(ga_ref[...]) * y_att + _sigmoid(gl_ref[...]) * y_lru
    o_ref[...] = merged.astype(o_ref.dtype)


def _mix(att, lru, proj, wpa, wpb, tm):
    m_rows = att.shape[0]
    return pl.pallas_call(
        _mix_kernel,
        grid=(m_rows // tm,),
        in_specs=[
            pl.BlockSpec((tm, ATT_WIDTH), lambda i: (i, 0)),
            pl.BlockSpec((tm, LRU_WIDTH), lambda i: (i, 0)),
            pl.BlockSpec((tm, D_MODEL), lambda i: (i, COL_GATT // D_MODEL)),
            pl.BlockSpec((tm, D_MODEL), lambda i: (i, COL_GLRU // D_MODEL)),
            pl.BlockSpec((ATT_WIDTH, D_MODEL), lambda i: (0, 0),
                         pipeline_mode=pl.Buffered(1)),
            pl.BlockSpec((LRU_WIDTH, D_MODEL), lambda i: (0, 0),
                         pipeline_mode=pl.Buffered(1)),
        ],
        out_specs=pl.BlockSpec((tm, D_MODEL), lambda i: (i, 0)),
        out_shape=jax.ShapeDtypeStruct((m_rows, D_MODEL), BF16),
        compiler_params=_cparams(1),
        name="branch_mix",
    )(att, lru, proj, proj, wpa, wpb)


def _residual_kernel(m_ref, w_ref, x_ref, gate_ref, g_ref, *rest, last):
    out = jnp.dot(m_ref[...], w_ref[...].astype(BF16), preferred_element_type=F32)
    x_new = x_ref[...] + gate_ref[...] * out
    y = x_new * lax.rsqrt(jnp.mean(x_new * x_new, axis=-1, keepdims=True) + RMS_EPS)
    y = y * g_ref[...]
    if last:
        (y_ref,) = rest
        y_ref[...] = y
    else:
        sc_ref, sh_ref, xo_ref, ho_ref = rest
        xo_ref[...] = x_new
        ho_ref[...] = (y * (1.0 + sc_ref[...]) + sh_ref[...]).astype(ho_ref.dtype)


def _residual_prompt(merged, w_out, x, gate, g, scale, shift, layer):
    tm = 512
    last = scale is None
    row_spec = pl.BlockSpec((None, tm, D_MODEL), lambda b, i: (b, i, 0))
    vec_spec = pl.BlockSpec((None, 1, D_MODEL), lambda b, i: (b, 0, 0))
    in_specs = [
        row_spec,
        pl.BlockSpec((None, D_MODEL, D_MODEL), lambda b, i: (layer, 0, 0),
                     pipeline_mode=pl.Buffered(1)),
        row_spec,
        vec_spec,
        pl.BlockSpec((1, D_MODEL), lambda b, i: (0, 0)),
    ]
    args = [merged, w_out, x, gate, g.reshape(1, D_MODEL)]
    if last:
        out_specs = row_spec
        out_shape = jax.ShapeDtypeStruct(x.shape, F32)
    else:
        in_specs += [vec_spec, vec_spec]
        args += [scale, shift]
        out_specs = [row_spec, row_spec]
        out_shape = [jax.ShapeDtypeStruct(x.shape, F32), jax.ShapeDtypeStruct(x.shape, BF16)]
    return pl.pallas_call(
        functools.partial(_residual_kernel, last=last),
        grid=(BATCH, SEQ // tm),
        in_specs=in_specs,
        out_specs=out_specs,
        out_shape=out_shape,
        compiler_params=_cparams(2),
        name="residual_prompt",
    )(*args)


def _residual_sample(merged, w_out, x, gate_rows, g, scale_rows, shift_rows, layer):
    last = scale_rows is None
    full = pl.BlockSpec((N_SAMPLE_ROWS, D_MODEL), lambda i: (0, 0))
    in_specs = [full, pl.BlockSpec((None, D_MODEL, D_MODEL), lambda i: (layer, 0, 0)), full, full,
                pl.BlockSpec((1, D_MODEL), lambda i: (0, 0))]
    args = [merged, w_out, x, gate_rows, g.reshape(1, D_MODEL)]
    if last:
        out_specs = full
        out_shape = jax.ShapeDtypeStruct(x.shape, F32)
    else:
        in_specs += [full, full]
        args += [scale_rows, shift_rows]
        out_specs = [full, full]
        out_shape = [jax.ShapeDtypeStruct(x.shape, F32), jax.ShapeDtypeStruct(x.shape, BF16)]
    return pl.pallas_call(
        functools.partial(_residual_kernel, last=last),
        grid=(1,),
        in_specs=in_specs,
        out_specs=out_specs,
        out_shape=out_shape,
        compiler_params=_cparams(1),
        name="residual_sample",
    )(*args)


CACHE_CHUNK_ROWS = 512
CACHE_SLOTS = 6
CACHE_LOOKAHEAD = 3


def _cache_update_kernel(c0, c1, c2, n0, n1, n2, o0, o1, o2, buf, in_sem, out_sem, new_sem):
    chunks = []
    new_copies = []
    for gi, (c, n, o) in enumerate(((c0, n0, o0), (c1, n1, o1), (c2, n2, o2))):
        keep = c.shape[2] - DEC_SEQ
        for l in range(DEPTH):
            new_copies.append(pltpu.make_async_copy(
                n.at[l], o.at[l, :, pl.ds(keep, DEC_SEQ)], new_sem.at[gi, l]))
            for b in range(DEC_BATCH):
                for r in range(0, keep, CACHE_CHUNK_ROWS):
                    rows = min(CACHE_CHUNK_ROWS, keep - r)
                    chunks.append((c.at[l, b, pl.ds(DEC_SEQ + r, rows)],
                                   o.at[l, b, pl.ds(r, rows)], rows))

    def read(i):
        src, _, rows = chunks[i]
        slot = i % CACHE_SLOTS
        return pltpu.make_async_copy(src, buf.at[slot, pl.ds(0, rows)], in_sem.at[slot])

    def write(i):
        _, dst, rows = chunks[i]
        slot = i % CACHE_SLOTS
        return pltpu.make_async_copy(buf.at[slot, pl.ds(0, rows)], dst, out_sem.at[slot])

    for cp in new_copies:
        cp.start()
    n_chunks = len(chunks)
    for i in range(n_chunks + CACHE_LOOKAHEAD):
        if i < n_chunks:
            if i >= CACHE_SLOTS:
                write(i - CACHE_SLOTS).wait()
            read(i).start()
        j = i - CACHE_LOOKAHEAD
        if j >= 0:
            read(j).wait()
            write(j).start()
    for j in range(max(n_chunks - CACHE_SLOTS, 0), n_chunks):
        write(j).wait()
    for cp in new_copies:
        cp.wait()


def _cache_update(caches, new_rows):
    any_spec = pl.BlockSpec(memory_space=pl.ANY)
    vmem_spec = pl.BlockSpec(memory_space=pltpu.VMEM)
    return pl.pallas_call(
        _cache_update_kernel,
        in_specs=[any_spec] * 3 + [vmem_spec] * 3,
        out_specs=[any_spec] * 3,
        out_shape=[jax.ShapeDtypeStruct(c.shape, c.dtype) for c in caches],
        scratch_shapes=[
            pltpu.VMEM((CACHE_SLOTS, CACHE_CHUNK_ROWS, 2, H_PER_GROUP, HEAD_DIM), F32),
            pltpu.SemaphoreType.DMA((CACHE_SLOTS,)),
            pltpu.SemaphoreType.DMA((CACHE_SLOTS,)),
            pltpu.SemaphoreType.DMA((N_GROUPS, DEPTH)),
        ],
        compiler_params=pltpu.CompilerParams(vmem_limit_bytes=VMEM_LIMIT),
        name="cache_update",
    )(*caches, *new_rows)


def _attn_sample_kernel(q_ref, k_ref, v_ref, z_ref, c0_ref, c1_ref, c2_ref, bc_ref, bn_ref,
                        o_ref, kv0_ref, kv1_ref, kv2_ref):
    hp = H_PER_GROUP
    kv_refs = (kv0_ref, kv1_ref, kv2_ref)
    k_new, v_new = [], []
    for g in range(N_GROUPS):
        kn = k_ref[:, g * hp:(g + 1) * hp, :]
        vn = v_ref[:, g * hp:(g + 1) * hp, :]
        kv_refs[g][:, 0] = kn
        kv_refs[g][:, 1] = vn
        k_new.append(kn)
        v_new.append(vn)

    for s in range(DEC_SEQ):
        accs, ms, ss = [], [], []
        for g in range(N_GROUPS):
            qg = q_ref[s, g * hp:(g + 1) * hp, :]
            if g == 0:
                kc, vc = c0_ref[:, 0], c0_ref[:, 1]
            elif g == 1:
                kc, vc = c1_ref[:, s, 0], c1_ref[:, s, 1]
            else:
                kc, vc = c2_ref[:, s, 0], c2_ref[:, s, 1]
            lc = jnp.sum(kc * qg[None], axis=-1, keepdims=True) * ATT_SCALE + bc_ref[g, s]
            ln = jnp.sum(k_new[g] * qg[None], axis=-1, keepdims=True) * ATT_SCALE + bn_ref[g, s]
            m = jnp.maximum(jnp.max(lc, axis=0), jnp.max(ln, axis=0))
            pc = jnp.exp(lc - m[None])
            pn = jnp.exp(ln - m[None])
            ss.append(jnp.sum(pc, axis=0) + jnp.sum(pn, axis=0))
            accs.append(jnp.sum(pc * vc, axis=0) + jnp.sum(pn * v_new[g], axis=0))
            ms.append(m)
        mm = jnp.maximum(jnp.maximum(ms[0], ms[1]), ms[2])
        ws = [jnp.exp(m - mm) for m in ms]
        num = ws[0] * accs[0] + ws[1] * accs[1] + ws[2] * accs[2]
        den = ws[0] * ss[0] + ws[1] * ss[1] + ws[2] * ss[2]
        o_ref[s] = (num / den) * _silu(z_ref[s])


def _attn_sample(proj_s4, caches, bias_c, bias_n, layer):
    c0, c1, c2 = caches
    hp = H_PER_GROUP

    def head_spec(col0, n_heads):
        return pl.BlockSpec((None, DEC_SEQ, n_heads, HEAD_DIM),
                            lambda b, c=col0 // (HEAD_DIM * n_heads): (b, 0, c, 0))

    in_specs = [
        head_spec(COL_Q, N_ATT_HEADS), head_spec(COL_K, N_ATT_HEADS), head_spec(COL_V, N_ATT_HEADS),
        head_spec(COL_ZATT, hp),
        pl.BlockSpec((None, None, BLK, 2, hp, HEAD_DIM), lambda b: (layer, b, 0, 0, 0, 0)),
        pl.BlockSpec((None, None, BLK, DEC_SEQ, 2, hp, HEAD_DIM), lambda b: (layer, b, 0, 0, 0, 0, 0)),
        pl.BlockSpec((None, None, BLK, DEC_SEQ, 2, hp, HEAD_DIM), lambda b: (layer, b, 0, 0, 0, 0, 0)),
        pl.BlockSpec(bias_c.shape, lambda b: (0,) * 5),
        pl.BlockSpec(bias_n.shape, lambda b: (0,) * 5),
    ]
    out_specs = [pl.BlockSpec((None, DEC_SEQ, hp, HEAD_DIM), lambda b: (b, 0, 0, 0))]
    out_shape = [jax.ShapeDtypeStruct((DEC_BATCH, DEC_SEQ, hp, HEAD_DIM), F32)]
    for _ in range(N_GROUPS):
        out_specs.append(pl.BlockSpec((None, DEC_SEQ, 2, hp, HEAD_DIM), lambda b: (b, 0, 0, 0, 0)))
        out_shape.append(jax.ShapeDtypeStruct((DEC_BATCH, DEC_SEQ, 2, hp, HEAD_DIM), F32))
    return pl.pallas_call(
        _attn_sample_kernel,
        grid=(DEC_BATCH,),
        in_specs=in_specs,
        out_specs=out_specs,
        out_shape=out_shape,
        compiler_params=_cparams(1),
        name="attn_sample",
    )(proj_s4, proj_s4, proj_s4, proj_s4, c0, c1, c2, bias_c, bias_n)


def _sample_bias(rel_bias):
    s_idx = np.arange(DEC_SEQ)
    bc, bn = [], []
    for g, (win, dil) in enumerate(ATT_GROUPS):
        tbl = rel_bias[:, g * H_PER_GROUP:(g + 1) * H_PER_GROUP].astype(F32)
        key = np.arange(BLK)
        if dil == 1:
            j = BLK + s_idx[:, None] - key[None, :]
            valid = j <= BLK
        else:
            j = np.broadcast_to(BLK - key[None, :], (DEC_SEQ, BLK))
            valid = np.ones_like(j, dtype=bool)
        b = _lookup(tbl, _t5_bucket(np.clip(j, 0, None) * dil))
        bc.append(jnp.where(valid[:, :, None], b, NEG))
        jn = s_idx[:, None] - s_idx[None, :]
        valid_n = (jn >= 0) & (jn * dil <= win) & ((jn == 0) | (dil == 1))
        b = _lookup(tbl, _t5_bucket(np.clip(jn, 0, None) * dil))
        bn.append(jnp.where(valid_n[:, :, None], b, NEG))
    bc = jnp.stack(bc)
    bn = jnp.stack(bn)
    bc = jnp.broadcast_to(bc[..., None], bc.shape + (HEAD_DIM,))
    bn = jnp.broadcast_to(bn[..., None], bn.shape + (HEAD_DIM,))
    return bc, bn


def _lru_sample_kernel(x_ref, z_ref, cs_ref, h0_ref, cw_ref, cb_ref, wr_ref, br_ref, wi_ref, bi_ref,
                       lam_ref, o_ref, conv_ref, hl_ref):
    S = DEC_SEQ
    xp = [cs_ref[t] for t in range(CONV_WIDTH - 1)] + [x_ref[t] for t in range(S)]
    xc = []
    for t in range(S):
        y = xp[t] * cw_ref[0:1, :]
        for tap in range(1, CONV_WIDTH):
            y = y + xp[t + tap] * cw_ref[tap:tap + 1, :]
        xc.append(y + cb_ref[...])
    xcat = jnp.concatenate(xc, axis=0)
    nsp = _softplus(-lam_ref[...])
    a_parts, u_parts = _lru_gates(xcat, wr_ref, br_ref[...], wi_ref, bi_ref[...], nsp, LRU_BLOCKS)
    a = jnp.concatenate(a_parts, axis=1)
    u = jnp.concatenate(u_parts, axis=1)
    nb = DEC_BATCH
    h = h0_ref[...]
    for t in range(S):
        h = a[t * nb:(t + 1) * nb] * h + u[t * nb:(t + 1) * nb]
        o_ref[t] = h * _silu(z_ref[t])
    for t in range(CONV_WIDTH - 1):
        conv_ref[t] = xp[S + t]
    hl_ref[...] = h


def _lru_sample(x_t, z_t, conv_t, h0, conv_w, conv_b, w_r, b_r, w_i, b_i, lam, layer):
    C = LRU_WIDTH

    def full(shape):
        return pl.BlockSpec(shape, lambda i: (0,) * len(shape))

    vec_spec = pl.BlockSpec((None, 1, C), lambda i: (layer, 0, 0))
    w_spec = pl.BlockSpec((None, LRU_BLOCKS, LRU_BLOCK, LRU_BLOCK), lambda i: (layer, 0, 0, 0))
    return pl.pallas_call(
        _lru_sample_kernel,
        grid=(1,),
        in_specs=[
            full((DEC_SEQ, DEC_BATCH, C)), full((DEC_SEQ, DEC_BATCH, C)),
            full((CONV_WIDTH - 1, DEC_BATCH, C)), full((DEC_BATCH, C)),
            pl.BlockSpec((None, CONV_WIDTH, C), lambda i: (layer, 0, 0)),
            vec_spec, w_spec, vec_spec, w_spec, vec_spec, vec_spec,
        ],
        out_specs=[full((DEC_SEQ, DEC_BATCH, C)), full((CONV_WIDTH - 1, DEC_BATCH, C)),
                   full((DEC_BATCH, C))],
        out_shape=[
            jax.ShapeDtypeStruct((DEC_SEQ, DEC_BATCH, C), F32),
            jax.ShapeDtypeStruct((CONV_WIDTH - 1, DEC_BATCH, C), F32),
            jax.ShapeDtypeStruct((DEC_BATCH, C), F32),
        ],
        compiler_params=_cparams(1),
        name="lru_sample",
    )(x_t, z_t, conv_t, h0, conv_w, conv_b, w_r, b_r, w_i, b_i, lam)


def kernel(x_prompt, x_sample, c_prompt, c_sample, cache_kv_g0, cache_kv_g1, cache_kv_g2, state_conv, state_h, rel_bias, w_ada, b_ada, norm_g, w_in, conv_w, conv_b, w_r, b_r, w_i, b_i, lam, w_pa, w_pb, w_out, final_g):
    L, B, T, D = DEPTH, BATCH, SEQ, D_MODEL
    Bd, S = DEC_BATCH, DEC_SEQ

    c_all = jnp.concatenate(
        [c_prompt, c_sample, jnp.zeros((MOD_ROWS - B - Bd, D), F32)], axis=0)
    mod = _modulation(c_all, w_ada, b_ada).reshape(L, MOD_ROWS, 3, D)
    mod_p = mod[:, :B]
    mod_s = jnp.repeat(mod[:, B:B + Bd], S, axis=1)

    conv_b3 = conv_b.reshape(L, 1, LRU_WIDTH)
    b_r3 = b_r.reshape(L, 1, LRU_WIDTH)
    b_i3 = b_i.reshape(L, 1, LRU_WIDTH)
    lam3 = lam.reshape(L, 1, LRU_WIDTH)

    bias_p = _prompt_bias(rel_bias)
    bias_c, bias_n = _sample_bias(rel_bias)

    cache_views = (
        cache_kv_g0,
        cache_kv_g1.reshape(L, Bd, BLK, 4, 2, H_PER_GROUP, HEAD_DIM),
        cache_kv_g2.reshape(L, Bd, BLK, 16, 2, H_PER_GROUP, HEAD_DIM),
    )

    xp = x_prompt
    xs = x_sample.reshape(Bd * S, D)
    kv_p = None
    kv_new = ([], [], [])
    conv_p, h_p, conv_s, h_s = [], [], [], []

    def mods(l):
        shift_p, scale_p, gate_p = (mod_p[l, :, i][:, None, :] for i in range(3))
        shift_s, scale_s, gate_s = (mod_s[l, :, i] for i in range(3))
        return (shift_p, scale_p, gate_p), (shift_s, scale_s, gate_s)

    (shift_p, scale_p, gate_p), (shift_s, scale_s, gate_s) = mods(0)
    hp_ = _norm_prompt(xp, norm_g[0], scale_p, shift_p)
    hs_ = _norm_sample(xs, norm_g[0], scale_s, shift_s)
    for l in range(L):
        proj_p, proj_s, *kv_p, wpa_bf, wpb_bf = _in_proj(hp_.reshape(B * T, D), hs_, w_in,
                                                         w_pa, w_pb, kv_p, l)
        proj3 = proj_p.reshape(B, T, IN_COLS)

        att_p = _attn_prompt(proj3, bias_p)
        lru_p, cv, hl = _lru_prompt(proj3, conv_w, conv_b3, w_r, b_r3, w_i, b_i3, lam3, l)
        merged_p = _mix(att_p.reshape(B * T, ATT_WIDTH), lru_p.reshape(B * T, LRU_WIDTH),
                        proj_p, wpa_bf, wpb_bf, 512)
        last = l == L - 1
        if not last:
            (shift_pn, scale_pn, gate_pn), (shift_sn, scale_sn, gate_sn) = mods(l + 1)
            xp, hp_ = _residual_prompt(merged_p.reshape(B, T, D), w_out, xp, gate_p,
                                       norm_g[l + 1], scale_pn, shift_pn, l)
        else:
            y_prompt = _residual_prompt(merged_p.reshape(B, T, D), w_out, xp, gate_p,
                                        final_g, None, None, l)
        conv_p.append(cv)
        h_p.append(hl.reshape(B, LRU_WIDTH))

        proj_s4 = proj_s.reshape(Bd, S, IN_COLS // HEAD_DIM, HEAD_DIM)
        outs = _attn_sample(proj_s4, cache_views, bias_c, bias_n, l)
        att_s = outs[0]
        for g in range(N_GROUPS):
            kv_new[g].append(outs[1 + g])
        ps3 = proj_s.reshape(Bd, S, IN_COLS)
        x_t = jnp.transpose(ps3[:, :, COL_XLRU:COL_XLRU + LRU_WIDTH], (1, 0, 2))
        z_t = jnp.transpose(ps3[:, :, COL_ZLRU:COL_ZLRU + LRU_WIDTH], (1, 0, 2))
        conv_t = jnp.transpose(state_conv[l], (1, 0, 2))
        lru_t, cv_t, hl_s = _lru_sample(x_t, z_t, conv_t, state_h[l], conv_w, conv_b3,
                                        w_r, b_r3, w_i, b_i3, lam3, l)
        lru_s = jnp.transpose(lru_t, (1, 0, 2)).reshape(Bd * S, LRU_WIDTH).astype(BF16)
        merged_s = _mix(att_s.reshape(Bd * S, ATT_WIDTH).astype(BF16), lru_s, proj_s,
                        wpa_bf, wpb_bf, Bd * S)
        if not last:
            xs, hs_ = _residual_sample(merged_s, w_out, xs, gate_s,
                                       norm_g[l + 1], scale_sn, shift_sn, l)
            gate_p, gate_s = gate_pn, gate_sn
        else:
            y_sample = _residual_sample(merged_s, w_out, xs, gate_s,
                                        final_g, None, None, l).reshape(Bd, S, D)
        conv_s.append(jnp.transpose(cv_t, (1, 0, 2)))
        h_s.append(hl_s)

    kv_s = _cache_update((cache_kv_g0, cache_kv_g1, cache_kv_g2),
                         tuple(jnp.stack(rows) for rows in kv_new))
    return (y_prompt, y_sample,
            kv_p[0], kv_p[1], kv_p[2],
            jnp.stack(conv_p), jnp.stack(h_p),
            kv_s[0], kv_s[1], kv_s[2],
            jnp.stack(conv_s), jnp.stack(h_s))
```
